```python
import jax, jax.numpy as jnp
from jax import lax
import numpy as np

D_MODEL = 1024
BATCH = 8
SEQ = 2048
DEPTH = 1
DEC_BATCH = 128
DEC_SEQ = 4
PAST_LEN = 16384
PAGE_SIZE = 128

CHUNK = 128
A_WIDTH = D_MODEL
A_GROUPS = 8
A_GROUP_DIM = A_WIDTH // A_GROUPS
B_WIDTH = D_MODEL
CONV_WIDTH = 3
PROJ_WIDTH = 3 * A_WIDTH + 4 * B_WIDTH + 2 * D_MODEL
DEEPNORM_ALPHA = (2.0 * DEPTH) ** 0.25
DEEPNORM_BETA = (8.0 * DEPTH) ** -0.25
LN_EPS = 1e-5
COND_SCALE = 0.3

kernel_name = "gated_parallel_chunkmlp_shortconv_step"

_SPLITS = tuple(int(i) for i in np.cumsum(
    [A_WIDTH, A_WIDTH, A_WIDTH, B_WIDTH, B_WIDTH, B_WIDTH, B_WIDTH, D_MODEL]))


def _layer_norm(x, gain, bias):
    xf = x.astype(jnp.float32)
    mu = jnp.mean(xf, axis=-1, keepdims=True)
    var = jnp.mean(jnp.square(xf - mu), axis=-1, keepdims=True)
    y = (xf - mu) * lax.rsqrt(var + LN_EPS)
    return (y * gain.astype(jnp.float32) + bias.astype(jnp.float32)).astype(x.dtype)


def _chunk_spatial_gate(v, w_s, b_s):
    n, length, _ = v.shape
    n_chunks = -(-length // CHUNK)
    pad = n_chunks * CHUNK - length
    vp = jnp.pad(v, ((0, 0), (0, pad), (0, 0)))
    vp = vp.reshape(n, n_chunks, CHUNK, A_GROUPS, A_GROUP_DIM)
    causal = jnp.tril(jnp.ones((CHUNK, CHUNK), dtype=bool))
    w = jnp.where(causal, w_s, 0)
    s = jnp.einsum('gts,ncsgd->nctgd', w, vp) + b_s.T[None, None, :, :, None]
    return s.reshape(n, n_chunks * CHUNK, A_WIDTH)[:, :length]


def _short_conv(z, prev, w_conv, b_conv):
    length = z.shape[1]
    zp = jnp.concatenate([prev, z], axis=1)
    y = b_conv
    for k in range(CONV_WIDTH):
        y = y + w_conv[k] * zp[:, k:k + length]
    return y, zp[:, length:]


def _layer(x, c, conv_prev, w_c, b_c, w_in, a_ln_g, a_ln_b, w_s, b_s,
           w_conv, b_conv, w_pa, w_pb, w_o, ln_g, ln_b):
    mod = c @ w_c + b_c
    shift, scale, gate = jnp.split(mod[:, None, :], 3, axis=-1)
    h = x * (1 + scale) + shift
    proj = h @ w_in
    u, v, z_a, b_g, c_g, h_b, z_b, g_a, g_b = jnp.split(proj, _SPLITS, axis=-1)
    v = _layer_norm(v, a_ln_g, a_ln_b)
    s = _chunk_spatial_gate(v, w_s, b_s)
    out_a = u * s * jax.nn.silu(z_a)
    conv_out, new_conv = _short_conv(c_g * h_b, conv_prev, w_conv, b_conv)
    out_b = b_g * conv_out * jax.nn.silu(z_b)
    merged = jax.nn.sigmoid(g_a) * (out_a @ w_pa) + jax.nn.sigmoid(g_b) * (out_b @ w_pb)
    y = merged @ w_o
    x_new = _layer_norm(DEEPNORM_ALPHA * x + (1 + gate) * y, ln_g, ln_b)
    return x_new, new_conv, v


def setup_inputs(seed: int = 0) -> dict:
    key = jax.random.key(seed)
    ks = jax.random.split(key, 24)
    nrm = lambda k, shape: jax.random.normal(k, shape, dtype=jnp.float32)
    d = D_MODEL
    return {
        "x_prompt": nrm(ks[0], (BATCH, SEQ, d)),
        "x_sample": nrm(ks[1], (DEC_BATCH, DEC_SEQ, d)),
        "state_conv": nrm(ks[2], (DEPTH, DEC_BATCH, CONV_WIDTH - 1, B_WIDTH)),
        "c_prompt": nrm(ks[3], (BATCH, d)),
        "c_sample": nrm(ks[4], (DEC_BATCH, d)),
        "w_c": nrm(ks[5], (DEPTH, d, 3 * d)) * (d ** -0.5) * COND_SCALE,
        "b_c": nrm(ks[6], (DEPTH, 3 * d)) * 0.02,
        "w_in": nrm(ks[7], (DEPTH, d, PROJ_WIDTH)) * (d ** -0.5),
        "a_ln_g": 1.0 + 0.02 * nrm(ks[8], (DEPTH, A_WIDTH)),
        "a_ln_b": 0.02 * nrm(ks[9], (DEPTH, A_WIDTH)),
        "w_s": nrm(ks[10], (DEPTH, A_GROUPS, CHUNK, CHUNK)) * (CHUNK ** -0.5),
        "b_s": 1.0 + 0.02 * nrm(ks[11], (DEPTH, A_GROUPS, CHUNK)),
        "w_conv": nrm(ks[12], (DEPTH, CONV_WIDTH, B_WIDTH)) * (CONV_WIDTH ** -0.5),
        "b_conv": 0.02 * nrm(ks[13], (DEPTH, B_WIDTH)),
        "w_pa": nrm(ks[14], (DEPTH, A_WIDTH, d)) * (A_WIDTH ** -0.5) * DEEPNORM_BETA,
        "w_pb": nrm(ks[15], (DEPTH, B_WIDTH, d)) * (B_WIDTH ** -0.5) * DEEPNORM_BETA,
        "w_o": nrm(ks[16], (DEPTH, d, d)) * (d ** -0.5) * DEEPNORM_BETA,
        "ln_g": 1.0 + 0.02 * nrm(ks[17], (DEPTH, d)),
        "ln_b": 0.02 * nrm(ks[18], (DEPTH, d)),
    }


def reference(x_prompt, x_sample, state_conv, c_prompt, c_sample, w_c, b_c, w_in,
              a_ln_g, a_ln_b, w_s, b_s, w_conv, b_conv, w_pa, w_pb, w_o, ln_g, ln_b):
    xp, xs = x_prompt, x_sample
    conv_prompt_rows, conv_sample_rows, chunk_v_rows = [], [], []
    zero_conv = jnp.zeros((xp.shape[0], CONV_WIDTH - 1, B_WIDTH), dtype=xp.dtype)
    for l in range(DEPTH):
        params = (w_c[l], b_c[l], w_in[l], a_ln_g[l], a_ln_b[l], w_s[l], b_s[l],
                  w_conv[l], b_conv[l], w_pa[l], w_pb[l], w_o[l], ln_g[l], ln_b[l])
        xp, conv_p, _ = _layer(xp, c_prompt, zero_conv, *params)
        xs, conv_s, v_s = _layer(xs, c_sample, state_conv[l], *params)
        conv_prompt_rows.append(conv_p)
        conv_sample_rows.append(conv_s)
        chunk_v_rows.append(v_s)
    new_conv_prompt = jnp.stack(conv_prompt_rows)
    new_conv_sample = jnp.stack(conv_sample_rows)
    new_chunk_v_sample = jnp.stack(chunk_v_rows)
    return (xp, xs, new_conv_prompt, new_conv_sample, new_chunk_v_sample)
```

```python
import functools

import jax
import jax.numpy as jnp
from jax import lax
from jax.experimental import pallas as pl
from jax.experimental.pallas import tpu as pltpu

CHUNK = 128
GROUPS = 8
CONV_WIDTH = 3
LN_EPS = 1e-5
ROW_TILE = 256
VMEM_LIMIT_BYTES = 56 * 1024 * 1024

_F32 = jnp.float32
_BF16 = jnp.bfloat16


def _dot(a, b):
    return jnp.dot(a, b, preferred_element_type=_F32)


def _layer_norm(x, gain, bias):
    mu = jnp.mean(x, axis=-1, keepdims=True)
    xc = x - mu
    var = jnp.mean(xc * xc, axis=-1, keepdims=True)
    return xc * lax.rsqrt(var + LN_EPS) * gain + bias


def _silu(x):
    return x * jax.nn.sigmoid(x)


def _mod_kernel(c_ref, wc_ref, bc_ref, o_ref):
    o_ref[...] = _dot(c_ref[...].astype(_BF16), wc_ref[...].astype(_BF16)) + bc_ref[...]


def _mod_call(c_all, w_c, b_c):
    n, d = c_all.shape
    d3 = w_c.shape[1]
    return pl.pallas_call(
        _mod_kernel,
        grid=(d3 // d,),
        in_specs=[
            pl.BlockSpec((n, d), lambda i: (0, 0)),
            pl.BlockSpec((d, d), lambda i: (0, i)),
            pl.BlockSpec((1, d), lambda i: (0, i)),
        ],
        out_specs=pl.BlockSpec((n, d), lambda i: (0, i)),
        out_shape=jax.ShapeDtypeStruct((n, d3), _F32),
        compiler_params=pltpu.CompilerParams(dimension_semantics=("arbitrary",)),
        name="cond_proj",
    )(c_all, w_c, b_c.reshape(1, d3))


def _merge_and_norm(x, gate, out_a, out_b, g_a, g_b, wpa_ref, wpb_ref, wo_ref, lng, lnb, alpha):
    pa = _dot(out_a, wpa_ref[...])
    pb = _dot(out_b, wpb_ref[...])
    merged = (jax.nn.sigmoid(g_a) * pa + jax.nn.sigmoid(g_b) * pb).astype(_BF16)
    y = _dot(merged, wo_ref[...])
    return _layer_norm(alpha * x + (1.0 + gate) * y, lng, lnb)


def _prompt_kernel(x_ref, mod_ref, win_ref, wpa_ref, wpb_ref, wo_ref, ws_ref, bst_ref,
                   alg_ref, alb_ref, wconv_ref, bconv_ref, lng_ref, lnb_ref,
                   y_ref, conv_ref, wsm_scr, bias_scr, z_scr, *, alpha):
    tm, d = x_ref.shape[1], x_ref.shape[2]
    gd = d // GROUPS
    j = pl.program_id(1)

    @pl.when((pl.program_id(0) == 0) & (j == 0))
    def _():
        row = lax.broadcasted_iota(jnp.int32, (CHUNK, CHUNK), 0)
        col = lax.broadcasted_iota(jnp.int32, (CHUNK, CHUNK), 1)
        for g in range(GROUPS):
            wsm_scr[g] = jnp.where(col <= row, ws_ref[g], 0.0).astype(_BF16)
            bias_scr[:, g * gd:(g + 1) * gd] = jnp.broadcast_to(bst_ref[:, g:g + 1], (CHUNK, gd))

    @pl.when(j == 0)
    def _():
        z_scr[0:8, :] = jnp.zeros((8, d), _F32)

    x = x_ref[0]
    mod = mod_ref[0]
    shift, scale, gate = mod[:, 0:d], mod[:, d:2 * d], mod[:, 2 * d:3 * d]
    h = (x * (1.0 + scale) + shift).astype(_BF16)

    def proj(k):
        return _dot(h, win_ref[:, k * d:(k + 1) * d])

    vn = _layer_norm(proj(1), alg_ref[...], alb_ref[...])
    vb = vn.astype(_BF16)
    bias = bias_scr[...]
    s_cols = []
    for g in range(GROUPS):
        cols = slice(g * gd, (g + 1) * gd)
        pair_rows = []
        for c in range(0, tm // CHUNK, 2):
            rhs = jnp.concatenate([vb[c * CHUNK:(c + 1) * CHUNK, cols],
                                   vb[(c + 1) * CHUNK:(c + 2) * CHUNK, cols]], axis=1)
            res = _dot(wsm_scr[g], rhs)
            pair_rows.append(res[:, :gd] + bias[:, cols])
            pair_rows.append(res[:, gd:] + bias[:, cols])
        s_cols.append(jnp.concatenate(pair_rows, axis=0))
    s = jnp.concatenate(s_cols, axis=1)
    out_a = (proj(0) * s * _silu(proj(2))).astype(_BF16)

    z = proj(4) * proj(5)
    z_scr[8:8 + tm, :] = z
    wconv = wconv_ref[...]
    conv = (bconv_ref[...] + wconv[0:1, :] * z_scr[6:6 + tm, :]
            + wconv[1:2, :] * z_scr[7:7 + tm, :] + wconv[2:3, :] * z)
    last = z[tm - (CONV_WIDTH - 1):tm, :]
    z_scr[6:8, :] = last
    conv_ref[0] = last
    out_b = (proj(3) * conv * _silu(proj(6))).astype(_BF16)

    y_ref[0] = _merge_and_norm(x, gate, out_a, out_b, proj(7), proj(8), wpa_ref, wpb_ref, wo_ref,
                               lng_ref[...], lnb_ref[...], alpha)


def _resident(shape):
    nd = len(shape)
    return pl.BlockSpec(shape, lambda b, j: (0,) * nd, pipeline_mode=pl.Buffered(1))


def _prompt_call(x, mod, win, wpa, wpb, wo, w_s, b_s_t, alg, alb, wconv, bconv, lng, lnb, alpha):
    nb, length, d = x.shape
    tm = ROW_TILE
    assert length % tm == 0 and tm % (2 * CHUNK) == 0 and d % GROUPS == 0 and d // GROUPS == CHUNK
    small = [alg, alb, wconv, bconv, lng, lnb]
    return pl.pallas_call(
        functools.partial(_prompt_kernel, alpha=alpha),
        grid=(nb, length // tm),
        in_specs=[
            pl.BlockSpec((1, tm, d), lambda b, j: (b, j, 0)),
            pl.BlockSpec((1, 1, 3 * d), lambda b, j: (b, 0, 0)),
            _resident(win.shape), _resident(wpa.shape), _resident(wpb.shape), _resident(wo.shape),
            _resident(w_s.shape), _resident(b_s_t.shape),
        ] + [_resident(a.shape) for a in small],
        out_specs=[
            pl.BlockSpec((1, tm, d), lambda b, j: (b, j, 0)),
            pl.BlockSpec((1, CONV_WIDTH - 1, d), lambda b, j: (b, 0, 0)),
        ],
        out_shape=[
            jax.ShapeDtypeStruct((nb, length, d), _F32),
            jax.ShapeDtypeStruct((nb, CONV_WIDTH - 1, d), _F32),
        ],
        scratch_shapes=[
            pltpu.VMEM((GROUPS, CHUNK, CHUNK), _BF16),
            pltpu.VMEM((CHUNK, d), _F32),
            pltpu.VMEM((8 + tm, d), _F32),
        ],
        compiler_params=pltpu.CompilerParams(
            dimension_semantics=("arbitrary", "arbitrary"),
            vmem_limit_bytes=VMEM_LIMIT_BYTES),
        name="prompt_layer",
    )(x, mod.reshape(nb, 1, 3 * d), win, wpa, wpb, wo, w_s, b_s_t, *small)


def _sample_kernel(ws4_ref, bs4_ref, x_ref, mod_ref, st_ref, win_ref, wpa_ref, wpb_ref, wo_ref,
                   alg_ref, alb_ref, wconv_ref, bconv_ref, lng_ref, lnb_ref,
                   y_ref, conv_ref, v_ref, *, alpha, steps):
    n = x_ref.shape[0]
    d = x_ref.shape[1] // steps
    gd = d // GROUPS

    def rows(a, t):
        return a[t * n:(t + 1) * n, :]

    mod = mod_ref[...]
    shift, scale, gate = mod[:, 0:d], mod[:, d:2 * d], mod[:, 2 * d:3 * d]
    xs = [x_ref[:, t * d:(t + 1) * d] for t in range(steps)]
    x = jnp.concatenate(xs, axis=0)
    h = jnp.concatenate([(xt * (1.0 + scale) + shift).astype(_BF16) for xt in xs], axis=0)
    gate = jnp.concatenate([gate] * steps, axis=0)

    def proj(k):
        return _dot(h, win_ref[:, k * d:(k + 1) * d])

    vn = _layer_norm(proj(1), alg_ref[...], alb_ref[...])
    s_rows = []
    for t in range(steps):
        v_ref[:, t * d:(t + 1) * d] = rows(vn, t)
        s_cols = []
        for g in range(GROUPS):
            cols = slice(g * gd, (g + 1) * gd)
            acc = jnp.full((n, gd), bs4_ref[g, t], _F32)
            for jj in range(t + 1):
                acc = acc + ws4_ref[g, t * steps + jj] * rows(vn, jj)[:, cols]
            s_cols.append(acc)
        s_rows.append(jnp.concatenate(s_cols, axis=1))
    s = jnp.concatenate(s_rows, axis=0)
    out_a = (proj(0) * s * _silu(proj(2))).astype(_BF16)

    z = proj(4) * proj(5)
    hist = [st_ref[:, k * d:(k + 1) * d] for k in range(CONV_WIDTH - 1)] + [rows(z, t) for t in range(steps)]
    wconv = wconv_ref[...]
    conv = jnp.concatenate(
        [bconv_ref[...] + sum(wconv[k:k + 1, :] * hist[t + k] for k in range(CONV_WIDTH))
         for t in range(steps)], axis=0)
    for k in range(CONV_WIDTH - 1):
        conv_ref[:, k * d:(k + 1) * d] = hist[steps + k]
    out_b = (proj(3) * conv * _silu(proj(6))).astype(_BF16)

    y = _merge_and_norm(x, gate, out_a, out_b, proj(7), proj(8), wpa_ref, wpb_ref, wo_ref,
                        lng_ref[...], lnb_ref[...], alpha)
    for t in range(steps):
        y_ref[:, t * d:(t + 1) * d] = rows(y, t)


def _sample_call(x, mod, state, win, wpa, wpb, wo, ws4, bs4, alg, alb, wconv, bconv, lng, lnb, alpha):
    n, steps, d = x.shape
    assert steps <= CHUNK and steps >= CONV_WIDTH - 1
    vmem = pl.BlockSpec(memory_space=pltpu.VMEM)
    smem = pl.BlockSpec(memory_space=pltpu.SMEM)
    y, conv, v = pl.pallas_call(
        functools.partial(_sample_kernel, alpha=alpha, steps=steps),
        in_specs=[smem, smem] + [vmem] * 13,
        out_specs=[vmem, vmem, vmem],
        out_shape=[
            jax.ShapeDtypeStruct((n, steps * d), _F32),
            jax.ShapeDtypeStruct((n, (CONV_WIDTH - 1) * d), _F32),
            jax.ShapeDtypeStruct((n, steps * d), _F32),
        ],
        compiler_params=pltpu.CompilerParams(vmem_limit_bytes=VMEM_LIMIT_BYTES),
        name="sample_layer",
    )(ws4, bs4, x.reshape(n, steps * d), mod, state.reshape(n, (CONV_WIDTH - 1) * d),
      win, wpa, wpb, wo, alg, alb, wconv, bconv, lng, lnb)
    return (y.reshape(n, steps, d), conv.reshape(n, CONV_WIDTH - 1, d), v.reshape(n, steps, d))


def kernel(x_prompt, x_sample, state_conv, c_prompt, c_sample, w_c, b_c, w_in, a_ln_g, a_ln_b,
           w_s, b_s, w_conv, b_conv, w_pa, w_pb, w_o, ln_g, ln_b):
    depth = w_in.shape[0]
    d = x_prompt.shape[-1]
    nb = x_prompt.shape[0]
    steps = x_sample.shape[1]
    alpha = (2.0 * depth) ** 0.25
    c_all = jnp.concatenate([c_prompt, c_sample], axis=0)

    xp, xs = x_prompt, x_sample
    conv_p_rows, conv_s_rows, v_rows = [], [], []
    for l in range(depth):
        mod = _mod_call(c_all, w_c[l], b_c[l])
        win, wpa, wpb, wo = (w.astype(_BF16) for w in (w_in[l], w_pa[l], w_pb[l], w_o[l]))
        row = lambda a: a.reshape(1, d)
        small = (row(a_ln_g[l]), row(a_ln_b[l]), w_conv[l], row(b_conv[l]), row(ln_g[l]), row(ln_b[l]))
        xp, conv_p = _prompt_call(xp, mod[:nb], win, wpa, wpb, wo, w_s[l], b_s[l].T, *small, alpha)
        ws4 = w_s[l][:, :steps, :steps].reshape(GROUPS, steps * steps)
        bs4 = b_s[l][:, :steps]
        xs, conv_s, v_s = _sample_call(xs, mod[nb:], state_conv[l], win, wpa, wpb, wo, ws4, bs4,
                                       *small, alpha)
        conv_p_rows.append(conv_p)
        conv_s_rows.append(conv_s)
        v_rows.append(v_s)
    return (xp, xs, jnp.stack(conv_p_rows), jnp.stack(conv_s_rows), jnp.stack(v_rows))
```

```python
import functools

import jax
import jax.numpy as jnp
from jax import lax
from jax.experimental import pallas as pl
from jax.experimental.pallas import tpu as pltpu

CHUNK = 128
GROUPS = 8
CONV_WIDTH = 3
LN_EPS = 1e-5
ROW_TILE = 256
VMEM_LIMIT_BYTES = 56 * 1024 * 1024

_F32 = jnp.float32
_BF16 = jnp.bfloat16


def _dot(a, b):
    return jnp.dot(a, b, preferred_element_type=_F32)


def _layer_norm(x, gain, bias):
    mu = jnp.mean(x, axis=-1, keepdims=True)
    xc = x - mu
    var = jnp.mean(xc * xc, axis=-1, keepdims=True)
    return xc * lax.rsqrt(var + LN_EPS) * gain + bias


def _silu(x):
    return x * jax.nn.sigmoid(x)


def _stage_weights_bf16(srcs, dsts, stage, sem):
    d = stage.shape[1]
    blocks = [(src, dst, k) for src, dst in zip(srcs, dsts) for k in range(src.shape[1] // d)]

    def copy(i):
        src, _, k = blocks[i]
        return pltpu.make_async_copy(src.at[:, k * d:(k + 1) * d], stage.at[i % 2], sem.at[i % 2])

    copy(0).start()
    for i, (_, dst, k) in enumerate(blocks):
        if i + 1 < len(blocks):
            copy(i + 1).start()
        copy(i).wait()

        def cast_rows(r, carry, dst=dst, k=k, slot=i % 2):
            rows = pl.ds(pl.multiple_of(r * CHUNK, CHUNK), CHUNK)
            dst[rows, k * d:(k + 1) * d] = stage[slot, rows, :].astype(_BF16)
            return carry

        lax.fori_loop(0, d // CHUNK, cast_rows, 0)


def _mod_kernel(c_ref, wc_ref, bc_ref, o_ref):
    o_ref[...] = _dot(c_ref[...].astype(_BF16), wc_ref[...].astype(_BF16)) + bc_ref[...]


def _mod_call(c_all, w_c, b_c):
    n, d = c_all.shape
    d3 = w_c.shape[1]
    return pl.pallas_call(
        _mod_kernel,
        grid=(d3 // d,),
        in_specs=[
            pl.BlockSpec((n, d), lambda i: (0, 0)),
            pl.BlockSpec((d, d), lambda i: (0, i)),
            pl.BlockSpec((1, d), lambda i: (0, i)),
        ],
        out_specs=pl.BlockSpec((n, d), lambda i: (0, i)),
        out_shape=jax.ShapeDtypeStruct((n, d3), _F32),
        compiler_params=pltpu.CompilerParams(dimension_semantics=("arbitrary",)),
        name="cond_proj",
    )(c_all, w_c, b_c.reshape(1, d3))


def _merge_and_norm(x, gate, out_a, out_b, g_a, g_b, wpa_ref, wpb_ref, wo_ref, lng, lnb, alpha):
    pa = _dot(out_a, wpa_ref[...])
    pb = _dot(out_b, wpb_ref[...])
    merged = (jax.nn.sigmoid(g_a) * pa + jax.nn.sigmoid(g_b) * pb).astype(_BF16)
    y = _dot(merged, wo_ref[...])
    return _layer_norm(alpha * x + (1.0 + gate) * y, lng, lnb)


def _prompt_kernel(x_ref, mod_ref, win_hbm, wpa_hbm, wpb_hbm, wo_hbm, ws_ref, bst_ref,
                   alg_ref, alb_ref, wconv_ref, bconv_ref, lng_ref, lnb_ref,
                   y_ref, conv_ref, win_ref, wpa_ref, wpb_ref, wo_ref, stage, sem,
                   wsm_scr, bias_scr, z_scr, *, alpha):
    tm, d = x_ref.shape[1], x_ref.shape[2]
    gd = d // GROUPS
    j = pl.program_id(1)

    @pl.when((pl.program_id(0) == 0) & (j == 0))
    def _():
        _stage_weights_bf16((win_hbm, wpa_hbm, wpb_hbm, wo_hbm), (win_ref, wpa_ref, wpb_ref, wo_ref),
                            stage, sem)
        row = lax.broadcasted_iota(jnp.int32, (CHUNK, CHUNK), 0)
        col = lax.broadcasted_iota(jnp.int32, (CHUNK, CHUNK), 1)
        for g in range(GROUPS):
            wsm_scr[g] = jnp.where(col <= row, ws_ref[g], 0.0).astype(_BF16)
            bias_scr[:, g * gd:(g + 1) * gd] = jnp.broadcast_to(bst_ref[:, g:g + 1], (CHUNK, gd))

    @pl.when(j == 0)
    def _():
        z_scr[0:8, :] = jnp.zeros((8, d), _F32)

    x = x_ref[0]
    mod = mod_ref[0]
    shift, scale, gate = mod[:, 0:d], mod[:, d:2 * d], mod[:, 2 * d:3 * d]
    h = (x * (1.0 + scale) + shift).astype(_BF16)

    def proj(k):
        return _dot(h, win_ref[:, k * d:(k + 1) * d])

    vn = _layer_norm(proj(1), alg_ref[...], alb_ref[...])
    vb = vn.astype(_BF16)
    bias = bias_scr[...]
    s_cols = []
    for g in range(GROUPS):
        cols = slice(g * gd, (g + 1) * gd)
        pair_rows = []
        for c in range(0, tm // CHUNK, 2):
            rhs = jnp.concatenate([vb[c * CHUNK:(c + 1) * CHUNK, cols],
                                   vb[(c + 1) * CHUNK:(c + 2) * CHUNK, cols]], axis=1)
            res = _dot(wsm_scr[g], rhs)
            pair_rows.append(res[:, :gd] + bias[:, cols])
            pair_rows.append(res[:, gd:] + bias[:, cols])
        s_cols.append(jnp.concatenate(pair_rows, axis=0))
    s = jnp.concatenate(s_cols, axis=1)
    out_a = (proj(0) * s * _silu(proj(2))).astype(_BF16)

    z = proj(4) * proj(5)
    z_scr[8:8 + tm, :] = z
    wconv = wconv_ref[...]
    conv = (bconv_ref[...] + wconv[0:1, :] * z_scr[6:6 + tm, :]
            + wconv[1:2, :] * z_scr[7:7 + tm, :] + wconv[2:3, :] * z)
    last = z[tm - (CONV_WIDTH - 1):tm, :]
    z_scr[6:8, :] = last
    conv_ref[0] = last
    out_b = (proj(3) * conv * _silu(proj(6))).astype(_BF16)

    y_ref[0] = _merge_and_norm(x, gate, out_a, out_b, proj(7), proj(8), wpa_ref, wpb_ref, wo_ref,
                               lng_ref[...], lnb_ref[...], alpha)


def _weight_scratch(*weights):
    d = weights[0].shape[0]
    return ([pltpu.VMEM(w.shape, _BF16) for w in weights]
            + [pltpu.VMEM((2, d, d), _F32), pltpu.SemaphoreType.DMA((2,))])


def _resident(shape):
    nd = len(shape)
    return pl.BlockSpec(shape, lambda b, j: (0,) * nd, pipeline_mode=pl.Buffered(1))


def _prompt_call(x, mod, win, wpa, wpb, wo, w_s, b_s_t, alg, alb, wconv, bconv, lng, lnb, alpha):
    nb, length, d = x.shape
    tm = ROW_TILE
    assert length % tm == 0 and tm % (2 * CHUNK) == 0 and d % GROUPS == 0 and d // GROUPS == CHUNK
    small = [alg, alb, wconv, bconv, lng, lnb]
    hbm = pl.BlockSpec(memory_space=pl.ANY)
    return pl.pallas_call(
        functools.partial(_prompt_kernel, alpha=alpha),
        grid=(nb, length // tm),
        in_specs=[
            pl.BlockSpec((1, tm, d), lambda b, j: (b, j, 0)),
            pl.BlockSpec((1, 1, 3 * d), lambda b, j: (b, 0, 0)),
            hbm, hbm, hbm, hbm,
            _resident(w_s.shape), _resident(b_s_t.shape),
        ] + [_resident(a.shape) for a in small],
        out_specs=[
            pl.BlockSpec((1, tm, d), lambda b, j: (b, j, 0)),
            pl.BlockSpec((1, CONV_WIDTH - 1, d), lambda b, j: (b, 0, 0)),
        ],
        out_shape=[
            jax.ShapeDtypeStruct((nb, length, d), _F32),
            jax.ShapeDtypeStruct((nb, CONV_WIDTH - 1, d), _F32),
        ],
        scratch_shapes=_weight_scratch(win, wpa, wpb, wo) + [
            pltpu.VMEM((GROUPS, CHUNK, CHUNK), _BF16),
            pltpu.VMEM((CHUNK, d), _F32),
            pltpu.VMEM((8 + tm, d), _F32),
        ],
        compiler_params=pltpu.CompilerParams(
            dimension_semantics=("arbitrary", "arbitrary"),
            vmem_limit_bytes=VMEM_LIMIT_BYTES),
        name="prompt_layer",
    )(x, mod.reshape(nb, 1, 3 * d), win, wpa, wpb, wo, w_s, b_s_t, *small)


def _sample_kernel(ws4_ref, bs4_ref, x_ref, mod_ref, st_ref, win_hbm, wpa_hbm, wpb_hbm, wo_hbm,
                   alg_ref, alb_ref, wconv_ref, bconv_ref, lng_ref, lnb_ref,
                   y_ref, conv_ref, v_ref, win_ref, wpa_ref, wpb_ref, wo_ref, stage, sem,
                   *, alpha, steps):
    _stage_weights_bf16((win_hbm, wpa_hbm, wpb_hbm, wo_hbm), (win_ref, wpa_ref, wpb_ref, wo_ref),
                        stage, sem)
    n = x_ref.shape[0]
    d = x_ref.shape[1] // steps
    gd = d // GROUPS

    def rows(a, t):
        return a[t * n:(t + 1) * n, :]

    mod = mod_ref[...]
    shift, scale, gate = mod[:, 0:d], mod[:, d:2 * d], mod[:, 2 * d:3 * d]
    xs = [x_ref[:, t * d:(t + 1) * d] for t in range(steps)]
    x = jnp.concatenate(xs, axis=0)
    h = jnp.concatenate([(xt * (1.0 + scale) + shift).astype(_BF16) for xt in xs], axis=0)
    gate = jnp.concatenate([gate] * steps, axis=0)

    def proj(k):
        return _dot(h, win_ref[:, k * d:(k + 1) * d])

    vn = _layer_norm(proj(1), alg_ref[...], alb_ref[...])
    s_rows = []
    for t in range(steps):
        v_ref[:, t * d:(t + 1) * d] = rows(vn, t)
        s_cols = []
        for g in range(GROUPS):
            cols = slice(g * gd, (g + 1) * gd)
            acc = jnp.full((n, gd), bs4_ref[g, t], _F32)
            for jj in range(t + 1):
                acc = acc + ws4_ref[g, t * steps + jj] * rows(vn, jj)[:, cols]
            s_cols.append(acc)
        s_rows.append(jnp.concatenate(s_cols, axis=1))
    s = jnp.concatenate(s_rows, axis=0)
    out_a = (proj(0) * s * _silu(proj(2))).astype(_BF16)

    z = proj(4) * proj(5)
    hist = [st_ref[:, k * d:(k + 1) * d] for k in range(CONV_WIDTH - 1)] + [rows(z, t) for t in range(steps)]
    wconv = wconv_ref[...]
    conv = jnp.concatenate(
        [bconv_ref[...] + sum(wconv[k:k + 1, :] * hist[t + k] for k in range(CONV_WIDTH))
         for t in range(steps)], axis=0)
    for k in range(CONV_WIDTH - 1):
        conv_ref[:, k * d:(k + 1) * d] = hist[steps + k]
    out_b = (proj(3) * conv * _silu(proj(6))).astype(_BF16)

    y = _merge_and_norm(x, gate, out_a, out_b, proj(7), proj(8), wpa_ref, wpb_ref, wo_ref,
                        lng_ref[...], lnb_ref[...], alpha)
    for t in range(steps):
        y_ref[:, t * d:(t + 1) * d] = rows(y, t)


def _sample_call(x, mod, state, win, wpa, wpb, wo, ws4, bs4, alg, alb, wconv, bconv, lng, lnb, alpha):
    n, steps, d = x.shape
    assert steps <= CHUNK and steps >= CONV_WIDTH - 1
    vmem = pl.BlockSpec(memory_space=pltpu.VMEM)
    smem = pl.BlockSpec(memory_space=pltpu.SMEM)
    hbm = pl.BlockSpec(memory_space=pl.ANY)
    y, conv, v = pl.pallas_call(
        functools.partial(_sample_kernel, alpha=alpha, steps=steps),
        in_specs=[smem, smem] + [vmem] * 3 + [hbm] * 4 + [vmem] * 6,
        scratch_shapes=_weight_scratch(win, wpa, wpb, wo),
        out_specs=[vmem, vmem, vmem],
        out_shape=[
            jax.ShapeDtypeStruct((n, steps * d), _F32),
            jax.ShapeDtypeStruct((n, (CONV_WIDTH - 1) * d), _F32),
            jax.ShapeDtypeStruct((n, steps * d), _F32),
        ],
        compiler_params=pltpu.CompilerParams(vmem_limit_bytes=VMEM_LIMIT_BYTES),
        name="sample_layer",
    )(ws4, bs4, x.reshape(n, steps * d), mod, state.reshape(n, (CONV_WIDTH - 1) * d),
      win, wpa, wpb, wo, alg, alb, wconv, bconv, lng, lnb)
    return (y.reshape(n, steps, d), conv.reshape(n, CONV_WIDTH - 1, d), v.reshape(n, steps, d))


def kernel(x_prompt, x_sample, state_conv, c_prompt, c_sample, w_c, b_c, w_in, a_ln_g, a_ln_b,
           w_s, b_s, w_conv, b_conv, w_pa, w_pb, w_o, ln_g, ln_b):
    depth = w_in.shape[0]
    d = x_prompt.shape[-1]
    nb = x_prompt.shape[0]
    steps = x_sample.shape[1]
    alpha = (2.0 * depth) ** 0.25
    c_all = jnp.concatenate([c_prompt, c_sample], axis=0)

    xp, xs = x_prompt, x_sample
    conv_p_rows, conv_s_rows, v_rows = [], [], []
    for l in range(depth):
        mod = _mod_call(c_all, w_c[l], b_c[l])
        win, wpa, wpb, wo = w_in[l], w_pa[l], w_pb[l], w_o[l]
        row = lambda a: a.reshape(1, d)
        small = (row(a_ln_g[l]), row(a_ln_b[l]), w_conv[l], row(b_conv[l]), row(ln_g[l]), row(ln_b[l]))
        xp, conv_p = _prompt_call(xp, mod[:nb], win, wpa, wpb, wo, w_s[l], b_s[l].T, *small, alpha)
        ws4 = w_s[l][:, :steps, :steps].reshape(GROUPS, steps * steps)
        bs4 = b_s[l][:, :steps]
        xs, conv_s, v_s = _sample_call(xs, mod[nb:], state_conv[l], win, wpa, wpb, wo, ws4, bs4,
                                       *small, alpha)
        conv_p_rows.append(conv_p)
        conv_s_rows.append(conv_s)
        v_rows.append(v_s)
    return (xp, xs, jnp.stack(conv_p_rows), jnp.stack(conv_s_rows), jnp.stack(v_rows))
```

```python
import functools

import jax
import jax.numpy as jnp
from jax import lax
from jax.experimental import pallas as pl
from jax.experimental.pallas import tpu as pltpu

CHUNK = 128
GROUPS = 8
CONV_WIDTH = 3
LN_EPS = 1e-5
ROW_TILE = 512
SUB_TILE = 256
STAGE_ROWS = 128
VMEM_LIMIT_BYTES = 62 * 1024 * 1024

_F32 = jnp.float32
_BF16 = jnp.bfloat16


def _dot(a, b):
    return jnp.dot(a, b, preferred_element_type=_F32)


def _layer_norm(x, gain, bias):
    mu = jnp.mean(x, axis=-1, keepdims=True)
    xc = x - mu
    var = jnp.mean(xc * xc, axis=-1, keepdims=True)
    return xc * lax.rsqrt(var + LN_EPS) * gain + bias


def _silu(x):
    return x * jax.nn.sigmoid(x)


def _stage_weights_bf16(srcs, dsts, stage, sem):
    _, rows, d = stage.shape
    cols = [(src, dst, k) for src, dst in zip(srcs, dsts) for k in range(src.shape[1] // d)]
    nr = srcs[0].shape[0] // rows
    assert nr % 2 == 0 and all(src.shape[0] == nr * rows for src in srcs)

    def copy(c, r):
        src, _, k = cols[c]
        row0 = r * rows if isinstance(r, int) else pl.multiple_of(r * rows, rows)
        return pltpu.make_async_copy(src.at[pl.ds(row0, rows), k * d:(k + 1) * d],
                                     stage.at[r % 2], sem.at[r % 2])

    copy(0, 0).start()
    for c, (_, dst, k) in enumerate(cols):

        def cast_block(r, carry, c=c, dst=dst, k=k):
            @pl.when(r + 1 < nr)
            def _():
                copy(c, r + 1).start()

            if c + 1 < len(cols):
                @pl.when(r + 1 == nr)
                def _():
                    copy(c + 1, 0).start()

            copy(c, r).wait()
            row0 = pl.multiple_of(r * rows, rows)
            dst[pl.ds(row0, rows), k * d:(k + 1) * d] = stage[r % 2].astype(_BF16)
            return carry

        lax.fori_loop(0, nr, cast_block, 0)


def _mod_kernel(c_ref, wc_ref, bc_ref, o_ref):
    o_ref[...] = _dot(c_ref[...].astype(_BF16), wc_ref[...].astype(_BF16)) + bc_ref[...]


def _mod_call(c_all, w_c, b_c):
    n, d = c_all.shape
    d3 = w_c.shape[1]
    return pl.pallas_call(
        _mod_kernel,
        grid=(d3 // d,),
        in_specs=[
            pl.BlockSpec((n, d), lambda i: (0, 0)),
            pl.BlockSpec((d, d), lambda i: (0, i)),
            pl.BlockSpec((1, d), lambda i: (0, i)),
        ],
        out_specs=pl.BlockSpec((n, d), lambda i: (0, i)),
        out_shape=jax.ShapeDtypeStruct((n, d3), _F32),
        compiler_params=pltpu.CompilerParams(dimension_semantics=("arbitrary",)),
        name="cond_proj",
    )(c_all, w_c, b_c.reshape(1, d3))


def _merge_and_norm(x, gate, out_a, out_b, g_a, g_b, wpa_ref, wpb_ref, wo_ref, lng, lnb, alpha):
    pa = _dot(out_a, wpa_ref[...])
    pb = _dot(out_b, wpb_ref[...])
    merged = (jax.nn.sigmoid(g_a) * pa + jax.nn.sigmoid(g_b) * pb).astype(_BF16)
    y = _dot(merged, wo_ref[...])
    return _layer_norm(alpha * x + (1.0 + gate) * y, lng, lnb)


def _prompt_kernel(x_ref, mod_ref, win_hbm, wpa_hbm, wpb_hbm, wo_hbm, ws_ref, bst_ref,
                   alg_ref, alb_ref, wconv_ref, bconv_ref, lng_ref, lnb_ref,
                   y_ref, conv_ref, win_ref, wpa_ref, wpb_ref, wo_ref, stage, sem,
                   wsm_scr, bias_scr, z_scr, *, alpha):
    tm, d = x_ref.shape[1], x_ref.shape[2]
    gd = d // GROUPS
    j = pl.program_id(1)

    @pl.when((pl.program_id(0) == 0) & (j == 0))
    def _():
        _stage_weights_bf16((win_hbm, wpa_hbm, wpb_hbm, wo_hbm), (win_ref, wpa_ref, wpb_ref, wo_ref),
                            stage, sem)
        row = lax.broadcasted_iota(jnp.int32, (CHUNK, CHUNK), 0)
        col = lax.broadcasted_iota(jnp.int32, (CHUNK, CHUNK), 1)
        for g in range(GROUPS):
            wsm_scr[g] = jnp.where(col <= row, ws_ref[g], 0.0).astype(_BF16)
            bias_scr[:, g * gd:(g + 1) * gd] = jnp.broadcast_to(bst_ref[:, g:g + 1], (CHUNK, gd))

    @pl.when(j == 0)
    def _():
        z_scr[0:8, :] = jnp.zeros((8, d), _F32)

    mod = mod_ref[0]
    shift, scale, gate = mod[:, 0:d], mod[:, d:2 * d], mod[:, 2 * d:3 * d]
    bias = bias_scr[...]
    wconv = wconv_ref[...]

    starts = range(0, tm, SUB_TILE)

    def project(r0):
        x = x_ref[0, r0:r0 + SUB_TILE, :]
        h = (x * (1.0 + scale) + shift).astype(_BF16)
        return [_dot(h, win_ref[:, k * d:(k + 1) * d]) for k in range(9)]

    def mix(r0, p):
        u, v, z_a, b_g, c_g, h_b, z_b, g_a, g_b = p
        vb = _layer_norm(v, alg_ref[...], alb_ref[...]).astype(_BF16)
        s_cols = []
        for g in range(GROUPS):
            cols = slice(g * gd, (g + 1) * gd)
            pair_rows = []
            for c in range(0, SUB_TILE // CHUNK, 2):
                rhs = jnp.concatenate([vb[c * CHUNK:(c + 1) * CHUNK, cols],
                                       vb[(c + 1) * CHUNK:(c + 2) * CHUNK, cols]], axis=1)
                res = _dot(wsm_scr[g], rhs)
                pair_rows.append(res[:, :gd] + bias[:, cols])
                pair_rows.append(res[:, gd:] + bias[:, cols])
            s_cols.append(jnp.concatenate(pair_rows, axis=0))
        s = jnp.concatenate(s_cols, axis=1)
        out_a = (u * s * _silu(z_a)).astype(_BF16)
        z = c_g * h_b
        z_scr[8 + r0:8 + r0 + SUB_TILE, :] = z
        conv = (bconv_ref[...] + wconv[0:1, :] * z_scr[6 + r0:6 + r0 + SUB_TILE, :]
                + wconv[1:2, :] * z_scr[7 + r0:7 + r0 + SUB_TILE, :] + wconv[2:3, :] * z)
        out_b = (b_g * conv * _silu(z_b)).astype(_BF16)
        return out_a, out_b, jax.nn.sigmoid(g_a), jax.nn.sigmoid(g_b)

    def merge(m):
        out_a, out_b, sg_a, sg_b = m
        return (sg_a * _dot(out_a, wpa_ref[...]) + sg_b * _dot(out_b, wpb_ref[...])).astype(_BF16)

    def finish(r0, merged):
        x = x_ref[0, r0:r0 + SUB_TILE, :]
        y = _dot(merged, wo_ref[...])
        y_ref[0, r0:r0 + SUB_TILE, :] = _layer_norm(alpha * x + (1.0 + gate) * y,
                                                    lng_ref[...], lnb_ref[...])

    projected = [project(r0) for r0 in starts]
    mixed = [mix(r0, p) for r0, p in zip(starts, projected)]
    merged = [merge(m) for m in mixed]
    for r0, mg in zip(starts, merged):
        finish(r0, mg)

    last = z_scr[8 + tm - (CONV_WIDTH - 1):8 + tm, :]
    z_scr[6:8, :] = last
    conv_ref[0] = last


def _weight_scratch(*weights):
    d = weights[0].shape[0]
    return ([pltpu.VMEM(w.shape, _BF16) for w in weights]
            + [pltpu.VMEM((2, STAGE_ROWS, d), _F32), pltpu.SemaphoreType.DMA((2,))])


def _resident(shape):
    nd = len(shape)
    return pl.BlockSpec(shape, lambda b, j: (0,) * nd, pipeline_mode=pl.Buffered(1))


def _prompt_call(x, mod, win, wpa, wpb, wo, w_s, b_s_t, alg, alb, wconv, bconv, lng, lnb, alpha):
    nb, length, d = x.shape
    tm = ROW_TILE
    assert length % tm == 0 and tm % SUB_TILE == 0 and SUB_TILE % (2 * CHUNK) == 0
    assert d % GROUPS == 0 and d // GROUPS == CHUNK
    small = [alg, alb, wconv, bconv, lng, lnb]
    hbm = pl.BlockSpec(memory_space=pl.ANY)
    return pl.pallas_call(
        functools.partial(_prompt_kernel, alpha=alpha),
        grid=(nb, length // tm),
        in_specs=[
            pl.BlockSpec((1, tm, d), lambda b, j: (b, j, 0)),
            pl.BlockSpec((1, 1, 3 * d), lambda b, j: (b, 0, 0)),
            hbm, hbm, hbm, hbm,
            _resident(w_s.shape), _resident(b_s_t.shape),
        ] + [_resident(a.shape) for a in small],
        out_specs=[
            pl.BlockSpec((1, tm, d), lambda b, j: (b, j, 0)),
            pl.BlockSpec((1, CONV_WIDTH - 1, d), lambda b, j: (b, 0, 0)),
        ],
        out_shape=[
            jax.ShapeDtypeStruct((nb, length, d), _F32),
            jax.ShapeDtypeStruct((nb, CONV_WIDTH - 1, d), _F32),
        ],
        scratch_shapes=_weight_scratch(win, wpa, wpb, wo) + [
            pltpu.VMEM((GROUPS, CHUNK, CHUNK), _BF16),
            pltpu.VMEM((CHUNK, d), _F32),
            pltpu.VMEM((8 + tm, d), _F32),
        ],
        compiler_params=pltpu.CompilerParams(
            dimension_semantics=("arbitrary", "arbitrary"),
            vmem_limit_bytes=VMEM_LIMIT_BYTES),
        name="prompt_layer",
    )(x, mod.reshape(nb, 1, 3 * d), win, wpa, wpb, wo, w_s, b_s_t, *small)


def _sample_kernel(ws4_ref, bs4_ref, x_ref, mod_ref, st_ref, win_hbm, wpa_hbm, wpb_hbm, wo_hbm,
                   alg_ref, alb_ref, wconv_ref, bconv_ref, lng_ref, lnb_ref,
                   y_ref, conv_ref, v_ref, win_ref, wpa_ref, wpb_ref, wo_ref, stage, sem,
                   *, alpha, steps):
    _stage_weights_bf16((win_hbm, wpa_hbm, wpb_hbm, wo_hbm), (win_ref, wpa_ref, wpb_ref, wo_ref),
                        stage, sem)
    n = x_ref.shape[0]
    d = x_ref.shape[1] // steps
    gd = d // GROUPS

    def rows(a, t):
        return a[t * n:(t + 1) * n, :]

    mod = mod_ref[...]
    shift, scale, gate = mod[:, 0:d], mod[:, d:2 * d], mod[:, 2 * d:3 * d]
    xs = [x_ref[:, t * d:(t + 1) * d] for t in range(steps)]
    x = jnp.concatenate(xs, axis=0)
    h = jnp.concatenate([(xt * (1.0 + scale) + shift).astype(_BF16) for xt in xs], axis=0)
    gate = jnp.concatenate([gate] * steps, axis=0)

    def proj(k):
        return _dot(h, win_ref[:, k * d:(k + 1) * d])

    vn = _layer_norm(proj(1), alg_ref[...], alb_ref[...])
    s_rows = []
    for t in range(steps):
        v_ref[:, t * d:(t + 1) * d] = rows(vn, t)
        s_cols = []
        for g in range(GROUPS):
            cols = slice(g * gd, (g + 1) * gd)
            acc = jnp.full((n, gd), bs4_ref[g, t], _F32)
            for jj in range(t + 1):
                acc = acc + ws4_ref[g, t * steps + jj] * rows(vn, jj)[:, cols]
            s_cols.append(acc)
        s_rows.append(jnp.concatenate(s_cols, axis=1))
    s = jnp.concatenate(s_rows, axis=0)
    out_a = (proj(0) * s * _silu(proj(2))).astype(_BF16)

    z = proj(4) * proj(5)
    hist = [st_ref[:, k * d:(k + 1) * d] for k in range(CONV_WIDTH - 1)] + [rows(z, t) for t in range(steps)]
    wconv = wconv_ref[...]
    conv = jnp.concatenate(
        [bconv_ref[...] + sum(wconv[k:k + 1, :] * hist[t + k] for k in range(CONV_WIDTH))
         for t in range(steps)], axis=0)
    for k in range(CONV_WIDTH - 1):
        conv_ref[:, k * d:(k + 1) * d] = hist[steps + k]
    out_b = (proj(3) * conv * _silu(proj(6))).astype(_BF16)

    y = _merge_and_norm(x, gate, out_a, out_b, proj(7), proj(8), wpa_ref, wpb_ref, wo_ref,
                        lng_ref[...], lnb_ref[...], alpha)
    for t in range(steps):
        y_ref[:, t * d:(t + 1) * d] = rows(y, t)


def _sample_call(x, mod, state, win, wpa, wpb, wo, ws4, bs4, alg, alb, wconv, bconv, lng, lnb, alpha):
    n, steps, d = x.shape
    assert steps <= CHUNK and steps >= CONV_WIDTH - 1
    vmem = pl.BlockSpec(memory_space=pltpu.VMEM)
    smem = pl.BlockSpec(memory_space=pltpu.SMEM)
    hbm = pl.BlockSpec(memory_space=pl.ANY)
    y, conv, v = pl.pallas_call(
        functools.partial(_sample_kernel, alpha=alpha, steps=steps),
        in_specs=[smem, smem] + [vmem] * 3 + [hbm] * 4 + [vmem] * 6,
        scratch_shapes=_weight_scratch(win, wpa, wpb, wo),
        out_specs=[vmem, vmem, vmem],
        out_shape=[
            jax.ShapeDtypeStruct((n, steps * d), _F32),
            jax.ShapeDtypeStruct((n, (CONV_WIDTH - 1) * d), _F32),
            jax.ShapeDtypeStruct((n, steps * d), _F32),
        ],
        compiler_params=pltpu.CompilerParams(vmem_limit_bytes=VMEM_LIMIT_BYTES),
        name="sample_layer",
    )(ws4, bs4, x.reshape(n, steps * d), mod, state.reshape(n, (CONV_WIDTH - 1) * d),
      win, wpa, wpb, wo, alg, alb, wconv, bconv, lng, lnb)
    return (y.reshape(n, steps, d), conv.reshape(n, CONV_WIDTH - 1, d), v.reshape(n, steps, d))


def kernel(x_prompt, x_sample, state_conv, c_prompt, c_sample, w_c, b_c, w_in, a_ln_g, a_ln_b,
           w_s, b_s, w_conv, b_conv, w_pa, w_pb, w_o, ln_g, ln_b):
    depth = w_in.shape[0]
    d = x_prompt.shape[-1]
    nb = x_prompt.shape[0]
    steps = x_sample.shape[1]
    alpha = (2.0 * depth) ** 0.25
    c_all = jnp.concatenate([c_prompt, c_sample], axis=0)

    xp, xs = x_prompt, x_sample
    conv_p_rows, conv_s_rows, v_rows = [], [], []
    for l in range(depth):
        mod = _mod_call(c_all, w_c[l], b_c[l])
        win, wpa, wpb, wo = w_in[l], w_pa[l], w_pb[l], w_o[l]
        row = lambda a: a.reshape(1, d)
        small = (row(a_ln_g[l]), row(a_ln_b[l]), w_conv[l], row(b_conv[l]), row(ln_g[l]), row(ln_b[l]))
        xp, conv_p = _prompt_call(xp, mod[:nb], win, wpa, wpb, wo, w_s[l], b_s[l].T, *small, alpha)
        ws4 = w_s[l][:, :steps, :steps].reshape(GROUPS, steps * steps)
        bs4 = b_s[l][:, :steps]
        xs, conv_s, v_s = _sample_call(xs, mod[nb:], state_conv[l], win, wpa, wpb, wo, ws4, bs4,
                                       *small, alpha)
        conv_p_rows.append(conv_p)
        conv_s_rows.append(conv_s)
        v_rows.append(v_s)
    return (xp, xs, jnp.stack(conv_p_rows), jnp.stack(conv_s_rows), jnp.stack(v_rows))
```

```python
import functools

import jax
import jax.numpy as jnp
from jax import lax
from jax.experimental import pallas as pl
from jax.experimental.pallas import tpu as pltpu

CHUNK = 128
GROUPS = 8
CONV_WIDTH = 3
LN_EPS = 1e-5
ROW_TILE = 512
SUB_TILE = 256
VMEM_LIMIT_BYTES = 62 * 1024 * 1024

_F32 = jnp.float32
_BF16 = jnp.bfloat16


def _dot(a, b):
    return jnp.dot(a, b, preferred_element_type=_F32)


def _layer_norm(x, gain, bias):
    mu = jnp.mean(x, axis=-1, keepdims=True)
    xc = x - mu
    var = jnp.mean(xc * xc, axis=-1, keepdims=True)
    return xc * lax.rsqrt(var + LN_EPS) * gain + bias


def _silu(x):
    return x * jax.nn.sigmoid(x)


def _stage_weights_bf16(srcs, dsts, stage, sem):
    d = stage.shape[1]
    blocks = [(src, dst, k) for src, dst in zip(srcs, dsts) for k in range(src.shape[1] // d)]

    def copy(i):
        src, _, k = blocks[i]
        return pltpu.make_async_copy(src.at[:, k * d:(k + 1) * d], stage.at[i % 2], sem.at[i % 2])

    copy(0).start()
    for i, (_, dst, k) in enumerate(blocks):
        if i + 1 < len(blocks):
            copy(i + 1).start()
        copy(i).wait()

        def cast_rows(r, carry, dst=dst, k=k, slot=i % 2):
            rows = pl.ds(pl.multiple_of(r * CHUNK, CHUNK), CHUNK)
            dst[rows, k * d:(k + 1) * d] = stage[slot, rows, :].astype(_BF16)
            return carry

        lax.fori_loop(0, d // CHUNK, cast_rows, 0)


def _mod_kernel(c_ref, wc_ref, bc_ref, o_ref):
    o_ref[...] = _dot(c_ref[...].astype(_BF16), wc_ref[...].astype(_BF16)) + bc_ref[...]


def _mod_call(c_all, w_c, b_c):
    n, d = c_all.shape
    d3 = w_c.shape[1]
    return pl.pallas_call(
        _mod_kernel,
        grid=(d3 // d,),
        in_specs=[
            pl.BlockSpec((n, d), lambda i: (0, 0)),
            pl.BlockSpec((d, d), lambda i: (0, i)),
            pl.BlockSpec((1, d), lambda i: (0, i)),
        ],
        out_specs=pl.BlockSpec((n, d), lambda i: (0, i)),
        out_shape=jax.ShapeDtypeStruct((n, d3), _F32),
        compiler_params=pltpu.CompilerParams(dimension_semantics=("arbitrary",)),
        name="cond_proj",
    )(c_all, w_c, b_c.reshape(1, d3))


def _merge_and_norm(x, gate, out_a, out_b, g_a, g_b, wpa_ref, wpb_ref, wo_ref, lng, lnb, alpha):
    pa = _dot(out_a, wpa_ref[...])
    pb = _dot(out_b, wpb_ref[...])
    merged = (jax.nn.sigmoid(g_a) * pa + jax.nn.sigmoid(g_b) * pb).astype(_BF16)
    y = _dot(merged, wo_ref[...])
    return _layer_norm(alpha * x + (1.0 + gate) * y, lng, lnb)


def _prompt_kernel(x_ref, mod_ref, win_ref, wpa_ref, wpb_ref, wo_ref, ws_ref, bst_ref,
                   alg_ref, alb_ref, wconv_ref, bconv_ref, lng_ref, lnb_ref,
                   y_ref, conv_ref, wsm_scr, bias_scr, z_scr, *, alpha):
    tm, d = x_ref.shape[1], x_ref.shape[2]
    gd = d // GROUPS
    j = pl.program_id(1)

    @pl.when((pl.program_id(0) == 0) & (j == 0))
    def _():
        row = lax.broadcasted_iota(jnp.int32, (CHUNK, CHUNK), 0)
        col = lax.broadcasted_iota(jnp.int32, (CHUNK, CHUNK), 1)
        for g in range(GROUPS):
            wsm_scr[g] = jnp.where(col <= row, ws_ref[g], 0.0).astype(_BF16)
            bias_scr[:, g * gd:(g + 1) * gd] = jnp.broadcast_to(bst_ref[:, g:g + 1], (CHUNK, gd))

    @pl.when(j == 0)
    def _():
        z_scr[0:8, :] = jnp.zeros((8, d), _F32)

    mod = mod_ref[0]
    shift, scale, gate = mod[:, 0:d], mod[:, d:2 * d], mod[:, 2 * d:3 * d]
    bias = bias_scr[...]
    wconv = wconv_ref[...]

    starts = range(0, tm, SUB_TILE)

    def project(r0):
        x = x_ref[0, r0:r0 + SUB_TILE, :]
        h = (x * (1.0 + scale) + shift).astype(_BF16)
        return [_dot(h, win_ref[:, k * d:(k + 1) * d]) for k in range(9)]

    def mix(r0, p):
        u, v, z_a, b_g, c_g, h_b, z_b, g_a, g_b = p
        vb = _layer_norm(v, alg_ref[...], alb_ref[...]).astype(_BF16)
        s_cols = []
        for g in range(GROUPS):
            cols = slice(g * gd, (g + 1) * gd)
            pair_rows = []
            for c in range(0, SUB_TILE // CHUNK, 2):
                rhs = jnp.concatenate([vb[c * CHUNK:(c + 1) * CHUNK, cols],
                                       vb[(c + 1) * CHUNK:(c + 2) * CHUNK, cols]], axis=1)
                res = _dot(wsm_scr[g], rhs)
                pair_rows.append(res[:, :gd] + bias[:, cols])
                pair_rows.append(res[:, gd:] + bias[:, cols])
            s_cols.append(jnp.concatenate(pair_rows, axis=0))
        s = jnp.concatenate(s_cols, axis=1)
        out_a = (u * s * _silu(z_a)).astype(_BF16)
        z = c_g * h_b
        z_scr[8 + r0:8 + r0 + SUB_TILE, :] = z
        conv = (bconv_ref[...] + wconv[0:1, :] * z_scr[6 + r0:6 + r0 + SUB_TILE, :]
                + wconv[1:2, :] * z_scr[7 + r0:7 + r0 + SUB_TILE, :] + wconv[2:3, :] * z)
        out_b = (b_g * conv * _silu(z_b)).astype(_BF16)
        return out_a, out_b, jax.nn.sigmoid(g_a), jax.nn.sigmoid(g_b)

    def merge(m):
        out_a, out_b, sg_a, sg_b = m
        return (sg_a * _dot(out_a, wpa_ref[...]) + sg_b * _dot(out_b, wpb_ref[...])).astype(_BF16)

    def finish(r0, merged):
        x = x_ref[0, r0:r0 + SUB_TILE, :]
        y = _dot(merged, wo_ref[...])
        y_ref[0, r0:r0 + SUB_TILE, :] = _layer_norm(alpha * x + (1.0 + gate) * y,
                                                    lng_ref[...], lnb_ref[...])

    projected = [project(r0) for r0 in starts]
    mixed = [mix(r0, p) for r0, p in zip(starts, projected)]
    merged = [merge(m) for m in mixed]
    for r0, mg in zip(starts, merged):
        finish(r0, mg)

    last = z_scr[8 + tm - (CONV_WIDTH - 1):8 + tm, :]
    z_scr[6:8, :] = last
    conv_ref[0] = last


def _resident(shape):
    nd = len(shape)
    return pl.BlockSpec(shape, lambda b, j: (0,) * nd, pipeline_mode=pl.Buffered(1))


def _prompt_call(x, mod, win, wpa, wpb, wo, w_s, b_s_t, alg, alb, wconv, bconv, lng, lnb, alpha):
    nb, length, d = x.shape
    tm = ROW_TILE
    assert length % tm == 0 and tm % SUB_TILE == 0 and SUB_TILE % (2 * CHUNK) == 0
    assert d % GROUPS == 0 and d // GROUPS == CHUNK
    small = [alg, alb, wconv, bconv, lng, lnb]
    return pl.pallas_call(
        functools.partial(_prompt_kernel, alpha=alpha),
        grid=(nb, length // tm),
        in_specs=[
            pl.BlockSpec((1, tm, d), lambda b, j: (b, j, 0)),
            pl.BlockSpec((1, 1, 3 * d), lambda b, j: (b, 0, 0)),
            _resident(win.shape), _resident(wpa.shape), _resident(wpb.shape), _resident(wo.shape),
            _resident(w_s.shape), _resident(b_s_t.shape),
        ] + [_resident(a.shape) for a in small],
        out_specs=[
            pl.BlockSpec((1, tm, d), lambda b, j: (b, j, 0)),
            pl.BlockSpec((1, CONV_WIDTH - 1, d), lambda b, j: (b, 0, 0)),
        ],
        out_shape=[
            jax.ShapeDtypeStruct((nb, length, d), _F32),
            jax.ShapeDtypeStruct((nb, CONV_WIDTH - 1, d), _F32),
        ],
        scratch_shapes=[
            pltpu.VMEM((GROUPS, CHUNK, CHUNK), _BF16),
            pltpu.VMEM((CHUNK, d), _F32),
            pltpu.VMEM((8 + tm, d), _F32),
        ],
        compiler_params=pltpu.CompilerParams(
            dimension_semantics=("arbitrary", "arbitrary"),
            vmem_limit_bytes=VMEM_LIMIT_BYTES),
        name="prompt_layer",
    )(x, mod.reshape(nb, 1, 3 * d), win, wpa, wpb, wo, w_s, b_s_t, *small)


def _sample_kernel(ws4_ref, bs4_ref, x_ref, mod_ref, st_ref, win_hbm, wpa_hbm, wpb_hbm, wo_hbm,
                   alg_ref, alb_ref, wconv_ref, bconv_ref, lng_ref, lnb_ref,
                   y_ref, conv_ref, v_ref, win_out, wpa_out, wpb_out, wo_out,
                   win_ref, wpa_ref, wpb_ref, wo_ref, stage, sem, out_sem, *, alpha, steps):
    _stage_weights_bf16((win_hbm, wpa_hbm, wpb_hbm, wo_hbm), (win_ref, wpa_ref, wpb_ref, wo_ref),
                        stage, sem)
    exports = [pltpu.make_async_copy(src, dst, out_sem.at[i]) for i, (src, dst) in enumerate(
        zip((win_ref, wpa_ref, wpb_ref, wo_ref), (win_out, wpa_out, wpb_out, wo_out)))]
    for cp in exports:
        cp.start()
    n = x_ref.shape[0]
    d = x_ref.shape[1] // steps
    gd = d // GROUPS

    def rows(a, t):
        return a[t * n:(t + 1) * n, :]

    mod = mod_ref[...]
    shift, scale, gate = mod[:, 0:d], mod[:, d:2 * d], mod[:, 2 * d:3 * d]
    xs = [x_ref[:, t * d:(t + 1) * d] for t in range(steps)]
    x = jnp.concatenate(xs, axis=0)
    h = jnp.concatenate([(xt * (1.0 + scale) + shift).astype(_BF16) for xt in xs], axis=0)
    gate = jnp.concatenate([gate] * steps, axis=0)

    def proj(k):
        return _dot(h, win_ref[:, k * d:(k + 1) * d])

    vn = _layer_norm(proj(1), alg_ref[...], alb_ref[...])
    s_rows = []
    for t in range(steps):
        v_ref[:, t * d:(t + 1) * d] = rows(vn, t)
        s_cols = []
        for g in range(GROUPS):
            cols = slice(g * gd, (g + 1) * gd)
            acc = jnp.full((n, gd), bs4_ref[g, t], _F32)
            for jj in range(t + 1):
                acc = acc + ws4_ref[g, t * steps + jj] * rows(vn, jj)[:, cols]
            s_cols.append(acc)
        s_rows.append(jnp.concatenate(s_cols, axis=1))
    s = jnp.concatenate(s_rows, axis=0)
    out_a = (proj(0) * s * _silu(proj(2))).astype(_BF16)

    z = proj(4) * proj(5)
    hist = [st_ref[:, k * d:(k + 1) * d] for k in range(CONV_WIDTH - 1)] + [rows(z, t) for t in range(steps)]
    wconv = wconv_ref[...]
    conv = jnp.concatenate(
        [bconv_ref[...] + sum(wconv[k:k + 1, :] * hist[t + k] for k in range(CONV_WIDTH))
         for t in range(steps)], axis=0)
    for k in range(CONV_WIDTH - 1):
        conv_ref[:, k * d:(k + 1) * d] = hist[steps + k]
    out_b = (proj(3) * conv * _silu(proj(6))).astype(_BF16)

    y = _merge_and_norm(x, gate, out_a, out_b, proj(7), proj(8), wpa_ref, wpb_ref, wo_ref,
                        lng_ref[...], lnb_ref[...], alpha)
    for t in range(steps):
        y_ref[:, t * d:(t + 1) * d] = rows(y, t)
    for cp in exports:
        cp.wait()


def _sample_call(x, mod, state, win, wpa, wpb, wo, ws4, bs4, alg, alb, wconv, bconv, lng, lnb, alpha):
    n, steps, d = x.shape
    assert steps <= CHUNK and steps >= CONV_WIDTH - 1
    vmem = pl.BlockSpec(memory_space=pltpu.VMEM)
    smem = pl.BlockSpec(memory_space=pltpu.SMEM)
    hbm = pl.BlockSpec(memory_space=pl.ANY)
    weights = (win, wpa, wpb, wo)
    y, conv, v, *weights_bf16 = pl.pallas_call(
        functools.partial(_sample_kernel, alpha=alpha, steps=steps),
        in_specs=[smem, smem] + [vmem] * 3 + [hbm] * 4 + [vmem] * 6,
        scratch_shapes=[pltpu.VMEM(w.shape, _BF16) for w in weights] + [
            pltpu.VMEM((2, d, d), _F32), pltpu.SemaphoreType.DMA((2,)),
            pltpu.SemaphoreType.DMA((len(weights),))],
        out_specs=[vmem, vmem, vmem] + [hbm] * len(weights),
        out_shape=[
            jax.ShapeDtypeStruct((n, steps * d), _F32),
            jax.ShapeDtypeStruct((n, (CONV_WIDTH - 1) * d), _F32),
            jax.ShapeDtypeStruct((n, steps * d), _F32),
        ] + [jax.ShapeDtypeStruct(w.shape, _BF16) for w in weights],
        compiler_params=pltpu.CompilerParams(vmem_limit_bytes=VMEM_LIMIT_BYTES),
        name="sample_layer",
    )(ws4, bs4, x.reshape(n, steps * d), mod, state.reshape(n, (CONV_WIDTH - 1) * d),
      win, wpa, wpb, wo, alg, alb, wconv, bconv, lng, lnb)
    return (y.reshape(n, steps, d), conv.reshape(n, CONV_WIDTH - 1, d), v.reshape(n, steps, d),
            weights_bf16)


def kernel(x_prompt, x_sample, state_conv, c_prompt, c_sample, w_c, b_c, w_in, a_ln_g, a_ln_b,
           w_s, b_s, w_conv, b_conv, w_pa, w_pb, w_o, ln_g, ln_b):
    depth = w_in.shape[0]
    d = x_prompt.shape[-1]
    nb = x_prompt.shape[0]
    steps = x_sample.shape[1]
    alpha = (2.0 * depth) ** 0.25
    c_all = jnp.concatenate([c_prompt, c_sample], axis=0)

    xp, xs = x_prompt, x_sample
    conv_p_rows, conv_s_rows, v_rows = [], [], []
    for l in range(depth):
        mod = _mod_call(c_all, w_c[l], b_c[l])
        row = lambda a: a.reshape(1, d)
        small = (row(a_ln_g[l]), row(a_ln_b[l]), w_conv[l], row(b_conv[l]), row(ln_g[l]), row(ln_b[l]))
        ws4 = w_s[l][:, :steps, :steps].reshape(GROUPS, steps * steps)
        bs4 = b_s[l][:, :steps]
        xs, conv_s, v_s, weights_bf16 = _sample_call(
            xs, mod[nb:], state_conv[l], w_in[l], w_pa[l], w_pb[l], w_o[l], ws4, bs4, *small, alpha)
        xp, conv_p = _prompt_call(xp, mod[:nb], *weights_bf16, w_s[l], b_s[l].T, *small, alpha)
        conv_p_rows.append(conv_p)
        conv_s_rows.append(conv_s)
        v_rows.append(v_s)
    return (xp, xs, jnp.stack(conv_p_rows), jnp.stack(conv_s_rows), jnp.stack(v_rows))
```

```python
import functools

import jax
import jax.numpy as jnp
from jax import lax
from jax.experimental import pallas as pl
from jax.experimental.pallas import tpu as pltpu

CHUNK = 128
GROUPS = 8
CONV_WIDTH = 3
LN_EPS = 1e-5
ROW_TILE = 512
SUB_TILE = 256
VMEM_LIMIT_BYTES = 62 * 1024 * 1024

_F32 = jnp.float32
_BF16 = jnp.bfloat16


def _dot(a, b):
    return jnp.dot(a, b, preferred_element_type=_F32)


def _layer_norm(x, gain, bias):
    mu = jnp.mean(x, axis=-1, keepdims=True)
    xc = x - mu
    var = jnp.mean(xc * xc, axis=-1, keepdims=True)
    return xc * lax.rsqrt(var + LN_EPS) * gain + bias


def _silu(x):
    return x * jax.nn.sigmoid(x)


def _prompt_kernel(x_ref, mod_ref, win_ref, wpa_ref, wpb_ref, wo_ref, ws_ref, bst_ref,
                   alg_ref, alb_ref, wconv_ref, bconv_ref, lng_ref, lnb_ref,
                   y_ref, conv_ref, wsm_scr, bias_scr, z_scr, *, alpha):
    tm, d = x_ref.shape[1], x_ref.shape[2]
    gd = d // GROUPS
    j = pl.program_id(1)

    @pl.when((pl.program_id(0) == 0) & (j == 0))
    def _():
        row = lax.broadcasted_iota(jnp.int32, (CHUNK, CHUNK), 0)
        col = lax.broadcasted_iota(jnp.int32, (CHUNK, CHUNK), 1)
        for g in range(GROUPS):
            wsm_scr[g] = jnp.where(col <= row, ws_ref[g], 0.0).astype(_BF16)
            bias_scr[:, g * gd:(g + 1) * gd] = jnp.broadcast_to(bst_ref[:, g:g + 1], (CHUNK, gd))

    @pl.when(j == 0)
    def _():
        z_scr[0:8, :] = jnp.zeros((8, d), _F32)

    mod = mod_ref[pl.ds(pl.program_id(0), 1), :]
    shift, scale, gate = mod[:, 0:d], mod[:, d:2 * d], mod[:, 2 * d:3 * d]
    bias = bias_scr[...]
    wconv = wconv_ref[...]

    starts = range(0, tm, SUB_TILE)

    def project(r0):
        x = x_ref[0, r0:r0 + SUB_TILE, :]
        h = (x * (1.0 + scale) + shift).astype(_BF16)
        return [_dot(h, win_ref[:, k * d:(k + 1) * d]) for k in range(9)]

    def mix(r0, p):
        u, v, z_a, b_g, c_g, h_b, z_b, g_a, g_b = p
        vb = _layer_norm(v, alg_ref[...], alb_ref[...]).astype(_BF16)
        s_cols = []
        for g in range(GROUPS):
            cols = slice(g * gd, (g + 1) * gd)
            pair_rows = []
            for c in range(0, SUB_TILE // CHUNK, 2):
                rhs = jnp.concatenate([vb[c * CHUNK:(c + 1) * CHUNK, cols],
                                       vb[(c + 1) * CHUNK:(c + 2) * CHUNK, cols]], axis=1)
                res = _dot(wsm_scr[g], rhs)
                pair_rows.append(res[:, :gd] + bias[:, cols])
                pair_rows.append(res[:, gd:] + bias[:, cols])
            s_cols.append(jnp.concatenate(pair_rows, axis=0))
        s = jnp.concatenate(s_cols, axis=1)
        out_a = (u * s * _silu(z_a)).astype(_BF16)
        z = c_g * h_b
        z_scr[8 + r0:8 + r0 + SUB_TILE, :] = z
        conv = (bconv_ref[...] + wconv[0:1, :] * z_scr[6 + r0:6 + r0 + SUB_TILE, :]
                + wconv[1:2, :] * z_scr[7 + r0:7 + r0 + SUB_TILE, :] + wconv[2:3, :] * z)
        out_b = (b_g * conv * _silu(z_b)).astype(_BF16)
        return out_a, out_b, jax.nn.sigmoid(g_a), jax.nn.sigmoid(g_b)

    def merge(m):
        out_a, out_b, sg_a, sg_b = m
        return (sg_a * _dot(out_a, wpa_ref[...]) + sg_b * _dot(out_b, wpb_ref[...])).astype(_BF16)

    def finish(r0, merged):
        x = x_ref[0, r0:r0 + SUB_TILE, :]
        y = _dot(merged, wo_ref[...])
        y_ref[0, r0:r0 + SUB_TILE, :] = _layer_norm(alpha * x + (1.0 + gate) * y,
                                                    lng_ref[...], lnb_ref[...])

    projected = [project(r0) for r0 in starts]
    mixed = [mix(r0, p) for r0, p in zip(starts, projected)]
    merged = [merge(m) for m in mixed]
    for r0, mg in zip(starts, merged):
        finish(r0, mg)

    last = z_scr[8 + tm - (CONV_WIDTH - 1):8 + tm, :]
    z_scr[6:8, :] = last
    conv_ref[0] = last


def _resident(shape):
    nd = len(shape)
    return pl.BlockSpec(shape, lambda b, j: (0,) * nd, pipeline_mode=pl.Buffered(1))


def _prompt_call(x, mod, win, wpa, wpb, wo, w_s, b_s_t, alg, alb, wconv, bconv, lng, lnb, alpha):
    nb, length, d = x.shape
    tm = ROW_TILE
    assert length % tm == 0 and tm % SUB_TILE == 0 and SUB_TILE % (2 * CHUNK) == 0
    assert d % GROUPS == 0 and d // GROUPS == CHUNK
    small = [alg, alb, wconv, bconv, lng, lnb]
    return pl.pallas_call(
        functools.partial(_prompt_kernel, alpha=alpha),
        grid=(nb, length // tm),
        in_specs=[
            pl.BlockSpec((1, tm, d), lambda b, j: (b, j, 0)),
            _resident(mod.shape),
            _resident(win.shape), _resident(wpa.shape), _resident(wpb.shape), _resident(wo.shape),
            _resident(w_s.shape), _resident(b_s_t.shape),
        ] + [_resident(a.shape) for a in small],
        out_specs=[
            pl.BlockSpec((1, tm, d), lambda b, j: (b, j, 0)),
            pl.BlockSpec((1, CONV_WIDTH - 1, d), lambda b, j: (b, 0, 0)),
        ],
        out_shape=[
            jax.ShapeDtypeStruct((nb, length, d), _F32),
            jax.ShapeDtypeStruct((nb, CONV_WIDTH - 1, d), _F32),
        ],
        scratch_shapes=[
            pltpu.VMEM((GROUPS, CHUNK, CHUNK), _BF16),
            pltpu.VMEM((CHUNK, d), _F32),
            pltpu.VMEM((8 + tm, d), _F32),
        ],
        compiler_params=pltpu.CompilerParams(
            dimension_semantics=("arbitrary", "arbitrary"),
            vmem_limit_bytes=VMEM_LIMIT_BYTES),
        name="prompt_layer",
    )(x, mod, win, wpa, wpb, wo, w_s, b_s_t, *small)


_PROJ_ORDER = (1, 0, 2, 4, 5, 3, 6, 7, 8)


def _sample_kernel(ws4_ref, bs4_ref, cs_ref, cp_ref, bc_ref,
                   alg_ref, alb_ref, wconv_ref, bconv_ref, lng_ref, lnb_ref,
                   x_hbm, st_hbm, wc_hbm, win_hbm, wpa_hbm, wpb_hbm, wo_hbm,
                   y_hbm, conv_hbm, v_hbm, win_out, wpa_out, wpb_out, wo_out, modp_ref,
                   win_ref, wpa_ref, wpb_ref, wo_ref, stage, xbuf, stbuf, ybuf, cvbuf, vbuf,
                   sem, exp_sem, in_sem, res_sem, *, alpha):
    steps, n, d = xbuf.shape
    gd = d // GROUPS
    hist_rows = CONV_WIDTH - 1

    in_copies = ([pltpu.make_async_copy(x_hbm.at[:, t, :], xbuf.at[t], in_sem.at[t])
                  for t in range(steps)]
                 + [pltpu.make_async_copy(st_hbm.at[:, k, :], stbuf.at[k], in_sem.at[steps + k])
                    for k in range(hist_rows)])
    for cp in in_copies:
        cp.start()

    blocks = ([(wc_hbm, None, None, j) for j in range(wc_hbm.shape[1] // d)]
              + [(win_hbm, win_ref, win_out, k) for k in _PROJ_ORDER]
              + [(wpa_hbm, wpa_ref, wpa_out, 0), (wpb_hbm, wpb_ref, wpb_out, 0),
                 (wo_hbm, wo_ref, wo_out, 0)])
    exports = []
    position = [0]

    def stage_copy(i):
        src, _, _, k = blocks[i]
        return pltpu.make_async_copy(src.at[:, k * d:(k + 1) * d], stage.at[i % 2], sem.at[i % 2])

    def next_weight():
        i = position[0]
        position[0] += 1
        if i + 1 < len(blocks):
            stage_copy(i + 1).start()
        stage_copy(i).wait()
        w = stage[i % 2].astype(_BF16)
        _, keep, out, k = blocks[i]
        if keep is not None:
            cols = slice(k * d, (k + 1) * d)
            keep[:, cols] = w
            cp = pltpu.make_async_copy(keep.at[:, cols], out.at[:, cols], exp_sem.at[len(exports)])
            cp.start()
            exports.append(cp)
        return w

    stage_copy(0).start()

    c_all = jnp.concatenate([cs_ref[...], cp_ref[...]], axis=0).astype(_BF16)
    bc = bc_ref[...]
    mods = []
    for j in range(3):
        m = _dot(c_all, next_weight()) + bc[:, j * d:(j + 1) * d]
        modp_ref[:, j * d:(j + 1) * d] = m[n:, :]
        mods.append(m[0:n, :])
    shift, scale, gate = mods

    for cp in in_copies:
        cp.wait()
    h = jnp.concatenate([(xbuf[t] * (1.0 + scale) + shift).astype(_BF16) for t in range(steps)],
                        axis=0)

    def rows(a, t):
        return a[t * n:(t + 1) * n, :]

    def proj():
        return _dot(h, next_weight())

    results = []

    def send(buf, t, dst):
        cp = pltpu.make_async_copy(buf.at[t], dst.at[:, t, :], res_sem.at[len(results)])
        cp.start()
        results.append(cp)

    vn = _layer_norm(proj(), alg_ref[...], alb_ref[...])
    s_rows = []
    for t in range(steps):
        vbuf[t] = rows(vn, t)
        send(vbuf, t, v_hbm)
        s_cols = []
        for g in range(GROUPS):
            cols = slice(g * gd, (g + 1) * gd)
            acc = jnp.full((n, gd), bs4_ref[g, t], _F32)
            for jj in range(t + 1):
                acc = acc + ws4_ref[g, t * steps + jj] * rows(vn, jj)[:, cols]
            s_cols.append(acc)
        s_rows.append(jnp.concatenate(s_cols, axis=1))
    s = jnp.concatenate(s_rows, axis=0)
    u = proj()
    out_a = (u * s * _silu(proj())).astype(_BF16)

    z = proj()
    z = z * proj()
    hist = [stbuf[k] for k in range(hist_rows)] + [rows(z, t) for t in range(steps)]
    for k in range(hist_rows):
        cvbuf[k] = hist[steps + k]
        send(cvbuf, k, conv_hbm)
    wconv = wconv_ref[...]
    conv = jnp.concatenate(
        [bconv_ref[...] + sum(wconv[k:k + 1, :] * hist[t + k] for k in range(CONV_WIDTH))
         for t in range(steps)], axis=0)
    b_g = proj()
    out_b = (b_g * conv * _silu(proj())).astype(_BF16)

    sg_a = jax.nn.sigmoid(proj())
    sg_b = jax.nn.sigmoid(proj())
    pa = _dot(out_a, next_weight())
    pb = _dot(out_b, next_weight())
    merged = (sg_a * pa + sg_b * pb).astype(_BF16)
    y = _dot(merged, next_weight())
    for t in range(steps):
        ybuf[t] = _layer_norm(alpha * xbuf[t] + (1.0 + gate) * rows(y, t), lng_ref[...], lnb_ref[...])
        send(ybuf, t, y_hbm)

    for cp in exports + results:
        cp.wait()


def _sample_call(x, state, c_s, c_p, w_c, b_c, win, wpa, wpb, wo, ws4, bs4,
                 alg, alb, wconv, bconv, lng, lnb, alpha):
    n, steps, d = x.shape
    nb = c_p.shape[0]
    hist_rows = CONV_WIDTH - 1
    assert hist_rows <= steps <= CHUNK and w_c.shape == (d, 3 * d)
    vmem = pl.BlockSpec(memory_space=pltpu.VMEM)
    smem = pl.BlockSpec(memory_space=pltpu.SMEM)
    hbm = pl.BlockSpec(memory_space=pl.ANY)
    weights = (win, wpa, wpb, wo)
    n_exports = sum(w.shape[1] // d for w in weights)
    n_results = 2 * steps + hist_rows
    y, conv, v, *weights_bf16, mod_p = pl.pallas_call(
        functools.partial(_sample_kernel, alpha=alpha),
        in_specs=[smem, smem] + [vmem] * 9 + [hbm] * 7,
        out_specs=[hbm] * (3 + len(weights)) + [vmem],
        out_shape=[
            jax.ShapeDtypeStruct((n, steps, d), _F32),
            jax.ShapeDtypeStruct((n, hist_rows, d), _F32),
            jax.ShapeDtypeStruct((n, steps, d), _F32),
        ] + [jax.ShapeDtypeStruct(w.shape, _BF16) for w in weights] + [
            jax.ShapeDtypeStruct((nb, 3 * d), _F32)],
        scratch_shapes=[pltpu.VMEM(w.shape, _BF16) for w in weights] + [
            pltpu.VMEM((2, d, d), _F32),
            pltpu.VMEM((steps, n, d), _F32), pltpu.VMEM((hist_rows, n, d), _F32),
            pltpu.VMEM((steps, n, d), _F32), pltpu.VMEM((hist_rows, n, d), _F32),
            pltpu.VMEM((steps, n, d), _F32),
            pltpu.SemaphoreType.DMA((2,)), pltpu.SemaphoreType.DMA((n_exports,)),
            pltpu.SemaphoreType.DMA((steps + hist_rows,)), pltpu.SemaphoreType.DMA((n_results,))],
        compiler_params=pltpu.CompilerParams(vmem_limit_bytes=VMEM_LIMIT_BYTES),
        name="sample_layer",
    )(ws4, bs4, c_s, c_p, b_c.reshape(1, 3 * d), alg, alb, wconv, bconv, lng, lnb,
      x, state, w_c, win, wpa, wpb, wo)
    return y, conv, v, weights_bf16, mod_p


def kernel(x_prompt, x_sample, state_conv, c_prompt, c_sample, w_c, b_c, w_in, a_ln_g, a_ln_b,
           w_s, b_s, w_conv, b_conv, w_pa, w_pb, w_o, ln_g, ln_b):
    depth = w_in.shape[0]
    d = x_prompt.shape[-1]
    steps = x_sample.shape[1]
    alpha = (2.0 * depth) ** 0.25

    xp, xs = x_prompt, x_sample
    conv_p_rows, conv_s_rows, v_rows = [], [], []
    for l in range(depth):
        row = lambda a: a.reshape(1, d)
        small = (row(a_ln_g[l]), row(a_ln_b[l]), w_conv[l], row(b_conv[l]), row(ln_g[l]), row(ln_b[l]))
        ws4 = w_s[l][:, :steps, :steps].reshape(GROUPS, steps * steps)
        bs4 = b_s[l][:, :steps]
        xs, conv_s, v_s, weights_bf16, mod_p = _sample_call(
            xs, state_conv[l], c_sample, c_prompt, w_c[l], b_c[l],
            w_in[l], w_pa[l], w_pb[l], w_o[l], ws4, bs4, *small, alpha)
        xp, conv_p = _prompt_call(xp, mod_p, *weights_bf16, w_s[l], b_s[l].T, *small, alpha)
        conv_p_rows.append(conv_p)
        conv_s_rows.append(conv_s)
        v_rows.append(v_s)
    stack = (lambda rows: rows[0][None]) if depth == 1 else jnp.stack
    return (xp, xs, stack(conv_p_rows), stack(conv_s_rows), stack(v_rows))
```

```python
import functools

import jax
import jax.numpy as jnp
from jax import lax
from jax.experimental import pallas as pl
from jax.experimental.pallas import tpu as pltpu

CHUNK = 128
GROUPS = 8
CONV_WIDTH = 3
LN_EPS = 1e-5
ROW_TILE = 512
SUB_TILE = 256
VMEM_LIMIT_BYTES = 62 * 1024 * 1024

_F32 = jnp.float32
_BF16 = jnp.bfloat16


def _dot(a, b):
    return jnp.dot(a, b, preferred_element_type=_F32)


def _layer_norm(x, gain, bias):
    mu = jnp.mean(x, axis=-1, keepdims=True)
    xc = x - mu
    var = jnp.mean(xc * xc, axis=-1, keepdims=True)
    return xc * lax.rsqrt(var + LN_EPS) * gain + bias


def _silu(x):
    return x * jax.nn.sigmoid(x)


def _prompt_kernel(x_ref, mod_ref, win_ref, wpa_ref, wpb_ref, wo_ref, ws_ref, bst_ref,
                   alg_ref, alb_ref, wconv_ref, bconv_ref, lng_ref, lnb_ref,
                   y_hbm, conv_ref, wsm_scr, bias_scr, z_scr, ybuf, mg_scr, xb_scr, gate_scr, ysem,
                   *, alpha, n_steps, steps_per_seq):
    tm, d = x_ref.shape[1], x_ref.shape[2]
    sub = SUB_TILE
    gd = d // GROUPS
    i = pl.program_id(0)
    slot = i % 2

    def y_copy(step, part):
        row0 = pl.multiple_of(step * tm + (part - 1) * sub, sub)
        return pltpu.make_async_copy(ybuf.at[step % 2, part * sub:(part + 1) * sub, :],
                                     y_hbm.at[pl.ds(row0, sub), :], ysem.at[step % 2, part])

    @pl.when(i == 0)
    def _():
        row = lax.broadcasted_iota(jnp.int32, (CHUNK, CHUNK), 0)
        col = lax.broadcasted_iota(jnp.int32, (CHUNK, CHUNK), 1)
        for g in range(GROUPS):
            wsm_scr[g] = jnp.where(col <= row, ws_ref[g], 0.0).astype(_BF16)
            bias_scr[:, g * gd:(g + 1) * gd] = jnp.broadcast_to(bst_ref[:, g:g + 1], (CHUNK, gd))
        mg_scr[...] = jnp.zeros(mg_scr.shape, _BF16)
        xb_scr[...] = jnp.zeros(xb_scr.shape, _F32)
        gate_scr[...] = jnp.zeros(gate_scr.shape, _F32)

    @pl.when(i % steps_per_seq == 0)
    def _():
        z_scr[0:8, :] = jnp.zeros((8, d), _F32)

    @pl.when(i >= 3)
    def _():
        y_copy(i - 2, 0).wait()

    @pl.when(i >= 2)
    def _():
        y_copy(i - 2, 1).wait()

    def drain_dot():
        return _dot(mg_scr[...], wo_ref[...])

    def drain_norm(y):
        ybuf[slot, 0:sub, :] = _layer_norm(alpha * xb_scr[...] + (1.0 + gate_scr[...]) * y,
                                           lng_ref[...], lnb_ref[...])

    @pl.when(i < n_steps)
    def _():
        mod = mod_ref[pl.ds(i // steps_per_seq, 1), :]
        shift, scale, gate = mod[:, 0:d], mod[:, d:2 * d], mod[:, 2 * d:3 * d]
        bias = bias_scr[...]
        wconv = wconv_ref[...]
        starts = range(0, tm, sub)

        def project(r0):
            x = x_ref[0, r0:r0 + sub, :]
            h = (x * (1.0 + scale) + shift).astype(_BF16)
            return [_dot(h, win_ref[:, k * d:(k + 1) * d]) for k in range(9)]

        def mix(r0, p):
            u, v, z_a, b_g, c_g, h_b, z_b, g_a, g_b = p
            vb = _layer_norm(v, alg_ref[...], alb_ref[...]).astype(_BF16)
            s_cols = []
            for g in range(GROUPS):
                cols = slice(g * gd, (g + 1) * gd)
                pair_rows = []
                for c in range(0, sub // CHUNK, 2):
                    rhs = jnp.concatenate([vb[c * CHUNK:(c + 1) * CHUNK, cols],
                                           vb[(c + 1) * CHUNK:(c + 2) * CHUNK, cols]], axis=1)
                    res = _dot(wsm_scr[g], rhs)
                    pair_rows.append(res[:, :gd] + bias[:, cols])
                    pair_rows.append(res[:, gd:] + bias[:, cols])
                s_cols.append(jnp.concatenate(pair_rows, axis=0))
            s = jnp.concatenate(s_cols, axis=1)
            out_a = (u * s * _silu(z_a)).astype(_BF16)
            z = c_g * h_b
            z_scr[8 + r0:8 + r0 + sub, :] = z
            conv = (bconv_ref[...] + wconv[0:1, :] * z_scr[6 + r0:6 + r0 + sub, :]
                    + wconv[1:2, :] * z_scr[7 + r0:7 + r0 + sub, :] + wconv[2:3, :] * z)
            out_b = (b_g * conv * _silu(z_b)).astype(_BF16)
            return out_a, out_b, jax.nn.sigmoid(g_a), jax.nn.sigmoid(g_b)

        def merge(m):
            out_a, out_b, sg_a, sg_b = m
            return (sg_a * _dot(out_a, wpa_ref[...]) + sg_b * _dot(out_b, wpb_ref[...])).astype(_BF16)

        y_prev = drain_dot()
        projected = [project(r0) for r0 in starts]
        drain_norm(y_prev)
        mixed = [mix(r0, p) for r0, p in zip(starts, projected)]
        for r0, m in zip(starts[:-1], mixed[:-1]):
            x = x_ref[0, r0:r0 + sub, :]
            ybuf[slot, sub + r0:2 * sub + r0, :] = _layer_norm(
                alpha * x + (1.0 + gate) * _dot(merge(m), wo_ref[...]), lng_ref[...], lnb_ref[...])
        mg_scr[...] = merge(mixed[-1])
        xb_scr[...] = x_ref[0, tm - sub:tm, :]
        gate_scr[...] = gate

        last = z_scr[8 + tm - (CONV_WIDTH - 1):8 + tm, :]
        z_scr[6:8, :] = last
        conv_ref[0] = last

        @pl.when(i > 0)
        def _():
            y_copy(i, 0).start()

        y_copy(i, 1).start()

    @pl.when(i == n_steps)
    def _():
        drain_norm(drain_dot())
        y_copy(i, 0).start()
        y_copy(i, 0).wait()
        y_copy(i - 1, 0).wait()
        y_copy(i - 1, 1).wait()


def _resident(shape):
    nd = len(shape)
    return pl.BlockSpec(shape, lambda i: (0,) * nd, pipeline_mode=pl.Buffered(1))


def _prompt_call(x, mod, win, wpa, wpb, wo, w_s, b_s_t, alg, alb, wconv, bconv, lng, lnb, alpha):
    nb, length, d = x.shape
    tm = ROW_TILE
    assert tm == 2 * SUB_TILE and length % tm == 0 and SUB_TILE % (2 * CHUNK) == 0
    assert d % GROUPS == 0 and d // GROUPS == CHUNK
    steps_per_seq = length // tm
    n_steps = nb * steps_per_seq
    assert n_steps >= 2
    small = [alg, alb, wconv, bconv, lng, lnb]

    def tile(i):
        i = jnp.minimum(i, n_steps - 1)
        return i // steps_per_seq, i % steps_per_seq

    y, conv = pl.pallas_call(
        functools.partial(_prompt_kernel, alpha=alpha, n_steps=n_steps, steps_per_seq=steps_per_seq),
        grid=(n_steps + 1,),
        in_specs=[
            pl.BlockSpec((1, tm, d), lambda i: (*tile(i), 0)),
            _resident(mod.shape),
            _resident(win.shape), _resident(wpa.shape), _resident(wpb.shape), _resident(wo.shape),
            _resident(w_s.shape), _resident(b_s_t.shape),
        ] + [_resident(a.shape) for a in small],
        out_specs=[
            pl.BlockSpec(memory_space=pl.ANY),
            pl.BlockSpec((1, CONV_WIDTH - 1, d), lambda i: (tile(i)[0], 0, 0)),
        ],
        out_shape=[
            jax.ShapeDtypeStruct((nb * length, d), _F32),
            jax.ShapeDtypeStruct((nb, CONV_WIDTH - 1, d), _F32),
        ],
        scratch_shapes=[
            pltpu.VMEM((GROUPS, CHUNK, CHUNK), _BF16),
            pltpu.VMEM((CHUNK, d), _F32),
            pltpu.VMEM((8 + tm, d), _F32),
            pltpu.VMEM((2, tm, d), _F32),
            pltpu.VMEM((SUB_TILE, d), _BF16),
            pltpu.VMEM((SUB_TILE, d), _F32),
            pltpu.VMEM((1, d), _F32),
            pltpu.SemaphoreType.DMA((2, 2)),
        ],
        compiler_params=pltpu.CompilerParams(
            dimension_semantics=("arbitrary",),
            vmem_limit_bytes=VMEM_LIMIT_BYTES),
        name="prompt_layer",
    )(x, mod, win, wpa, wpb, wo, w_s, b_s_t, *small)
    return y.reshape(nb, length, d), conv


_PROJ_ORDER = (1, 0, 2, 4, 5, 3, 6, 7, 8)


def _sample_kernel(ws4_ref, bs4_ref, cs_ref, cp_ref, bc_ref,
                   alg_ref, alb_ref, wconv_ref, bconv_ref, lng_ref, lnb_ref,
                   x_hbm, st_hbm, wc_hbm, win_hbm, wpa_hbm, wpb_hbm, wo_hbm,
                   y_hbm, conv_hbm, v_hbm, win_out, wpa_out, wpb_out, wo_out, modp_ref,
                   win_ref, wpa_ref, wpb_ref, wo_ref, stage, xbuf, stbuf, ybuf, cvbuf, vbuf,
                   sem, exp_sem, in_sem, res_sem, *, alpha):
    steps, n, d = xbuf.shape
    gd = d // GROUPS
    hist_rows = CONV_WIDTH - 1

    in_copies = ([pltpu.make_async_copy(x_hbm.at[:, t, :], xbuf.at[t], in_sem.at[t])
                  for t in range(steps)]
                 + [pltpu.make_async_copy(st_hbm.at[:, k, :], stbuf.at[k], in_sem.at[steps + k])
                    for k in range(hist_rows)])
    for cp in in_copies:
        cp.start()

    blocks = ([(wc_hbm, None, None, j) for j in range(wc_hbm.shape[1] // d)]
              + [(win_hbm, win_ref, win_out, k) for k in _PROJ_ORDER]
              + [(wpa_hbm, wpa_ref, wpa_out, 0), (wpb_hbm, wpb_ref, wpb_out, 0),
                 (wo_hbm, wo_ref, wo_out, 0)])
    exports = []
    position = [0]

    def stage_copy(i):
        src, _, _, k = blocks[i]
        return pltpu.make_async_copy(src.at[:, k * d:(k + 1) * d], stage.at[i % 2], sem.at[i % 2])

    def next_weight():
        i = position[0]
        position[0] += 1
        if i + 1 < len(blocks):
            stage_copy(i + 1).start()
        stage_copy(i).wait()
        w = stage[i % 2].astype(_BF16)
        _, keep, out, k = blocks[i]
        if keep is not None:
            cols = slice(k * d, (k + 1) * d)
            keep[:, cols] = w
            cp = pltpu.make_async_copy(keep.at[:, cols], out.at[:, cols], exp_sem.at[len(exports)])
            cp.start()
            exports.append(cp)
        return w

    stage_copy(0).start()

    c_all = jnp.concatenate([cs_ref[...], cp_ref[...]], axis=0).astype(_BF16)
    bc = bc_ref[...]
    mods = []
    for j in range(3):
        m = _dot(c_all, next_weight()) + bc[:, j * d:(j + 1) * d]
        modp_ref[:, j * d:(j + 1) * d] = m[n:, :]
        mods.append(m[0:n, :])
    shift, scale, gate = mods

    for cp in in_copies:
        cp.wait()
    h = jnp.concatenate([(xbuf[t] * (1.0 + scale) + shift).astype(_BF16) for t in range(steps)],
                        axis=0)

    def rows(a, t):
        return a[t * n:(t + 1) * n, :]

    def proj():
        return _dot(h, next_weight())

    results = []

    def send(buf, t, dst):
        cp = pltpu.make_async_copy(buf.at[t], dst.at[:, t, :], res_sem.at[len(results)])
        cp.start()
        results.append(cp)

    vn = _layer_norm(proj(), alg_ref[...], alb_ref[...])
    s_rows = []
    for t in range(steps):
        vbuf[t] = rows(vn, t)
        send(vbuf, t, v_hbm)
        s_cols = []
        for g in range(GROUPS):
            cols = slice(g * gd, (g + 1) * gd)
            acc = jnp.full((n, gd), bs4_ref[g, t], _F32)
            for jj in range(t + 1):
                acc = acc + ws4_ref[g, t * steps + jj] * rows(vn, jj)[:, cols]
            s_cols.append(acc)
        s_rows.append(jnp.concatenate(s_cols, axis=1))
    s = jnp.concatenate(s_rows, axis=0)
    u = proj()
    out_a = (u * s * _silu(proj())).astype(_BF16)

    z = proj()
    z = z * proj()
    hist = [stbuf[k] for k in range(hist_rows)] + [rows(z, t) for t in range(steps)]
    for k in range(hist_rows):
        cvbuf[k] = hist[steps + k]
        send(cvbuf, k, conv_hbm)
    wconv = wconv_ref[...]
    conv = jnp.concatenate(
        [bconv_ref[...] + sum(wconv[k:k + 1, :] * hist[t + k] for k in range(CONV_WIDTH))
         for t in range(steps)], axis=0)
    b_g = proj()
    out_b = (b_g * conv * _silu(proj())).astype(_BF16)

    sg_a = jax.nn.sigmoid(proj())
    sg_b = jax.nn.sigmoid(proj())
    pa = _dot(out_a, next_weight())
    pb = _dot(out_b, next_weight())
    merged = (sg_a * pa + sg_b * pb).astype(_BF16)
    y = _dot(merged, next_weight())
    for t in range(steps):
        ybuf[t] = _layer_norm(alpha * xbuf[t] + (1.0 + gate) * rows(y, t), lng_ref[...], lnb_ref[...])
        send(ybuf, t, y_hbm)

    for cp in exports + results:
        cp.wait()


def _sample_call(x, state, c_s, c_p, w_c, b_c, win, wpa, wpb, wo, ws4, bs4,
                 alg, alb, wconv, bconv, lng, lnb, alpha):
    n, steps, d = x.shape
    nb = c_p.shape[0]
    hist_rows = CONV_WIDTH - 1
    assert hist_rows <= steps <= CHUNK and w_c.shape == (d, 3 * d)
    vmem = pl.BlockSpec(memory_space=pltpu.VMEM)
    smem = pl.BlockSpec(memory_space=pltpu.SMEM)
    hbm = pl.BlockSpec(memory_space=pl.ANY)
    weights = (win, wpa, wpb, wo)
    n_exports = sum(w.shape[1] // d for w in weights)
    n_results = 2 * steps + hist_rows
    y, conv, v, *weights_bf16, mod_p = pl.pallas_call(
        functools.partial(_sample_kernel, alpha=alpha),
        in_specs=[smem, smem] + [vmem] * 9 + [hbm] * 7,
        out_specs=[hbm] * (3 + len(weights)) + [vmem],
        out_shape=[
            jax.ShapeDtypeStruct((n, steps, d), _F32),
            jax.ShapeDtypeStruct((n, hist_rows, d), _F32),
            jax.ShapeDtypeStruct((n, steps, d), _F32),
        ] + [jax.ShapeDtypeStruct(w.shape, _BF16) for w in weights] + [
            jax.ShapeDtypeStruct((nb, 3 * d), _F32)],
        scratch_shapes=[pltpu.VMEM(w.shape, _BF16) for w in weights] + [
            pltpu.VMEM((2, d, d), _F32),
            pltpu.VMEM((steps, n, d), _F32), pltpu.VMEM((hist_rows, n, d), _F32),
            pltpu.VMEM((steps, n, d), _F32), pltpu.VMEM((hist_rows, n, d), _F32),
            pltpu.VMEM((steps, n, d), _F32),
            pltpu.SemaphoreType.DMA((2,)), pltpu.SemaphoreType.DMA((n_exports,)),
            pltpu.SemaphoreType.DMA((steps + hist_rows,)), pltpu.SemaphoreType.DMA((n_results,))],
        compiler_params=pltpu.CompilerParams(vmem_limit_bytes=VMEM_LIMIT_BYTES),
        name="sample_layer",
    )(ws4, bs4, c_s, c_p, b_c.reshape(1, 3 * d), alg, alb, wconv, bconv, lng, lnb,
      x, state, w_c, win, wpa, wpb, wo)
    return y, conv, v, weights_bf16, mod_p


def kernel(x_prompt, x_sample, state_conv, c_prompt, c_sample, w_c, b_c, w_in, a_ln_g, a_ln_b,
           w_s, b_s, w_conv, b_conv, w_pa, w_pb, w_o, ln_g, ln_b):
    depth = w_in.shape[0]
    d = x_prompt.shape[-1]
    steps = x_sample.shape[1]
    alpha = (2.0 * depth) ** 0.25

    xp, xs = x_prompt, x_sample
    conv_p_rows, conv_s_rows, v_rows = [], [], []
    for l in range(depth):
        row = lambda a: a.reshape(1, d)
        small = (row(a_ln_g[l]), row(a_ln_b[l]), w_conv[l], row(b_conv[l]), row(ln_g[l]), row(ln_b[l]))
        ws4 = w_s[l][:, :steps, :steps].reshape(GROUPS, steps * steps)
        bs4 = b_s[l][:, :steps]
        xs, conv_s, v_s, weights_bf16, mod_p = _sample_call(
            xs, state_conv[l], c_sample, c_prompt, w_c[l], b_c[l],
            w_in[l], w_pa[l], w_pb[l], w_o[l], ws4, bs4, *small, alpha)
        xp, conv_p = _prompt_call(xp, mod_p, *weights_bf16, w_s[l], b_s[l].T, *small, alpha)
        conv_p_rows.append(conv_p)
        conv_s_rows.append(conv_s)
        v_rows.append(v_s)
    stack = (lambda rows: rows[0][None]) if depth == 1 else jnp.stack
    return (xp, xs, stack(conv_p_rows), stack(conv_s_rows), stack(v_rows))
```

```python
import functools

import jax
import jax.numpy as jnp
from jax import lax
from jax.experimental import pallas as pl
from jax.experimental.pallas import tpu as pltpu

CHUNK = 128
GROUPS = 8
CONV_WIDTH = 3
LN_EPS = 1e-5
ROW_TILE = 512
SUB_TILE = 256
VMEM_LIMIT_BYTES = 62 * 1024 * 1024

_F32 = jnp.float32
_BF16 = jnp.bfloat16


def _dot(a, b):
    return jnp.dot(a, b, preferred_element_type=_F32)


def _layer_norm(x, gain, bias):
    mu = jnp.mean(x, axis=-1, keepdims=True)
    xc = x - mu
    var = jnp.mean(xc * xc, axis=-1, keepdims=True)
    return xc * lax.rsqrt(var + LN_EPS) * gain + bias


def _silu(x):
    return x * jax.nn.sigmoid(x)


_PACKED = jnp.uint32


def _packed_shape(w):
    return (w.shape[0] // 2, w.shape[1])


def _as_bf16(words):
    return pltpu.bitcast(words, _BF16)


def _prompt_kernel(x_ref, mod_ref, win_ref, wpa_ref, wpb_ref, wo_ref, ws_ref, bst_ref,
                   alg_ref, alb_ref, wconv_ref, bconv_ref, lng_ref, lnb_ref,
                   y_hbm, conv_ref, wsm_scr, bias_scr, z_scr, ybuf, mg_scr, xb_scr, gate_scr, ysem,
                   *, alpha, n_steps, steps_per_seq):
    tm, d = x_ref.shape[1], x_ref.shape[2]
    sub = SUB_TILE
    gd = d // GROUPS
    i = pl.program_id(0)
    slot = i % 2

    def y_copy(step, part):
        row0 = pl.multiple_of(step * tm + (part - 1) * sub, sub)
        return pltpu.make_async_copy(ybuf.at[step % 2, part * sub:(part + 1) * sub, :],
                                     y_hbm.at[pl.ds(row0, sub), :], ysem.at[step % 2, part])

    @pl.when(i == 0)
    def _():
        row = lax.broadcasted_iota(jnp.int32, (CHUNK, CHUNK), 0)
        col = lax.broadcasted_iota(jnp.int32, (CHUNK, CHUNK), 1)
        for g in range(GROUPS):
            wsm_scr[g] = jnp.where(col <= row, ws_ref[g], 0.0).astype(_BF16)
            bias_scr[:, g * gd:(g + 1) * gd] = jnp.broadcast_to(bst_ref[:, g:g + 1], (CHUNK, gd))
        mg_scr[...] = jnp.zeros(mg_scr.shape, _BF16)
        xb_scr[...] = jnp.zeros(xb_scr.shape, _F32)
        gate_scr[...] = jnp.zeros(gate_scr.shape, _F32)

    @pl.when(i % steps_per_seq == 0)
    def _():
        z_scr[0:8, :] = jnp.zeros((8, d), _F32)

    @pl.when(i >= 3)
    def _():
        y_copy(i - 2, 0).wait()

    @pl.when(i >= 2)
    def _():
        y_copy(i - 2, 1).wait()

    def drain_dot():
        return _dot(mg_scr[...], _as_bf16(wo_ref[...]))

    def drain_norm(y):
        ybuf[slot, 0:sub, :] = _layer_norm(alpha * xb_scr[...] + (1.0 + gate_scr[...]) * y,
                                           lng_ref[...], lnb_ref[...])

    @pl.when(i < n_steps)
    def _():
        mod = mod_ref[pl.ds(i // steps_per_seq, 1), :]
        shift, scale, gate = mod[:, 0:d], mod[:, d:2 * d], mod[:, 2 * d:3 * d]
        bias = bias_scr[...]
        wconv = wconv_ref[...]
        starts = range(0, tm, sub)

        def project(r0):
            x = x_ref[0, r0:r0 + sub, :]
            h = (x * (1.0 + scale) + shift).astype(_BF16)
            return [_dot(h, _as_bf16(win_ref[:, k * d:(k + 1) * d])) for k in range(9)]

        def mix(r0, p):
            u, v, z_a, b_g, c_g, h_b, z_b, g_a, g_b = p
            vb = _layer_norm(v, alg_ref[...], alb_ref[...]).astype(_BF16)
            s_cols = []
            for g in range(GROUPS):
                cols = slice(g * gd, (g + 1) * gd)
                pair_rows = []
                for c in range(0, sub // CHUNK, 2):
                    rhs = jnp.concatenate([vb[c * CHUNK:(c + 1) * CHUNK, cols],
                                           vb[(c + 1) * CHUNK:(c + 2) * CHUNK, cols]], axis=1)
                    res = _dot(wsm_scr[g], rhs)
                    pair_rows.append(res[:, :gd] + bias[:, cols])
                    pair_rows.append(res[:, gd:] + bias[:, cols])
                s_cols.append(jnp.concatenate(pair_rows, axis=0))
            s = jnp.concatenate(s_cols, axis=1)
            out_a = (u * s * _silu(z_a)).astype(_BF16)
            z = c_g * h_b
            z_scr[8 + r0:8 + r0 + sub, :] = z
            conv = (bconv_ref[...] + wconv[0:1, :] * z_scr[6 + r0:6 + r0 + sub, :]
                    + wconv[1:2, :] * z_scr[7 + r0:7 + r0 + sub, :] + wconv[2:3, :] * z)
            out_b = (b_g * conv * _silu(z_b)).astype(_BF16)
            return out_a, out_b, jax.nn.sigmoid(g_a), jax.nn.sigmoid(g_b)

        def merge(m):
            out_a, out_b, sg_a, sg_b = m
            return (sg_a * _dot(out_a, _as_bf16(wpa_ref[...]))
                    + sg_b * _dot(out_b, _as_bf16(wpb_ref[...]))).astype(_BF16)

        y_prev = drain_dot()
        projected = [project(r0) for r0 in starts]
        drain_norm(y_prev)
        mixed = [mix(r0, p) for r0, p in zip(starts, projected)]
        for r0, m in zip(starts[:-1], mixed[:-1]):
            x = x_ref[0, r0:r0 + sub, :]
            ybuf[slot, sub + r0:2 * sub + r0, :] = _layer_norm(
                alpha * x + (1.0 + gate) * _dot(merge(m), _as_bf16(wo_ref[...])),
                lng_ref[...], lnb_ref[...])
        mg_scr[...] = merge(mixed[-1])
        xb_scr[...] = x_ref[0, tm - sub:tm, :]
        gate_scr[...] = gate

        last = z_scr[8 + tm - (CONV_WIDTH - 1):8 + tm, :]
        z_scr[6:8, :] = last
        conv_ref[0] = last

        @pl.when(i > 0)
        def _():
            y_copy(i, 0).start()

        y_copy(i, 1).start()

    @pl.when(i == n_steps)
    def _():
        drain_norm(drain_dot())
        y_copy(i, 0).start()
        y_copy(i, 0).wait()
        y_copy(i - 1, 0).wait()
        y_copy(i - 1, 1).wait()


def _resident(shape):
    nd = len(shape)
    return pl.BlockSpec(shape, lambda i: (0,) * nd, pipeline_mode=pl.Buffered(1))


def _prompt_call(x, mod, win, wpa, wpb, wo, w_s, b_s_t, alg, alb, wconv, bconv, lng, lnb, alpha):
    nb, length, d = x.shape
    tm = ROW_TILE
    assert tm == 2 * SUB_TILE and length % tm == 0 and SUB_TILE % (2 * CHUNK) == 0
    assert d % GROUPS == 0 and d // GROUPS == CHUNK
    steps_per_seq = length // tm
    n_steps = nb * steps_per_seq
    assert n_steps >= 2
    small = [alg, alb, wconv, bconv, lng, lnb]

    def tile(i):
        i = jnp.minimum(i, n_steps - 1)
        return i // steps_per_seq, i % steps_per_seq

    y, conv = pl.pallas_call(
        functools.partial(_prompt_kernel, alpha=alpha, n_steps=n_steps, steps_per_seq=steps_per_seq),
        grid=(n_steps + 1,),
        in_specs=[
            pl.BlockSpec((1, tm, d), lambda i: (*tile(i), 0)),
            _resident(mod.shape),
            _resident(win.shape), _resident(wpa.shape), _resident(wpb.shape), _resident(wo.shape),
            _resident(w_s.shape), _resident(b_s_t.shape),
        ] + [_resident(a.shape) for a in small],
        out_specs=[
            pl.BlockSpec(memory_space=pl.ANY),
            pl.BlockSpec((1, CONV_WIDTH - 1, d), lambda i: (tile(i)[0], 0, 0)),
        ],
        out_shape=[
            jax.ShapeDtypeStruct((nb * length, d), _F32),
            jax.ShapeDtypeStruct((nb, CONV_WIDTH - 1, d), _F32),
        ],
        scratch_shapes=[
            pltpu.VMEM((GROUPS, CHUNK, CHUNK), _BF16),
            pltpu.VMEM((CHUNK, d), _F32),
            pltpu.VMEM((8 + tm, d), _F32),
            pltpu.VMEM((2, tm, d), _F32),
            pltpu.VMEM((SUB_TILE, d), _BF16),
            pltpu.VMEM((SUB_TILE, d), _F32),
            pltpu.VMEM((1, d), _F32),
            pltpu.SemaphoreType.DMA((2, 2)),
        ],
        compiler_params=pltpu.CompilerParams(
            dimension_semantics=("arbitrary",),
            vmem_limit_bytes=VMEM_LIMIT_BYTES),
        name="prompt_layer",
    )(x, mod, win, wpa, wpb, wo, w_s, b_s_t, *small)
    return y.reshape(nb, length, d), conv


_PROJ_ORDER = (1, 0, 2, 4, 5, 3, 6, 7, 8)


def _sample_kernel(ws4_ref, bs4_ref, cs_ref, cp_ref, bc_ref,
                   alg_ref, alb_ref, wconv_ref, bconv_ref, lng_ref, lnb_ref,
                   x_hbm, st_hbm, wc_hbm, win_hbm, wpa_hbm, wpb_hbm, wo_hbm,
                   y_hbm, conv_hbm, v_hbm, win_out, wpa_out, wpb_out, wo_out, modp_ref,
                   win_ref, wpa_ref, wpb_ref, wo_ref, stage, xbuf, stbuf, ybuf, cvbuf, vbuf,
                   sem, exp_sem, in_sem, res_sem, *, alpha):
    steps, n, d = xbuf.shape
    gd = d // GROUPS
    hist_rows = CONV_WIDTH - 1

    in_copies = ([pltpu.make_async_copy(x_hbm.at[:, t, :], xbuf.at[t], in_sem.at[t])
                  for t in range(steps)]
                 + [pltpu.make_async_copy(st_hbm.at[:, k, :], stbuf.at[k], in_sem.at[steps + k])
                    for k in range(hist_rows)])
    for cp in in_copies:
        cp.start()

    blocks = ([(wc_hbm, None, None, j) for j in range(wc_hbm.shape[1] // d)]
              + [(win_hbm, win_ref, win_out, k) for k in _PROJ_ORDER]
              + [(wpa_hbm, wpa_ref, wpa_out, 0), (wpb_hbm, wpb_ref, wpb_out, 0),
                 (wo_hbm, wo_ref, wo_out, 0)])
    exports = []
    position = [0]

    def stage_copy(i):
        src, _, _, k = blocks[i]
        return pltpu.make_async_copy(src.at[:, k * d:(k + 1) * d], stage.at[i % 2], sem.at[i % 2])

    def next_weight():
        i = position[0]
        position[0] += 1
        if i + 1 < len(blocks):
            stage_copy(i + 1).start()
        stage_copy(i).wait()
        w = stage[i % 2].astype(_BF16)
        _, keep, out, k = blocks[i]
        if keep is not None:
            cols = slice(k * d, (k + 1) * d)
            keep[:, cols] = pltpu.bitcast(w, _PACKED)
            cp = pltpu.make_async_copy(keep.at[:, cols], out.at[:, cols], exp_sem.at[len(exports)])
            cp.start()
            exports.append(cp)
        return w

    stage_copy(0).start()

    c_all = jnp.concatenate([cs_ref[...], cp_ref[...]], axis=0).astype(_BF16)
    bc = bc_ref[...]
    mods = []
    for j in range(3):
        m = _dot(c_all, next_weight()) + bc[:, j * d:(j + 1) * d]
        modp_ref[:, j * d:(j + 1) * d] = m[n:, :]
        mods.append(m[0:n, :])
    shift, scale, gate = mods

    for cp in in_copies:
        cp.wait()
    h = jnp.concatenate([(xbuf[t] * (1.0 + scale) + shift).astype(_BF16) for t in range(steps)],
                        axis=0)

    def rows(a, t):
        return a[t * n:(t + 1) * n, :]

    def proj():
        return _dot(h, next_weight())

    results = []

    def send(buf, t, dst):
        cp = pltpu.make_async_copy(buf.at[t], dst.at[:, t, :], res_sem.at[len(results)])
        cp.start()
        results.append(cp)

    vn = _layer_norm(proj(), alg_ref[...], alb_ref[...])
    s_rows = []
    for t in range(steps):
        vbuf[t] = rows(vn, t)
        send(vbuf, t, v_hbm)
        s_cols = []
        for g in range(GROUPS):
            cols = slice(g * gd, (g + 1) * gd)
            acc = jnp.full((n, gd), bs4_ref[g, t], _F32)
            for jj in range(t + 1):
                acc = acc + ws4_ref[g, t * steps + jj] * rows(vn, jj)[:, cols]
            s_cols.append(acc)
        s_rows.append(jnp.concatenate(s_cols, axis=1))
    s = jnp.concatenate(s_rows, axis=0)
    u = proj()
    out_a = (u * s * _silu(proj())).astype(_BF16)

    z = proj()
    z = z * proj()
    hist = [stbuf[k] for k in range(hist_rows)] + [rows(z, t) for t in range(steps)]
    for k in range(hist_rows):
        cvbuf[k] = hist[steps + k]
        send(cvbuf, k, conv_hbm)
    wconv = wconv_ref[...]
    conv = jnp.concatenate(
        [bconv_ref[...] + sum(wconv[k:k + 1, :] * hist[t + k] for k in range(CONV_WIDTH))
         for t in range(steps)], axis=0)
    b_g = proj()
    out_b = (b_g * conv * _silu(proj())).astype(_BF16)

    sg_a = jax.nn.sigmoid(proj())
    sg_b = jax.nn.sigmoid(proj())
    pa = _dot(out_a, next_weight())
    pb = _dot(out_b, next_weight())
    merged = (sg_a * pa + sg_b * pb).astype(_BF16)
    y = _dot(merged, next_weight())
    for t in range(steps):
        ybuf[t] = _layer_norm(alpha * xbuf[t] + (1.0 + gate) * rows(y, t), lng_ref[...], lnb_ref[...])
        send(ybuf, t, y_hbm)

    for cp in exports + results:
        cp.wait()


def _sample_call(x, state, c_s, c_p, w_c, b_c, win, wpa, wpb, wo, ws4, bs4,
                 alg, alb, wconv, bconv, lng, lnb, alpha):
    n, steps, d = x.shape
    nb = c_p.shape[0]
    hist_rows = CONV_WIDTH - 1
    assert hist_rows <= steps <= CHUNK and w_c.shape == (d, 3 * d)
    vmem = pl.BlockSpec(memory_space=pltpu.VMEM)
    smem = pl.BlockSpec(memory_space=pltpu.SMEM)
    hbm = pl.BlockSpec(memory_space=pl.ANY)
    weights = (win, wpa, wpb, wo)
    n_exports = sum(w.shape[1] // d for w in weights)
    n_results = 2 * steps + hist_rows
    y, conv, v, *weights_bf16, mod_p = pl.pallas_call(
        functools.partial(_sample_kernel, alpha=alpha),
        in_specs=[smem, smem] + [vmem] * 9 + [hbm] * 7,
        out_specs=[hbm] * (3 + len(weights)) + [vmem],
        out_shape=[
            jax.ShapeDtypeStruct((n, steps, d), _F32),
            jax.ShapeDtypeStruct((n, hist_rows, d), _F32),
            jax.ShapeDtypeStruct((n, steps, d), _F32),
        ] + [jax.ShapeDtypeStruct(_packed_shape(w), _PACKED) for w in weights] + [
            jax.ShapeDtypeStruct((nb, 3 * d), _F32)],
        scratch_shapes=[pltpu.VMEM(_packed_shape(w), _PACKED) for w in weights] + [
            pltpu.VMEM((2, d, d), _F32),
            pltpu.VMEM((steps, n, d), _F32), pltpu.VMEM((hist_rows, n, d), _F32),
            pltpu.VMEM((steps, n, d), _F32), pltpu.VMEM((hist_rows, n, d), _F32),
            pltpu.VMEM((steps, n, d), _F32),
            pltpu.SemaphoreType.DMA((2,)), pltpu.SemaphoreType.DMA((n_exports,)),
            pltpu.SemaphoreType.DMA((steps + hist_rows,)), pltpu.SemaphoreType.DMA((n_results,))],
        compiler_params=pltpu.CompilerParams(vmem_limit_bytes=VMEM_LIMIT_BYTES),
        name="sample_layer",
    )(ws4, bs4, c_s, c_p, b_c.reshape(1, 3 * d), alg, alb, wconv, bconv, lng, lnb,
      x, state, w_c, win, wpa, wpb, wo)
    return y, conv, v, weights_bf16, mod_p


def kernel(x_prompt, x_sample, state_conv, c_prompt, c_sample, w_c, b_c, w_in, a_ln_g, a_ln_b,
           w_s, b_s, w_conv, b_conv, w_pa, w_pb, w_o, ln_g, ln_b):
    depth = w_in.shape[0]
    d = x_prompt.shape[-1]
    steps = x_sample.shape[1]
    alpha = (2.0 * depth) ** 0.25

    xp, xs = x_prompt, x_sample
    conv_p_rows, conv_s_rows, v_rows = [], [], []
    for l in range(depth):
        row = lambda a: a.reshape(1, d)
        small = (row(a_ln_g[l]), row(a_ln_b[l]), w_conv[l], row(b_conv[l]), row(ln_g[l]), row(ln_b[l]))
        ws4 = w_s[l][:, :steps, :steps].reshape(GROUPS, steps * steps)
        bs4 = b_s[l][:, :steps]
        xs, conv_s, v_s, weights_bf16, mod_p = _sample_call(
            xs, state_conv[l], c_sample, c_prompt, w_c[l], b_c[l],
            w_in[l], w_pa[l], w_pb[l], w_o[l], ws4, bs4, *small, alpha)
        xp, conv_p = _prompt_call(xp, mod_p, *weights_bf16, w_s[l], b_s[l].T, *small, alpha)
        conv_p_rows.append(conv_p)
        conv_s_rows.append(conv_s)
        v_rows.append(v_s)
    stack = (lambda rows: rows[0][None]) if depth == 1 else jnp.stack
    return (xp, xs, stack(conv_p_rows), stack(conv_s_rows), stack(v_rows))
```

```python
import functools

import jax
import jax.numpy as jnp
from jax import lax
from jax.experimental import pallas as pl
from jax.experimental.pallas import tpu as pltpu

CHUNK = 128
GROUPS = 8
CONV_WIDTH = 3
LN_EPS = 1e-5
ROW_TILE = 512
SUB_TILE = 256
VMEM_LIMIT_BYTES = 62 * 1024 * 1024

_F32 = jnp.float32
_BF16 = jnp.bfloat16


def _dot(a, b):
    return jnp.dot(a, b, preferred_element_type=_F32)


def _layer_norm(x, gain, bias):
    mu = jnp.mean(x, axis=-1, keepdims=True)
    xc = x - mu
    var = jnp.mean(xc * xc, axis=-1, keepdims=True)
    return xc * lax.rsqrt(var + LN_EPS) * gain + bias


def _silu(x):
    return x * jax.nn.sigmoid(x)


_PACKED = jnp.uint32


def _packed_shape(w):
    return (w.shape[0] // 2, w.shape[1])


def _as_bf16(words):
    return pltpu.bitcast(words, _BF16)


def _prompt_kernel(x_ref, mod_ref, win_ref, wpa_ref, wpb_ref, wo_ref, ws_ref, bst_ref,
                   alg_ref, alb_ref, wconv_ref, bconv_ref, lng_ref, lnb_ref,
                   y_hbm, conv_ref, wsm_scr, bias_scr, z_scr, ybuf, mg_scr, xb_scr, gate_scr, ysem,
                   *, alpha, n_steps, steps_per_seq):
    tm, d = x_ref.shape[1], x_ref.shape[2]
    sub = SUB_TILE
    gd = d // GROUPS
    i = pl.program_id(0)
    slot = i % 2

    def y_copy(step, part):
        row0 = pl.multiple_of(step * tm + (part - 1) * sub, sub)
        return pltpu.make_async_copy(ybuf.at[step % 2, part * sub:(part + 1) * sub, :],
                                     y_hbm.at[pl.ds(row0, sub), :], ysem.at[step % 2, part])

    @pl.when(i == 0)
    def _():
        row = lax.broadcasted_iota(jnp.int32, (CHUNK, CHUNK), 0)
        col = lax.broadcasted_iota(jnp.int32, (CHUNK, CHUNK), 1)
        for g in range(GROUPS):
            wsm_scr[g] = jnp.where(col <= row, ws_ref[g], 0.0).astype(_BF16)
            bias_scr[:, g * gd:(g + 1) * gd] = jnp.broadcast_to(bst_ref[:, g:g + 1], (CHUNK, gd))
        mg_scr[...] = jnp.zeros(mg_scr.shape, _BF16)
        xb_scr[...] = jnp.zeros(xb_scr.shape, _F32)
        gate_scr[...] = jnp.zeros(gate_scr.shape, _F32)

    @pl.when(i % steps_per_seq == 0)
    def _():
        z_scr[...] = jnp.zeros(z_scr.shape, _F32)

    @pl.when(i >= 3)
    def _():
        y_copy(i - 2, 0).wait()

    @pl.when(i >= 2)
    def _():
        y_copy(i - 2, 1).wait()

    def drain_dot():
        return _dot(mg_scr[...], _as_bf16(wo_ref[...]))

    def drain_norm(y):
        ybuf[slot, 0:sub, :] = _layer_norm(alpha * xb_scr[...] + (1.0 + gate_scr[0:1, :]) * y,
                                           lng_ref[...], lnb_ref[...])

    @pl.when(i < n_steps)
    def _():
        mod = mod_ref[pl.ds(i // steps_per_seq, 1), :]
        shift, scale, gate = mod[:, 0:d], mod[:, d:2 * d], mod[:, 2 * d:3 * d]
        bias = bias_scr[...]
        wconv = wconv_ref[...]
        starts = range(0, tm, sub)
        tails = []

        def project(r0):
            x = x_ref[0, r0:r0 + sub, :]
            h = (x * (1.0 + scale) + shift).astype(_BF16)
            return [_dot(h, _as_bf16(win_ref[:, k * d:(k + 1) * d])) for k in range(9)]

        def mix(r0, p):
            u, v, z_a, b_g, c_g, h_b, z_b, g_a, g_b = p
            vb = _layer_norm(v, alg_ref[...], alb_ref[...]).astype(_BF16)
            s_cols = []
            for g in range(GROUPS):
                cols = slice(g * gd, (g + 1) * gd)
                pair_rows = []
                for c in range(0, sub // CHUNK, 2):
                    rhs = jnp.concatenate([vb[c * CHUNK:(c + 1) * CHUNK, cols],
                                           vb[(c + 1) * CHUNK:(c + 2) * CHUNK, cols]], axis=1)
                    res = _dot(wsm_scr[g], rhs)
                    pair_rows.append(res[:, :gd] + bias[:, cols])
                    pair_rows.append(res[:, gd:] + bias[:, cols])
                s_cols.append(jnp.concatenate(pair_rows, axis=0))
            s = jnp.concatenate(s_cols, axis=1)
            out_a = (u * s * _silu(z_a)).astype(_BF16)
            z = c_g * h_b
            hist = z_scr[...] if not tails else tails[-1]
            tails.append(z[sub - 8:sub, :])
            top_rows = lax.broadcasted_iota(jnp.int32, (8, d), 0)

            def delayed(k):
                rolled = pltpu.roll(z, k, axis=0)
                top = jnp.where(top_rows < k, pltpu.roll(hist, k, axis=0), rolled[0:8, :])
                return jnp.concatenate([top, rolled[8:, :]], axis=0)

            conv = (bconv_ref[...] + wconv[0:1, :] * delayed(2) + wconv[1:2, :] * delayed(1)
                    + wconv[2:3, :] * z)
            out_b = (b_g * conv * _silu(z_b)).astype(_BF16)
            return out_a, out_b, jax.nn.sigmoid(g_a), jax.nn.sigmoid(g_b)

        def merge(m):
            out_a, out_b, sg_a, sg_b = m
            return (sg_a * _dot(out_a, _as_bf16(wpa_ref[...]))
                    + sg_b * _dot(out_b, _as_bf16(wpb_ref[...]))).astype(_BF16)

        y_prev = drain_dot()
        projected = [project(r0) for r0 in starts]
        drain_norm(y_prev)
        mixed = [mix(r0, p) for r0, p in zip(starts, projected)]
        for r0, m in zip(starts[:-1], mixed[:-1]):
            x = x_ref[0, r0:r0 + sub, :]
            ybuf[slot, sub + r0:2 * sub + r0, :] = _layer_norm(
                alpha * x + (1.0 + gate) * _dot(merge(m), _as_bf16(wo_ref[...])),
                lng_ref[...], lnb_ref[...])
        mg_scr[...] = merge(mixed[-1])
        xb_scr[...] = x_ref[0, tm - sub:tm, :]
        gate_scr[...] = jnp.broadcast_to(gate, gate_scr.shape)

        z_scr[...] = tails[-1]
        conv_ref[0] = tails[-1][8 - (CONV_WIDTH - 1):8, :]

        @pl.when(i > 0)
        def _():
            y_copy(i, 0).start()

        y_copy(i, 1).start()

    @pl.when(i == n_steps)
    def _():
        drain_norm(drain_dot())
        y_copy(i, 0).start()
        y_copy(i, 0).wait()
        y_copy(i - 1, 0).wait()
        y_copy(i - 1, 1).wait()


def _resident(shape):
    nd = len(shape)
    return pl.BlockSpec(shape, lambda i: (0,) * nd, pipeline_mode=pl.Buffered(1))


def _prompt_call(x, mod, win, wpa, wpb, wo, w_s, b_s_t, alg, alb, wconv, bconv, lng, lnb, alpha):
    nb, length, d = x.shape
    tm = ROW_TILE
    assert tm == 2 * SUB_TILE and length % tm == 0 and SUB_TILE % (2 * CHUNK) == 0
    assert d % GROUPS == 0 and d // GROUPS == CHUNK
    steps_per_seq = length // tm
    n_steps = nb * steps_per_seq
    assert n_steps >= 2
    small = [alg, alb, wconv, bconv, lng, lnb]

    def tile(i):
        i = jnp.minimum(i, n_steps - 1)
        return i // steps_per_seq, i % steps_per_seq

    y, conv = pl.pallas_call(
        functools.partial(_prompt_kernel, alpha=alpha, n_steps=n_steps, steps_per_seq=steps_per_seq),
        grid=(n_steps + 1,),
        in_specs=[
            pl.BlockSpec((1, tm, d), lambda i: (*tile(i), 0)),
            _resident(mod.shape),
            _resident(win.shape), _resident(wpa.shape), _resident(wpb.shape), _resident(wo.shape),
            _resident(w_s.shape), _resident(b_s_t.shape),
        ] + [_resident(a.shape) for a in small],
        out_specs=[
            pl.BlockSpec(memory_space=pl.ANY),
            pl.BlockSpec((1, CONV_WIDTH - 1, d), lambda i: (tile(i)[0], 0, 0)),
        ],
        out_shape=[
            jax.ShapeDtypeStruct((nb * length, d), _F32),
            jax.ShapeDtypeStruct((nb, CONV_WIDTH - 1, d), _F32),
        ],
        scratch_shapes=[
            pltpu.VMEM((GROUPS, CHUNK, CHUNK), _BF16),
            pltpu.VMEM((CHUNK, d), _F32),
            pltpu.VMEM((8, d), _F32),
            pltpu.VMEM((2, tm, d), _F32),
            pltpu.VMEM((SUB_TILE, d), _BF16),
            pltpu.VMEM((SUB_TILE, d), _F32),
            pltpu.VMEM((8, d), _F32),
            pltpu.SemaphoreType.DMA((2, 2)),
        ],
        compiler_params=pltpu.CompilerParams(
            dimension_semantics=("arbitrary",),
            vmem_limit_bytes=VMEM_LIMIT_BYTES),
        name="prompt_layer",
    )(x, mod, win, wpa, wpb, wo, w_s, b_s_t, *small)
    return y.reshape(nb, length, d), conv


_PROJ_ORDER = (1, 0, 2, 4, 5, 3, 6, 7, 8)


def _sample_kernel(ws4_ref, bs4_ref, cs_ref, cp_ref, bc_ref,
                   alg_ref, alb_ref, wconv_ref, bconv_ref, lng_ref, lnb_ref,
                   x_hbm, st_hbm, wc_hbm, win_hbm, wpa_hbm, wpb_hbm, wo_hbm,
                   y_hbm, conv_hbm, v_hbm, win_out, wpa_out, wpb_out, wo_out, modp_ref,
                   win_ref, wpa_ref, wpb_ref, wo_ref, stage, xbuf, stbuf, ybuf, cvbuf, vbuf,
                   sem, exp_sem, in_sem, res_sem, *, alpha):
    steps, n, d = xbuf.shape
    gd = d // GROUPS
    hist_rows = CONV_WIDTH - 1

    in_copies = ([pltpu.make_async_copy(x_hbm.at[:, t, :], xbuf.at[t], in_sem.at[t])
                  for t in range(steps)]
                 + [pltpu.make_async_copy(st_hbm.at[:, k, :], stbuf.at[k], in_sem.at[steps + k])
                    for k in range(hist_rows)])
    for cp in in_copies:
        cp.start()

    blocks = ([(wc_hbm, None, None, j) for j in range(wc_hbm.shape[1] // d)]
              + [(win_hbm, win_ref, win_out, k) for k in _PROJ_ORDER]
              + [(wpa_hbm, wpa_ref, wpa_out, 0), (wpb_hbm, wpb_ref, wpb_out, 0),
                 (wo_hbm, wo_ref, wo_out, 0)])
    exports = []
    position = [0]

    def stage_copy(i):
        src, _, _, k = blocks[i]
        return pltpu.make_async_copy(src.at[:, k * d:(k + 1) * d], stage.at[i % 2], sem.at[i % 2])

    def next_weight():
        i = position[0]
        position[0] += 1
        if i + 1 < len(blocks):
            stage_copy(i + 1).start()
        stage_copy(i).wait()
        w = stage[i % 2].astype(_BF16)
        _, keep, out, k = blocks[i]
        if keep is not None:
            cols = slice(k * d, (k + 1) * d)
            keep[:, cols] = pltpu.bitcast(w, _PACKED)
            cp = pltpu.make_async_copy(keep.at[:, cols], out.at[:, cols], exp_sem.at[len(exports)])
            cp.start()
            exports.append(cp)
        return w

    stage_copy(0).start()

    c_all = jnp.concatenate([cs_ref[...], cp_ref[...]], axis=0).astype(_BF16)
    bc = bc_ref[...]
    mods = []
    for j in range(3):
        m = _dot(c_all, next_weight()) + bc[:, j * d:(j + 1) * d]
        modp_ref[:, j * d:(j + 1) * d] = m[n:, :]
        mods.append(m[0:n, :])
    shift, scale, gate = mods

    for cp in in_copies:
        cp.wait()
    h = jnp.concatenate([(xbuf[t] * (1.0 + scale) + shift).astype(_BF16) for t in range(steps)],
                        axis=0)

    def rows(a, t):
        return a[t * n:(t + 1) * n, :]

    def proj():
        return _dot(h, next_weight())

    results = []

    def send(buf, t, dst):
        cp = pltpu.make_async_copy(buf.at[t], dst.at[:, t, :], res_sem.at[len(results)])
        cp.start()
        results.append(cp)

    vn = _layer_norm(proj(), alg_ref[...], alb_ref[...])
    s_rows = []
    for t in range(steps):
        vbuf[t] = rows(vn, t)
        send(vbuf, t, v_hbm)
        s_cols = []
        for g in range(GROUPS):
            cols = slice(g * gd, (g + 1) * gd)
            acc = jnp.full((n, gd), bs4_ref[g, t], _F32)
            for jj in range(t + 1):
                acc = acc + ws4_ref[g, t * steps + jj] * rows(vn, jj)[:, cols]
            s_cols.append(acc)
        s_rows.append(jnp.concatenate(s_cols, axis=1))
    s = jnp.concatenate(s_rows, axis=0)
    u = proj()
    out_a = (u * s * _silu(proj())).astype(_BF16)

    z = proj()
    z = z * proj()
    hist = [stbuf[k] for k in range(hist_rows)] + [rows(z, t) for t in range(steps)]
    for k in range(hist_rows):
        cvbuf[k] = hist[steps + k]
        send(cvbuf, k, conv_hbm)
    wconv = wconv_ref[...]
    conv = jnp.concatenate(
        [bconv_ref[...] + sum(wconv[k:k + 1, :] * hist[t + k] for k in range(CONV_WIDTH))
         for t in range(steps)], axis=0)
    b_g = proj()
    out_b = (b_g * conv * _silu(proj())).astype(_BF16)

    sg_a = jax.nn.sigmoid(proj())
    sg_b = jax.nn.sigmoid(proj())
    pa = _dot(out_a, next_weight())
    pb = _dot(out_b, next_weight())
    merged = (sg_a * pa + sg_b * pb).astype(_BF16)
    y = _dot(merged, next_weight())
    for t in range(steps):
        ybuf[t] = _layer_norm(alpha * xbuf[t] + (1.0 + gate) * rows(y, t), lng_ref[...], lnb_ref[...])
        send(ybuf, t, y_hbm)

    for cp in exports + results:
        cp.wait()


def _sample_call(x, state, c_s, c_p, w_c, b_c, win, wpa, wpb, wo, ws4, bs4,
                 alg, alb, wconv, bconv, lng, lnb, alpha):
    n, steps, d = x.shape
    nb = c_p.shape[0]
    hist_rows = CONV_WIDTH - 1
    assert hist_rows <= steps <= CHUNK and w_c.shape == (d, 3 * d)
    vmem = pl.BlockSpec(memory_space=pltpu.VMEM)
    smem = pl.BlockSpec(memory_space=pltpu.SMEM)
    hbm = pl.BlockSpec(memory_space=pl.ANY)
    weights = (win, wpa, wpb, wo)
    n_exports = sum(w.shape[1] // d for w in weights)
    n_results = 2 * steps + hist_rows
    y, conv, v, *weights_bf16, mod_p = pl.pallas_call(
        functools.partial(_sample_kernel, alpha=alpha),
        in_specs=[smem, smem] + [vmem] * 9 + [hbm] * 7,
        out_specs=[hbm] * (3 + len(weights)) + [vmem],
        out_shape=[
            jax.ShapeDtypeStruct((n, steps, d), _F32),
            jax.ShapeDtypeStruct((n, hist_rows, d), _F32),
            jax.ShapeDtypeStruct((n, steps, d), _F32),
        ] + [jax.ShapeDtypeStruct(_packed_shape(w), _PACKED) for w in weights] + [
            jax.ShapeDtypeStruct((nb, 3 * d), _F32)],
        scratch_shapes=[pltpu.VMEM(_packed_shape(w), _PACKED) for w in weights] + [
            pltpu.VMEM((2, d, d), _F32),
            pltpu.VMEM((steps, n, d), _F32), pltpu.VMEM((hist_rows, n, d), _F32),
            pltpu.VMEM((steps, n, d), _F32), pltpu.VMEM((hist_rows, n, d), _F32),
            pltpu.VMEM((steps, n, d), _F32),
            pltpu.SemaphoreType.DMA((2,)), pltpu.SemaphoreType.DMA((n_exports,)),
            pltpu.SemaphoreType.DMA((steps + hist_rows,)), pltpu.SemaphoreType.DMA((n_results,))],
        compiler_params=pltpu.CompilerParams(vmem_limit_bytes=VMEM_LIMIT_BYTES),
        name="sample_layer",
    )(ws4, bs4, c_s, c_p, b_c.reshape(1, 3 * d), alg, alb, wconv, bconv, lng, lnb,
      x, state, w_c, win, wpa, wpb, wo)
    return y, conv, v, weights_bf16, mod_p


def kernel(x_prompt, x_sample, state_conv, c_prompt, c_sample, w_c, b_c, w_in, a_ln_g, a_ln_b,
           w_s, b_s, w_conv, b_conv, w_pa, w_pb, w_o, ln_g, ln_b):
    depth = w_in.shape[0]
    d = x_prompt.shape[-1]
    steps = x_sample.shape[1]
    alpha = (2.0 * depth) ** 0.25

    xp, xs = x_prompt, x_sample
    conv_p_rows, conv_s_rows, v_rows = [], [], []
    for l in range(depth):
        row = lambda a: a.reshape(1, d)
        small = (row(a_ln_g[l]), row(a_ln_b[l]), w_conv[l], row(b_conv[l]), row(ln_g[l]), row(ln_b[l]))
        ws4 = w_s[l][:, :steps, :steps].reshape(GROUPS, steps * steps)
        bs4 = b_s[l][:, :steps]
        xs, conv_s, v_s, weights_bf16, mod_p = _sample_call(
            xs, state_conv[l], c_sample, c_prompt, w_c[l], b_c[l],
            w_in[l], w_pa[l], w_pb[l], w_o[l], ws4, bs4, *small, alpha)
        xp, conv_p = _prompt_call(xp, mod_p, *weights_bf16, w_s[l], b_s[l].T, *small, alpha)
        conv_p_rows.append(conv_p)
        conv_s_rows.append(conv_s)
        v_rows.append(v_s)
    stack = (lambda rows: rows[0][None]) if depth == 1 else jnp.stack
    return (xp, xs, stack(conv_p_rows), stack(conv_s_rows), stack(v_rows))
```

```python
import functools

import jax
import jax.numpy as jnp
from jax import lax
from jax.experimental import pallas as pl
from jax.experimental.pallas import tpu as pltpu

CHUNK = 128
GROUPS = 8
CONV_WIDTH = 3
LN_EPS = 1e-5
ROW_TILE = 512
SUB_TILE = 256
STAGE_SLOTS = 3
VMEM_LIMIT_BYTES = 62 * 1024 * 1024

_F32 = jnp.float32
_BF16 = jnp.bfloat16


def _dot(a, b):
    return jnp.dot(a, b, preferred_element_type=_F32)


def _layer_norm(x, gain, bias):
    mu = jnp.mean(x, axis=-1, keepdims=True)
    xc = x - mu
    var = jnp.mean(xc * xc, axis=-1, keepdims=True)
    return xc * lax.rsqrt(var + LN_EPS) * gain + bias


def _silu(x):
    return x * jax.nn.sigmoid(x)


_PACKED = jnp.uint32


def _packed_shape(w):
    return (w.shape[0] // 2, w.shape[1])


def _as_bf16(words):
    return pltpu.bitcast(words, _BF16)


def _prompt_kernel(x_ref, mod_ref, win_ref, wpa_ref, wpb_ref, wo_ref, ws_ref, bst_ref,
                   alg_ref, alb_ref, wconv_ref, bconv_ref, lng_ref, lnb_ref,
                   y_hbm, conv_ref, wsm_scr, bias_scr, z_scr, ybuf, mg_scr, xb_scr, gate_scr, ysem,
                   *, alpha, n_steps, steps_per_seq):
    tm, d = x_ref.shape[1], x_ref.shape[2]
    sub = SUB_TILE
    gd = d // GROUPS
    i = pl.program_id(0)
    slot = i % 2

    def y_copy(step, part):
        row0 = pl.multiple_of(step * tm + (part - 1) * sub, sub)
        return pltpu.make_async_copy(ybuf.at[step % 2, part * sub:(part + 1) * sub, :],
                                     y_hbm.at[pl.ds(row0, sub), :], ysem.at[step % 2, part])

    @pl.when(i == 0)
    def _():
        row = lax.broadcasted_iota(jnp.int32, (CHUNK, CHUNK), 0)
        col = lax.broadcasted_iota(jnp.int32, (CHUNK, CHUNK), 1)
        for g in range(GROUPS):
            wsm_scr[g] = jnp.where(col <= row, ws_ref[g], 0.0).astype(_BF16)
            bias_scr[:, g * gd:(g + 1) * gd] = jnp.broadcast_to(bst_ref[:, g:g + 1], (CHUNK, gd))
        mg_scr[...] = jnp.zeros(mg_scr.shape, _BF16)
        xb_scr[...] = jnp.zeros(xb_scr.shape, _F32)
        gate_scr[...] = jnp.zeros(gate_scr.shape, _F32)

    @pl.when(i % steps_per_seq == 0)
    def _():
        z_scr[...] = jnp.zeros(z_scr.shape, _F32)

    @pl.when(i >= 3)
    def _():
        y_copy(i - 2, 0).wait()

    @pl.when(i >= 2)
    def _():
        y_copy(i - 2, 1).wait()

    def drain_dot():
        return _dot(mg_scr[...], _as_bf16(wo_ref[...]))

    def drain_norm(y):
        ybuf[slot, 0:sub, :] = _layer_norm(alpha * xb_scr[...] + (1.0 + gate_scr[0:1, :]) * y,
                                           lng_ref[...], lnb_ref[...])

    @pl.when(i < n_steps)
    def _():
        mod = mod_ref[pl.ds(i // steps_per_seq, 1), :]
        shift, scale, gate = mod[:, 0:d], mod[:, d:2 * d], mod[:, 2 * d:3 * d]
        bias = bias_scr[...]
        wconv = wconv_ref[...]
        starts = range(0, tm, sub)
        tails = []

        def project(r0):
            x = x_ref[0, r0:r0 + sub, :]
            h = (x * (1.0 + scale) + shift).astype(_BF16)
            return [_dot(h, _as_bf16(win_ref[:, k * d:(k + 1) * d])) for k in range(9)]

        def mix(r0, p):
            u, v, z_a, b_g, c_g, h_b, z_b, g_a, g_b = p
            vb = _layer_norm(v, alg_ref[...], alb_ref[...]).astype(_BF16)
            s_cols = []
            for g in range(GROUPS):
                cols = slice(g * gd, (g + 1) * gd)
                pair_rows = []
                for c in range(0, sub // CHUNK, 2):
                    rhs = jnp.concatenate([vb[c * CHUNK:(c + 1) * CHUNK, cols],
                                           vb[(c + 1) * CHUNK:(c + 2) * CHUNK, cols]], axis=1)
                    res = _dot(wsm_scr[g], rhs)
                    pair_rows.append(res[:, :gd] + bias[:, cols])
                    pair_rows.append(res[:, gd:] + bias[:, cols])
                s_cols.append(jnp.concatenate(pair_rows, axis=0))
            s = jnp.concatenate(s_cols, axis=1)
            out_a = (u * s * _silu(z_a)).astype(_BF16)
            z = c_g * h_b
            hist = z_scr[...] if not tails else tails[-1]
            tails.append(z[sub - 8:sub, :])
            top_rows = lax.broadcasted_iota(jnp.int32, (8, d), 0)

            def delayed(k):
                rolled = pltpu.roll(z, k, axis=0)
                top = jnp.where(top_rows < k, pltpu.roll(hist, k, axis=0), rolled[0:8, :])
                return jnp.concatenate([top, rolled[8:, :]], axis=0)

            conv = (bconv_ref[...] + wconv[0:1, :] * delayed(2) + wconv[1:2, :] * delayed(1)
                    + wconv[2:3, :] * z)
            out_b = (b_g * conv * _silu(z_b)).astype(_BF16)
            return out_a, out_b, jax.nn.sigmoid(g_a), jax.nn.sigmoid(g_b)

        def merge(m):
            out_a, out_b, sg_a, sg_b = m
            return (sg_a * _dot(out_a, _as_bf16(wpa_ref[...]))
                    + sg_b * _dot(out_b, _as_bf16(wpb_ref[...]))).astype(_BF16)

        y_prev = drain_dot()
        projected = [project(r0) for r0 in starts]
        drain_norm(y_prev)
        mixed = [mix(r0, p) for r0, p in zip(starts, projected)]
        for r0, m in zip(starts[:-1], mixed[:-1]):
            x = x_ref[0, r0:r0 + sub, :]
            ybuf[slot, sub + r0:2 * sub + r0, :] = _layer_norm(
                alpha * x + (1.0 + gate) * _dot(merge(m), _as_bf16(wo_ref[...])),
                lng_ref[...], lnb_ref[...])
        mg_scr[...] = merge(mixed[-1])
        xb_scr[...] = x_ref[0, tm - sub:tm, :]
        gate_scr[...] = jnp.broadcast_to(gate, gate_scr.shape)

        z_scr[...] = tails[-1]
        conv_ref[0] = tails[-1][8 - (CONV_WIDTH - 1):8, :]

        @pl.when(i > 0)
        def _():
            y_copy(i, 0).start()

        y_copy(i, 1).start()

    @pl.when(i == n_steps)
    def _():
        drain_norm(drain_dot())
        y_copy(i, 0).start()
        y_copy(i, 0).wait()
        y_copy(i - 1, 0).wait()
        y_copy(i - 1, 1).wait()


def _resident(shape):
    nd = len(shape)
    return pl.BlockSpec(shape, lambda i: (0,) * nd, pipeline_mode=pl.Buffered(1))


def _prompt_call(x, mod, win, wpa, wpb, wo, w_s, b_s_t, alg, alb, wconv, bconv, lng, lnb, alpha):
    nb, length, d = x.shape
    tm = ROW_TILE
    assert tm == 2 * SUB_TILE and length % tm == 0 and SUB_TILE % (2 * CHUNK) == 0
    assert d % GROUPS == 0 and d // GROUPS == CHUNK
    steps_per_seq = length // tm
    n_steps = nb * steps_per_seq
    assert n_steps >= 2
    small = [alg, alb, wconv, bconv, lng, lnb]

    def tile(i):
        i = jnp.minimum(i, n_steps - 1)
        return i // steps_per_seq, i % steps_per_seq

    y, conv = pl.pallas_call(
        functools.partial(_prompt_kernel, alpha=alpha, n_steps=n_steps, steps_per_seq=steps_per_seq),
        grid=(n_steps + 1,),
        in_specs=[
            pl.BlockSpec((1, tm, d), lambda i: (*tile(i), 0)),
            _resident(mod.shape),
            _resident(win.shape), _resident(wpa.shape), _resident(wpb.shape), _resident(wo.shape),
            _resident(w_s.shape), _resident(b_s_t.shape),
        ] + [_resident(a.shape) for a in small],
        out_specs=[
            pl.BlockSpec(memory_space=pl.ANY),
            pl.BlockSpec((1, CONV_WIDTH - 1, d), lambda i: (tile(i)[0], 0, 0)),
        ],
        out_shape=[
            jax.ShapeDtypeStruct((nb * length, d), _F32),
            jax.ShapeDtypeStruct((nb, CONV_WIDTH - 1, d), _F32),
        ],
        scratch_shapes=[
            pltpu.VMEM((GROUPS, CHUNK, CHUNK), _BF16),
            pltpu.VMEM((CHUNK, d), _F32),
            pltpu.VMEM((8, d), _F32),
            pltpu.VMEM((2, tm, d), _F32),
            pltpu.VMEM((SUB_TILE, d), _BF16),
            pltpu.VMEM((SUB_TILE, d), _F32),
            pltpu.VMEM((8, d), _F32),
            pltpu.SemaphoreType.DMA((2, 2)),
        ],
        compiler_params=pltpu.CompilerParams(
            dimension_semantics=("arbitrary",),
            vmem_limit_bytes=VMEM_LIMIT_BYTES),
        name="prompt_layer",
    )(x, mod, win, wpa, wpb, wo, w_s, b_s_t, *small)
    return y.reshape(nb, length, d), conv


_PROJ_ORDER = (1, 0, 2, 4, 5, 3, 6, 7, 8)


def _sample_kernel(ws4_ref, bs4_ref, cs_ref, cp_ref, bc_ref,
                   alg_ref, alb_ref, wconv_ref, bconv_ref, lng_ref, lnb_ref,
                   x_hbm, st_hbm, wc_hbm, win_hbm, wpa_hbm, wpb_hbm, wo_hbm,
                   y_hbm, conv_hbm, v_hbm, win_out, wpa_out, wpb_out, wo_out, modp_ref,
                   win_ref, wpa_ref, wpb_ref, wo_ref, stage, xbuf, stbuf, ybuf, cvbuf, vbuf,
                   sem, exp_sem, in_sem, res_sem, *, alpha):
    steps, n, d = xbuf.shape
    gd = d // GROUPS
    hist_rows = CONV_WIDTH - 1

    in_copies = ([pltpu.make_async_copy(x_hbm.at[:, t, :], xbuf.at[t], in_sem.at[t])
                  for t in range(steps)]
                 + [pltpu.make_async_copy(st_hbm.at[:, k, :], stbuf.at[k], in_sem.at[steps + k])
                    for k in range(hist_rows)])
    for cp in in_copies:
        cp.start()

    blocks = ([(wc_hbm, None, None, j) for j in range(wc_hbm.shape[1] // d)]
              + [(win_hbm, win_ref, win_out, k) for k in _PROJ_ORDER]
              + [(wpa_hbm, wpa_ref, wpa_out, 0), (wpb_hbm, wpb_ref, wpb_out, 0),
                 (wo_hbm, wo_ref, wo_out, 0)])
    exports = []
    position = [0]

    def stage_copy(i):
        src, _, _, k = blocks[i]
        slot = i % STAGE_SLOTS
        return pltpu.make_async_copy(src.at[:, k * d:(k + 1) * d], stage.at[slot], sem.at[slot])

    def next_weight():
        i = position[0]
        position[0] += 1
        stage_copy(i).wait()
        if i + STAGE_SLOTS - 1 < len(blocks):
            stage_copy(i + STAGE_SLOTS - 1).start()
        w = stage[i % STAGE_SLOTS].astype(_BF16)
        _, keep, out, k = blocks[i]
        if keep is not None:
            cols = slice(k * d, (k + 1) * d)
            keep[:, cols] = pltpu.bitcast(w, _PACKED)
            cp = pltpu.make_async_copy(keep.at[:, cols], out.at[:, cols], exp_sem.at[len(exports)])
            cp.start()
            exports.append(cp)
        return w

    for i in range(STAGE_SLOTS - 1):
        stage_copy(i).start()

    c_all = jnp.concatenate([cs_ref[...], cp_ref[...]], axis=0).astype(_BF16)
    bc = bc_ref[...]
    mods = []
    for j in range(3):
        m = _dot(c_all, next_weight()) + bc[:, j * d:(j + 1) * d]
        modp_ref[:, j * d:(j + 1) * d] = m[n:, :]
        mods.append(m[0:n, :])
    shift, scale, gate = mods

    for cp in in_copies:
        cp.wait()
    h = jnp.concatenate([(xbuf[t] * (1.0 + scale) + shift).astype(_BF16) for t in range(steps)],
                        axis=0)

    def rows(a, t):
        return a[t * n:(t + 1) * n, :]

    def proj():
        return _dot(h, next_weight())

    results = []

    def send(buf, t, dst):
        cp = pltpu.make_async_copy(buf.at[t], dst.at[:, t, :], res_sem.at[len(results)])
        cp.start()
        results.append(cp)

    vn = _layer_norm(proj(), alg_ref[...], alb_ref[...])
    s_rows = []
    for t in range(steps):
        vbuf[t] = rows(vn, t)
        send(vbuf, t, v_hbm)
        s_cols = []
        for g in range(GROUPS):
            cols = slice(g * gd, (g + 1) * gd)
            acc = jnp.full((n, gd), bs4_ref[g, t], _F32)
            for jj in range(t + 1):
                acc = acc + ws4_ref[g, t * steps + jj] * rows(vn, jj)[:, cols]
            s_cols.append(acc)
        s_rows.append(jnp.concatenate(s_cols, axis=1))
    s = jnp.concatenate(s_rows, axis=0)
    u = proj()
    out_a = (u * s * _silu(proj())).astype(_BF16)

    z = proj()
    z = z * proj()
    hist = [stbuf[k] for k in range(hist_rows)] + [rows(z, t) for t in range(steps)]
    for k in range(hist_rows):
        cvbuf[k] = hist[steps + k]
        send(cvbuf, k, conv_hbm)
    wconv = wconv_ref[...]
    conv = jnp.concatenate(
        [bconv_ref[...] + sum(wconv[k:k + 1, :] * hist[t + k] for k in range(CONV_WIDTH))
         for t in range(steps)], axis=0)
    b_g = proj()
    out_b = (b_g * conv * _silu(proj())).astype(_BF16)

    sg_a = jax.nn.sigmoid(proj())
    sg_b = jax.nn.sigmoid(proj())
    pa = _dot(out_a, next_weight())
    pb = _dot(out_b, next_weight())
    merged = (sg_a * pa + sg_b * pb).astype(_BF16)
    y = _dot(merged, next_weight())
    for t in range(steps):
        ybuf[t] = _layer_norm(alpha * xbuf[t] + (1.0 + gate) * rows(y, t), lng_ref[...], lnb_ref[...])
        send(ybuf, t, y_hbm)

    for cp in exports + results:
        cp.wait()


def _sample_call(x, state, c_s, c_p, w_c, b_c, win, wpa, wpb, wo, ws4, bs4,
                 alg, alb, wconv, bconv, lng, lnb, alpha):
    n, steps, d = x.shape
    nb = c_p.shape[0]
    hist_rows = CONV_WIDTH - 1
    assert hist_rows <= steps <= CHUNK and w_c.shape == (d, 3 * d)
    vmem = pl.BlockSpec(memory_space=pltpu.VMEM)
    smem = pl.BlockSpec(memory_space=pltpu.SMEM)
    hbm = pl.BlockSpec(memory_space=pl.ANY)
    weights = (win, wpa, wpb, wo)
    n_exports = sum(w.shape[1] // d for w in weights)
    n_results = 2 * steps + hist_rows
    y, conv, v, *weights_bf16, mod_p = pl.pallas_call(
        functools.partial(_sample_kernel, alpha=alpha),
        in_specs=[smem, smem] + [vmem] * 9 + [hbm] * 7,
        out_specs=[hbm] * (3 + len(weights)) + [vmem],
        out_shape=[
            jax.ShapeDtypeStruct((n, steps, d), _F32),
            jax.ShapeDtypeStruct((n, hist_rows, d), _F32),
            jax.ShapeDtypeStruct((n, steps, d), _F32),
        ] + [jax.ShapeDtypeStruct(_packed_shape(w), _PACKED) for w in weights] + [
            jax.ShapeDtypeStruct((nb, 3 * d), _F32)],
        scratch_shapes=[pltpu.VMEM(_packed_shape(w), _PACKED) for w in weights] + [
            pltpu.VMEM((STAGE_SLOTS, d, d), _F32),
            pltpu.VMEM((steps, n, d), _F32), pltpu.VMEM((hist_rows, n, d), _F32),
            pltpu.VMEM((steps, n, d), _F32), pltpu.VMEM((hist_rows, n, d), _F32),
            pltpu.VMEM((steps, n, d), _F32),
            pltpu.SemaphoreType.DMA((STAGE_SLOTS,)), pltpu.SemaphoreType.DMA((n_exports,)),
            pltpu.SemaphoreType.DMA((steps + hist_rows,)), pltpu.SemaphoreType.DMA((n_results,))],
        compiler_params=pltpu.CompilerParams(vmem_limit_bytes=VMEM_LIMIT_BYTES),
        name="sample_layer",
    )(ws4, bs4, c_s, c_p, b_c.reshape(1, 3 * d), alg, alb, wconv, bconv, lng, lnb,
      x, state, w_c, win, wpa, wpb, wo)
    return y, conv, v, weights_bf16, mod_p


def kernel(x_prompt, x_sample, state_conv, c_prompt, c_sample, w_c, b_c, w_in, a_ln_g, a_ln_b,
           w_s, b_s, w_conv, b_conv, w_pa, w_pb, w_o, ln_g, ln_b):
    depth = w_in.shape[0]
    d = x_prompt.shape[-1]
    steps = x_sample.shape[1]
    alpha = (2.0 * depth) ** 0.25

    xp, xs = x_prompt, x_sample
    conv_p_rows, conv_s_rows, v_rows = [], [], []
    for l in range(depth):
        row = lambda a: a.reshape(1, d)
        small = (row(a_ln_g[l]), row(a_ln_b[l]), w_conv[l], row(b_conv[l]), row(ln_g[l]), row(ln_b[l]))
        ws4 = w_s[l][:, :steps, :steps].reshape(GROUPS, steps * steps)
        bs4 = b_s[l][:, :steps]
        xs, conv_s, v_s, weights_bf16, mod_p = _sample_call(
            xs, state_conv[l], c_sample, c_prompt, w_c[l], b_c[l],
            w_in[l], w_pa[l], w_pb[l], w_o[l], ws4, bs4, *small, alpha)
        xp, conv_p = _prompt_call(xp, mod_p, *weights_bf16, w_s[l], b_s[l].T, *small, alpha)
        conv_p_rows.append(conv_p)
        conv_s_rows.append(conv_s)
        v_rows.append(v_s)
    stack = (lambda rows: rows[0][None]) if depth == 1 else jnp.stack
    return (xp, xs, stack(conv_p_rows), stack(conv_s_rows), stack(v_rows))
```

```python
import functools

import jax
import jax.numpy as jnp
from jax import lax
from jax.experimental import pallas as pl
from jax.experimental.pallas import tpu as pltpu

CHUNK = 128
GROUPS = 8
CONV_WIDTH = 3
LN_EPS = 1e-5
ROW_TILE = 512
SUB_TILE = 256
STAGE_SLOTS = 3
VMEM_LIMIT_BYTES = 62 * 1024 * 1024

_F32 = jnp.float32
_BF16 = jnp.bfloat16


def _dot(a, b):
    return jnp.dot(a, b, preferred_element_type=_F32)


def _layer_norm(x, gain, bias):
    mu = jnp.mean(x, axis=-1, keepdims=True)
    xc = x - mu
    var = jnp.mean(xc * xc, axis=-1, keepdims=True)
    return xc * lax.rsqrt(var + LN_EPS) * gain + bias


def _silu(x):
    return x * jax.nn.sigmoid(x)


_PACKED = jnp.uint32


def _packed_shape(w):
    return (w.shape[0] // 2, w.shape[1])


def _as_bf16(words):
    return pltpu.bitcast(words, _BF16)


def _prompt_kernel(x_ref, mod_ref, win_hbm, wpa_hbm, wpb_hbm, wo_hbm, ws_ref, bst_ref,
                   alg_ref, alb_ref, wconv_ref, bconv_ref, lng_ref, lnb_ref,
                   y_hbm, conv_ref, win_ref, wpa_ref, wpb_ref, wo_ref,
                   wsm_scr, bias_scr, z_scr, ybuf, mg_scr, xb_scr, gate_scr, ysem, wsem,
                   *, alpha, n_steps, steps_per_seq):
    tm, d = x_ref.shape[1], x_ref.shape[2]
    sub = SUB_TILE
    gd = d // GROUPS
    i = pl.program_id(0)
    slot = i % 2

    def y_copy(step, part):
        row0 = pl.multiple_of(step * tm + (part - 1) * sub, sub)
        return pltpu.make_async_copy(ybuf.at[step % 2, part * sub:(part + 1) * sub, :],
                                     y_hbm.at[pl.ds(row0, sub), :], ysem.at[step % 2, part])

    @pl.when(i == 0)
    def _():
        row = lax.broadcasted_iota(jnp.int32, (CHUNK, CHUNK), 0)
        col = lax.broadcasted_iota(jnp.int32, (CHUNK, CHUNK), 1)
        for g in range(GROUPS):
            wsm_scr[g] = jnp.where(col <= row, ws_ref[g], 0.0).astype(_BF16)
            bias_scr[:, g * gd:(g + 1) * gd] = jnp.broadcast_to(bst_ref[:, g:g + 1], (CHUNK, gd))

    @pl.when(i % steps_per_seq == 0)
    def _():
        z_scr[...] = jnp.zeros(z_scr.shape, _F32)

    @pl.when(i >= 3)
    def _():
        y_copy(i - 2, 0).wait()

    @pl.when(i >= 2)
    def _():
        y_copy(i - 2, 1).wait()

    def drain_dot():
        return _dot(mg_scr[...], _as_bf16(wo_ref[...]))

    def drain_norm(y):
        ybuf[slot, 0:sub, :] = _layer_norm(alpha * xb_scr[...] + (1.0 + gate_scr[0:1, :]) * y,
                                           lng_ref[...], lnb_ref[...])

    stage_rows = sub
    stage_slots = (2 * tm) // stage_rows
    weights = ((win_hbm, win_ref), (wpa_hbm, wpa_ref), (wpb_hbm, wpb_ref), (wo_hbm, wo_ref))
    stage_blocks = [(src, dst, k, r) for src, dst in weights for k in range(src.shape[1] // d)
                    for r in range(src.shape[0] // stage_rows)]
    blocks_per_col = weights[0][0].shape[0] // stage_rows

    def stage_view(j):
        s = j % stage_slots
        return ybuf.at[s // 2, (s % 2) * stage_rows:(s % 2 + 1) * stage_rows, :]

    def stage_copy(j):
        src, _, k, r = stage_blocks[j]
        return pltpu.make_async_copy(
            src.at[r * stage_rows:(r + 1) * stage_rows, k * d:(k + 1) * d], stage_view(j),
            wsem.at[j % stage_slots])

    def step(first):
        staged = [0]

        def need(col):
            if not first:
                return
            while staged[0] < (col + 1) * blocks_per_col:
                j = staged[0]
                staged[0] += 1
                stage_copy(j).wait()
                if j + stage_slots - 1 < len(stage_blocks):
                    stage_copy(j + stage_slots - 1).start()
                _, dst, k, r = stage_blocks[j]
                half = stage_rows // 2
                dst[r * half:(r + 1) * half, k * d:(k + 1) * d] = pltpu.bitcast(
                    stage_view(j)[...].astype(_BF16), _PACKED)

        n_in = win_hbm.shape[1] // d
        if first:
            for j in range(stage_slots - 1):
                stage_copy(j).start()

        mod = mod_ref[pl.ds(i // steps_per_seq, 1), :]
        shift, scale, gate = mod[:, 0:d], mod[:, d:2 * d], mod[:, 2 * d:3 * d]
        bias = bias_scr[...]
        wconv = wconv_ref[...]
        starts = range(0, tm, sub)
        tails = []

        def modulated(r0):
            return (x_ref[0, r0:r0 + sub, :] * (1.0 + scale) + shift).astype(_BF16)

        def project_block(h, k):
            return _dot(h, _as_bf16(win_ref[:, k * d:(k + 1) * d]))

        def mix(r0, p):
            u, v, z_a, b_g, c_g, h_b, z_b, g_a, g_b = p
            vb = _layer_norm(v, alg_ref[...], alb_ref[...]).astype(_BF16)
            s_cols = []
            for g in range(GROUPS):
                cols = slice(g * gd, (g + 1) * gd)
                pair_rows = []
                for c in range(0, sub // CHUNK, 2):
                    rhs = jnp.concatenate([vb[c * CHUNK:(c + 1) * CHUNK, cols],
                                           vb[(c + 1) * CHUNK:(c + 2) * CHUNK, cols]], axis=1)
                    res = _dot(wsm_scr[g], rhs)
                    pair_rows.append(res[:, :gd] + bias[:, cols])
                    pair_rows.append(res[:, gd:] + bias[:, cols])
                s_cols.append(jnp.concatenate(pair_rows, axis=0))
            s = jnp.concatenate(s_cols, axis=1)
            out_a = (u * s * _silu(z_a)).astype(_BF16)
            z = c_g * h_b
            hist = z_scr[...] if not tails else tails[-1]
            tails.append(z[sub - 8:sub, :])
            top_rows = lax.broadcasted_iota(jnp.int32, (8, d), 0)

            def delayed(k):
                rolled = pltpu.roll(z, k, axis=0)
                top = jnp.where(top_rows < k, pltpu.roll(hist, k, axis=0), rolled[0:8, :])
                return jnp.concatenate([top, rolled[8:, :]], axis=0)

            conv = (bconv_ref[...] + wconv[0:1, :] * delayed(2) + wconv[1:2, :] * delayed(1)
                    + wconv[2:3, :] * z)
            out_b = (b_g * conv * _silu(z_b)).astype(_BF16)
            return out_a, out_b, jax.nn.sigmoid(g_a), jax.nn.sigmoid(g_b)

        def merge(m):
            out_a, out_b, sg_a, sg_b = m
            need(n_in + 1)
            return (sg_a * _dot(out_a, _as_bf16(wpa_ref[...]))
                    + sg_b * _dot(out_b, _as_bf16(wpb_ref[...]))).astype(_BF16)

        hs = [modulated(r0) for r0 in starts]
        if first:
            projected = [[] for _ in starts]
            for k in range(n_in):
                need(k)
                for h, p in zip(hs, projected):
                    p.append(project_block(h, k))
        else:
            y_prev = drain_dot()
            projected = [[project_block(h, k) for k in range(n_in)] for h in hs]
            drain_norm(y_prev)
        mixed = [mix(r0, p) for r0, p in zip(starts, projected)]
        for r0, m in zip(starts[:-1], mixed[:-1]):
            x = x_ref[0, r0:r0 + sub, :]
            mg = merge(m)
            need(n_in + 2)
            ybuf[slot, sub + r0:2 * sub + r0, :] = _layer_norm(
                alpha * x + (1.0 + gate) * _dot(mg, _as_bf16(wo_ref[...])),
                lng_ref[...], lnb_ref[...])
        mg_scr[...] = merge(mixed[-1])
        xb_scr[...] = x_ref[0, tm - sub:tm, :]
        gate_scr[...] = jnp.broadcast_to(gate, gate_scr.shape)

        z_scr[...] = tails[-1]
        conv_ref[0] = tails[-1][8 - (CONV_WIDTH - 1):8, :]

        if not first:
            y_copy(i, 0).start()
        y_copy(i, 1).start()

    @pl.when(i == 0)
    def _():
        step(first=True)

    @pl.when((i > 0) & (i < n_steps))
    def _():
        step(first=False)

    @pl.when(i == n_steps)
    def _():
        drain_norm(drain_dot())
        y_copy(i, 0).start()
        y_copy(i, 0).wait()
        y_copy(i - 1, 0).wait()
        y_copy(i - 1, 1).wait()


def _resident(shape):
    nd = len(shape)
    return pl.BlockSpec(shape, lambda i: (0,) * nd, pipeline_mode=pl.Buffered(1))


def _prompt_call(x, mod, win, wpa, wpb, wo, w_s, b_s_t, alg, alb, wconv, bconv, lng, lnb, alpha):
    nb, length, d = x.shape
    tm = ROW_TILE
    assert tm == 2 * SUB_TILE and length % tm == 0 and SUB_TILE % (2 * CHUNK) == 0
    assert d % GROUPS == 0 and d // GROUPS == CHUNK
    steps_per_seq = length // tm
    n_steps = nb * steps_per_seq
    assert n_steps >= 2
    small = [alg, alb, wconv, bconv, lng, lnb]
    weights = (win, wpa, wpb, wo)
    assert all(w.shape[0] % SUB_TILE == 0 and w.shape[1] % d == 0 for w in weights)
    hbm = pl.BlockSpec(memory_space=pl.ANY)

    def tile(i):
        i = jnp.minimum(i, n_steps - 1)
        return i // steps_per_seq, i % steps_per_seq

    y, conv = pl.pallas_call(
        functools.partial(_prompt_kernel, alpha=alpha, n_steps=n_steps, steps_per_seq=steps_per_seq),
        grid=(n_steps + 1,),
        in_specs=[
            pl.BlockSpec((1, tm, d), lambda i: (*tile(i), 0)),
            _resident(mod.shape),
            hbm, hbm, hbm, hbm,
            _resident(w_s.shape), _resident(b_s_t.shape),
        ] + [_resident(a.shape) for a in small],
        out_specs=[
            pl.BlockSpec(memory_space=pl.ANY),
            pl.BlockSpec((1, CONV_WIDTH - 1, d), lambda i: (tile(i)[0], 0, 0)),
        ],
        out_shape=[
            jax.ShapeDtypeStruct((nb * length, d), _F32),
            jax.ShapeDtypeStruct((nb, CONV_WIDTH - 1, d), _F32),
        ],
        scratch_shapes=[pltpu.VMEM(_packed_shape(w), _PACKED) for w in weights] + [
            pltpu.VMEM((GROUPS, CHUNK, CHUNK), _BF16),
            pltpu.VMEM((CHUNK, d), _F32),
            pltpu.VMEM((8, d), _F32),
            pltpu.VMEM((2, tm, d), _F32),
            pltpu.VMEM((SUB_TILE, d), _BF16),
            pltpu.VMEM((SUB_TILE, d), _F32),
            pltpu.VMEM((8, d), _F32),
            pltpu.SemaphoreType.DMA((2, 2)),
            pltpu.SemaphoreType.DMA(((2 * tm) // SUB_TILE,)),
        ],
        compiler_params=pltpu.CompilerParams(
            dimension_semantics=("arbitrary",),
            vmem_limit_bytes=VMEM_LIMIT_BYTES),
        name="prompt_layer",
    )(x, mod, win, wpa, wpb, wo, w_s, b_s_t, *small)
    return y.reshape(nb, length, d), conv


_PROJ_ORDER = (1, 0, 2, 4, 5, 3, 6, 7, 8)


def _sample_kernel(ws4_ref, bs4_ref, cs_ref, cp_ref, bc_ref,
                   alg_ref, alb_ref, wconv_ref, bconv_ref, lng_ref, lnb_ref,
                   x_hbm, st_hbm, wc_hbm, win_hbm, wpa_hbm, wpb_hbm, wo_hbm,
                   y_hbm, conv_hbm, v_hbm, modp_ref,
                   stage, xbuf, stbuf, ybuf, cvbuf, vbuf, sem, in_sem, res_sem, *, alpha):
    steps, n, d = xbuf.shape
    gd = d // GROUPS
    hist_rows = CONV_WIDTH - 1

    in_copies = ([pltpu.make_async_copy(x_hbm.at[:, t, :], xbuf.at[t], in_sem.at[t])
                  for t in range(steps)]
                 + [pltpu.make_async_copy(st_hbm.at[:, k, :], stbuf.at[k], in_sem.at[steps + k])
                    for k in range(hist_rows)])
    for cp in in_copies:
        cp.start()

    blocks = ([(wc_hbm, j) for j in range(wc_hbm.shape[1] // d)]
              + [(win_hbm, k) for k in _PROJ_ORDER] + [(wpa_hbm, 0), (wpb_hbm, 0), (wo_hbm, 0)])
    position = [0]

    def stage_copy(i):
        src, k = blocks[i]
        slot = i % STAGE_SLOTS
        return pltpu.make_async_copy(src.at[:, k * d:(k + 1) * d], stage.at[slot], sem.at[slot])

    def next_weight():
        i = position[0]
        position[0] += 1
        stage_copy(i).wait()
        if i + STAGE_SLOTS - 1 < len(blocks):
            stage_copy(i + STAGE_SLOTS - 1).start()
        return stage[i % STAGE_SLOTS].astype(_BF16)

    for i in range(STAGE_SLOTS - 1):
        stage_copy(i).start()

    c_all = jnp.concatenate([cs_ref[...], cp_ref[...]], axis=0).astype(_BF16)
    bc = bc_ref[...]
    mods = []
    for j in range(3):
        m = _dot(c_all, next_weight()) + bc[:, j * d:(j + 1) * d]
        modp_ref[:, j * d:(j + 1) * d] = m[n:, :]
        mods.append(m[0:n, :])
    shift, scale, gate = mods

    for cp in in_copies:
        cp.wait()
    h = jnp.concatenate([(xbuf[t] * (1.0 + scale) + shift).astype(_BF16) for t in range(steps)],
                        axis=0)

    def rows(a, t):
        return a[t * n:(t + 1) * n, :]

    def proj():
        return _dot(h, next_weight())

    results = []

    def send(buf, t, dst):
        cp = pltpu.make_async_copy(buf.at[t], dst.at[:, t, :], res_sem.at[len(results)])
        cp.start()
        results.append(cp)

    vn = _layer_norm(proj(), alg_ref[...], alb_ref[...])
    s_rows = []
    for t in range(steps):
        vbuf[t] = rows(vn, t)
        send(vbuf, t, v_hbm)
        s_cols = []
        for g in range(GROUPS):
            cols = slice(g * gd, (g + 1) * gd)
            acc = jnp.full((n, gd), bs4_ref[g, t], _F32)
            for jj in range(t + 1):
                acc = acc + ws4_ref[g, t * steps + jj] * rows(vn, jj)[:, cols]
            s_cols.append(acc)
        s_rows.append(jnp.concatenate(s_cols, axis=1))
    s = jnp.concatenate(s_rows, axis=0)
    u = proj()
    out_a = (u * s * _silu(proj())).astype(_BF16)

    z = proj()
    z = z * proj()
    hist = [stbuf[k] for k in range(hist_rows)] + [rows(z, t) for t in range(steps)]
    for k in range(hist_rows):
        cvbuf[k] = hist[steps + k]
        send(cvbuf, k, conv_hbm)
    wconv = wconv_ref[...]
    conv = jnp.concatenate(
        [bconv_ref[...] + sum(wconv[k:k + 1, :] * hist[t + k] for k in range(CONV_WIDTH))
         for t in range(steps)], axis=0)
    b_g = proj()
    out_b = (b_g * conv * _silu(proj())).astype(_BF16)

    sg_a = jax.nn.sigmoid(proj())
    sg_b = jax.nn.sigmoid(proj())
    pa = _dot(out_a, next_weight())
    pb = _dot(out_b, next_weight())
    merged = (sg_a * pa + sg_b * pb).astype(_BF16)
    y = _dot(merged, next_weight())
    for t in range(steps):
        ybuf[t] = _layer_norm(alpha * xbuf[t] + (1.0 + gate) * rows(y, t), lng_ref[...], lnb_ref[...])
        send(ybuf, t, y_hbm)

    for cp in results:
        cp.wait()


def _sample_call(x, state, c_s, c_p, w_c, b_c, win, wpa, wpb, wo, ws4, bs4,
                 alg, alb, wconv, bconv, lng, lnb, alpha):
    n, steps, d = x.shape
    nb = c_p.shape[0]
    hist_rows = CONV_WIDTH - 1
    assert hist_rows <= steps <= CHUNK and w_c.shape == (d, 3 * d)
    vmem = pl.BlockSpec(memory_space=pltpu.VMEM)
    smem = pl.BlockSpec(memory_space=pltpu.SMEM)
    hbm = pl.BlockSpec(memory_space=pl.ANY)
    n_results = 2 * steps + hist_rows
    y, conv, v, mod_p = pl.pallas_call(
        functools.partial(_sample_kernel, alpha=alpha),
        in_specs=[smem, smem] + [vmem] * 9 + [hbm] * 7,
        out_specs=[hbm] * 3 + [vmem],
        out_shape=[
            jax.ShapeDtypeStruct((n, steps, d), _F32),
            jax.ShapeDtypeStruct((n, hist_rows, d), _F32),
            jax.ShapeDtypeStruct((n, steps, d), _F32),
            jax.ShapeDtypeStruct((nb, 3 * d), _F32)],
        scratch_shapes=[
            pltpu.VMEM((STAGE_SLOTS, d, d), _F32),
            pltpu.VMEM((steps, n, d), _F32), pltpu.VMEM((hist_rows, n, d), _F32),
            pltpu.VMEM((steps, n, d), _F32), pltpu.VMEM((hist_rows, n, d), _F32),
            pltpu.VMEM((steps, n, d), _F32),
            pltpu.SemaphoreType.DMA((STAGE_SLOTS,)),
            pltpu.SemaphoreType.DMA((steps + hist_rows,)), pltpu.SemaphoreType.DMA((n_results,))],
        compiler_params=pltpu.CompilerParams(vmem_limit_bytes=VMEM_LIMIT_BYTES),
        name="sample_layer",
    )(ws4, bs4, c_s, c_p, b_c.reshape(1, 3 * d), alg, alb, wconv, bconv, lng, lnb,
      x, state, w_c, win, wpa, wpb, wo)
    return y, conv, v, mod_p


def kernel(x_prompt, x_sample, state_conv, c_prompt, c_sample, w_c, b_c, w_in, a_ln_g, a_ln_b,
           w_s, b_s, w_conv, b_conv, w_pa, w_pb, w_o, ln_g, ln_b):
    depth = w_in.shape[0]
    d = x_prompt.shape[-1]
    steps = x_sample.shape[1]
    alpha = (2.0 * depth) ** 0.25

    xp, xs = x_prompt, x_sample
    conv_p_rows, conv_s_rows, v_rows = [], [], []
    for l in range(depth):
        row = lambda a: a.reshape(1, d)
        small = (row(a_ln_g[l]), row(a_ln_b[l]), w_conv[l], row(b_conv[l]), row(ln_g[l]), row(ln_b[l]))
        ws4 = w_s[l][:, :steps, :steps].reshape(GROUPS, steps * steps)
        bs4 = b_s[l][:, :steps]
        weights = (w_in[l], w_pa[l], w_pb[l], w_o[l])
        xs, conv_s, v_s, mod_p = _sample_call(
            xs, state_conv[l], c_sample, c_prompt, w_c[l], b_c[l], *weights, ws4, bs4, *small, alpha)
        xp, conv_p = _prompt_call(xp, mod_p, *weights, w_s[l], b_s[l].T, *small, alpha)
        conv_p_rows.append(conv_p)
        conv_s_rows.append(conv_s)
        v_rows.append(v_s)
    stack = (lambda rows: rows[0][None]) if depth == 1 else jnp.stack
    return (xp, xs, stack(conv_p_rows), stack(conv_s_rows), stack(v_rows))
```

```python
import functools

import jax
import jax.numpy as jnp
from jax import lax
from jax.experimental import pallas as pl
from jax.experimental.pallas import tpu as pltpu

CHUNK = 128
GROUPS = 8
CONV_WIDTH = 3
LN_EPS = 1e-5
ROW_TILE = 512
SUB_TILE = 256
STAGE_SLOTS = 3
VMEM_LIMIT_BYTES = 62 * 1024 * 1024

_F32 = jnp.float32
_BF16 = jnp.bfloat16


def _dot(a, b):
    return jnp.dot(a, b, preferred_element_type=_F32)


def _layer_norm(x, gain, bias):
    mu = jnp.mean(x, axis=-1, keepdims=True)
    xc = x - mu
    var = jnp.mean(xc * xc, axis=-1, keepdims=True)
    return xc * lax.rsqrt(var + LN_EPS) * gain + bias


def _silu(x):
    return x * jax.nn.sigmoid(x)


_PACKED = jnp.uint32


def _packed_shape(w):
    return (w.shape[0] // 2, w.shape[1])


def _as_bf16(words):
    return pltpu.bitcast(words, _BF16)


def _prompt_kernel(x_ref, mod_ref, win_ref, wpa_ref, wpb_ref, wo_ref, ws_ref, bst_ref,
                   alg_ref, alb_ref, wconv_ref, bconv_ref, lng_ref, lnb_ref,
                   y_hbm, conv_ref, wsm_scr, bias_scr, z_scr, ybuf, mg_scr, xb_scr, gate_scr, ysem,
                   *, alpha, n_steps, steps_per_seq):
    tm, d = x_ref.shape[1], x_ref.shape[2]
    sub = SUB_TILE
    gd = d // GROUPS
    i = pl.program_id(0)
    slot = i % 2

    def y_copy(step, part):
        row0 = pl.multiple_of(step * tm + (part - 1) * sub, sub)
        return pltpu.make_async_copy(ybuf.at[step % 2, part * sub:(part + 1) * sub, :],
                                     y_hbm.at[pl.ds(row0, sub), :], ysem.at[step % 2, part])

    @pl.when(i == 0)
    def _():
        row = lax.broadcasted_iota(jnp.int32, (CHUNK, CHUNK), 0)
        col = lax.broadcasted_iota(jnp.int32, (CHUNK, CHUNK), 1)
        for g in range(GROUPS):
            wsm_scr[g] = jnp.where(col <= row, ws_ref[g], 0.0).astype(_BF16)
            bias_scr[:, g * gd:(g + 1) * gd] = jnp.broadcast_to(bst_ref[:, g:g + 1], (CHUNK, gd))
        mg_scr[...] = jnp.zeros(mg_scr.shape, _BF16)
        xb_scr[...] = jnp.zeros(xb_scr.shape, _F32)
        gate_scr[...] = jnp.zeros(gate_scr.shape, _F32)

    @pl.when(i % steps_per_seq == 0)
    def _():
        z_scr[...] = jnp.zeros(z_scr.shape, _F32)

    @pl.when(i >= 3)
    def _():
        y_copy(i - 2, 0).wait()

    @pl.when(i >= 2)
    def _():
        y_copy(i - 2, 1).wait()

    def drain_dot():
        return _dot(mg_scr[...], _as_bf16(wo_ref[...]))

    def drain_norm(y):
        ybuf[slot, 0:sub, :] = _layer_norm(alpha * xb_scr[...] + (1.0 + gate_scr[0:1, :]) * y,
                                           lng_ref[...], lnb_ref[...])

    @pl.when(i < n_steps)
    def _():
        mod = mod_ref[pl.ds(i // steps_per_seq, 1), :]
        shift, scale, gate = mod[:, 0:d], mod[:, d:2 * d], mod[:, 2 * d:3 * d]
        bias = bias_scr[...]
        wconv = wconv_ref[...]
        starts = range(0, tm, sub)
        tails = []

        def project():
            h = (x_ref[0] * (1.0 + scale) + shift).astype(_BF16)
            pieces = [_dot(h, _as_bf16(win_ref[:, k * d:(k + 1) * d]))
                      for k in range(win_ref.shape[1] // d)]
            return [[p[r0:r0 + sub, :] for p in pieces] for r0 in starts]

        def mix(r0, p):
            u, v, z_a, b_g, c_g, h_b, z_b, g_a, g_b = p
            vb = _layer_norm(v, alg_ref[...], alb_ref[...]).astype(_BF16)
            s_cols = []
            for g in range(GROUPS):
                cols = slice(g * gd, (g + 1) * gd)
                pair_rows = []
                for c in range(0, sub // CHUNK, 2):
                    rhs = jnp.concatenate([vb[c * CHUNK:(c + 1) * CHUNK, cols],
                                           vb[(c + 1) * CHUNK:(c + 2) * CHUNK, cols]], axis=1)
                    res = _dot(wsm_scr[g], rhs)
                    pair_rows.append(res[:, :gd] + bias[:, cols])
                    pair_rows.append(res[:, gd:] + bias[:, cols])
                s_cols.append(jnp.concatenate(pair_rows, axis=0))
            s = jnp.concatenate(s_cols, axis=1)
            out_a = (u * s * _silu(z_a)).astype(_BF16)
            z = c_g * h_b
            hist = z_scr[...] if not tails else tails[-1]
            tails.append(z[sub - 8:sub, :])
            top_rows = lax.broadcasted_iota(jnp.int32, (8, d), 0)

            def delayed(k):
                rolled = pltpu.roll(z, k, axis=0)
                top = jnp.where(top_rows < k, pltpu.roll(hist, k, axis=0), rolled[0:8, :])
                return jnp.concatenate([top, rolled[8:, :]], axis=0)

            conv = (bconv_ref[...] + wconv[0:1, :] * delayed(2) + wconv[1:2, :] * delayed(1)
                    + wconv[2:3, :] * z)
            out_b = (b_g * conv * _silu(z_b)).astype(_BF16)
            return out_a, out_b, jax.nn.sigmoid(g_a), jax.nn.sigmoid(g_b)

        def merge(m):
            out_a, out_b, sg_a, sg_b = m
            return (sg_a * _dot(out_a, _as_bf16(wpa_ref[...]))
                    + sg_b * _dot(out_b, _as_bf16(wpb_ref[...]))).astype(_BF16)

        y_prev = drain_dot()
        projected = project()
        drain_norm(y_prev)
        mixed = [mix(r0, p) for r0, p in zip(starts, projected)]
        for r0, m in zip(starts[:-1], mixed[:-1]):
            x = x_ref[0, r0:r0 + sub, :]
            ybuf[slot, sub + r0:2 * sub + r0, :] = _layer_norm(
                alpha * x + (1.0 + gate) * _dot(merge(m), _as_bf16(wo_ref[...])),
                lng_ref[...], lnb_ref[...])
        mg_scr[...] = merge(mixed[-1])
        xb_scr[...] = x_ref[0, tm - sub:tm, :]
        gate_scr[...] = jnp.broadcast_to(gate, gate_scr.shape)

        z_scr[...] = tails[-1]
        conv_ref[0] = tails[-1][8 - (CONV_WIDTH - 1):8, :]

        @pl.when(i > 0)
        def _():
            y_copy(i, 0).start()

        y_copy(i, 1).start()

    @pl.when(i == n_steps)
    def _():
        drain_norm(drain_dot())
        y_copy(i, 0).start()
        y_copy(i, 0).wait()
        y_copy(i - 1, 0).wait()
        y_copy(i - 1, 1).wait()


def _resident(shape):
    nd = len(shape)
    return pl.BlockSpec(shape, lambda i: (0,) * nd, pipeline_mode=pl.Buffered(1))


def _prompt_call(x, mod, win, wpa, wpb, wo, w_s, b_s_t, alg, alb, wconv, bconv, lng, lnb, alpha):
    nb, length, d = x.shape
    tm = ROW_TILE
    assert tm == 2 * SUB_TILE and length % tm == 0 and SUB_TILE % (2 * CHUNK) == 0
    assert d % GROUPS == 0 and d // GROUPS == CHUNK
    steps_per_seq = length // tm
    n_steps = nb * steps_per_seq
    assert n_steps >= 2
    small = [alg, alb, wconv, bconv, lng, lnb]

    def tile(i):
        i = jnp.minimum(i, n_steps - 1)
        return i // steps_per_seq, i % steps_per_seq

    y, conv = pl.pallas_call(
        functools.partial(_prompt_kernel, alpha=alpha, n_steps=n_steps, steps_per_seq=steps_per_seq),
        grid=(n_steps + 1,),
        in_specs=[
            pl.BlockSpec((1, tm, d), lambda i: (*tile(i), 0)),
            _resident(mod.shape),
            _resident(win.shape), _resident(wpa.shape), _resident(wpb.shape), _resident(wo.shape),
            _resident(w_s.shape), _resident(b_s_t.shape),
        ] + [_resident(a.shape) for a in small],
        out_specs=[
            pl.BlockSpec(memory_space=pl.ANY),
            pl.BlockSpec((1, CONV_WIDTH - 1, d), lambda i: (tile(i)[0], 0, 0)),
        ],
        out_shape=[
            jax.ShapeDtypeStruct((nb * length, d), _F32),
            jax.ShapeDtypeStruct((nb, CONV_WIDTH - 1, d), _F32),
        ],
        scratch_shapes=[
            pltpu.VMEM((GROUPS, CHUNK, CHUNK), _BF16),
            pltpu.VMEM((CHUNK, d), _F32),
            pltpu.VMEM((8, d), _F32),
            pltpu.VMEM((2, tm, d), _F32),
            pltpu.VMEM((SUB_TILE, d), _BF16),
            pltpu.VMEM((SUB_TILE, d), _F32),
            pltpu.VMEM((8, d), _F32),
            pltpu.SemaphoreType.DMA((2, 2)),
        ],
        compiler_params=pltpu.CompilerParams(
            dimension_semantics=("arbitrary",),
            vmem_limit_bytes=VMEM_LIMIT_BYTES),
        name="prompt_layer",
    )(x, mod, win, wpa, wpb, wo, w_s, b_s_t, *small)
    return y.reshape(nb, length, d), conv


_PROJ_ORDER = (1, 0, 2, 4, 5, 3, 6, 7, 8)


def _sample_kernel(ws4_ref, bs4_ref, cs_ref, cp_ref, bc_ref,
                   alg_ref, alb_ref, wconv_ref, bconv_ref, lng_ref, lnb_ref,
                   x_hbm, st_hbm, wc_hbm, win_hbm, wpa_hbm, wpb_hbm, wo_hbm,
                   y_hbm, conv_hbm, v_hbm, win_out, wpa_out, wpb_out, wo_out, modp_ref,
                   win_ref, wpa_ref, wpb_ref, wo_ref, stage, xbuf, stbuf, ybuf, cvbuf, vbuf,
                   sem, exp_sem, in_sem, res_sem, *, alpha):
    steps, n, d = xbuf.shape
    gd = d // GROUPS
    hist_rows = CONV_WIDTH - 1

    in_copies = ([pltpu.make_async_copy(x_hbm.at[:, t, :], xbuf.at[t], in_sem.at[t])
                  for t in range(steps)]
                 + [pltpu.make_async_copy(st_hbm.at[:, k, :], stbuf.at[k], in_sem.at[steps + k])
                    for k in range(hist_rows)])
    for cp in in_copies:
        cp.start()

    blocks = ([(wc_hbm, None, None, j) for j in range(wc_hbm.shape[1] // d)]
              + [(win_hbm, win_ref, win_out, k) for k in _PROJ_ORDER]
              + [(wpa_hbm, wpa_ref, wpa_out, 0), (wpb_hbm, wpb_ref, wpb_out, 0),
                 (wo_hbm, wo_ref, wo_out, 0)])
    exports = []
    position = [0]

    def stage_copy(i):
        src, _, _, k = blocks[i]
        slot = i % STAGE_SLOTS
        return pltpu.make_async_copy(src.at[:, k * d:(k + 1) * d], stage.at[slot], sem.at[slot])

    def next_weight():
        i = position[0]
        position[0] += 1
        stage_copy(i).wait()
        if i + STAGE_SLOTS - 1 < len(blocks):
            stage_copy(i + STAGE_SLOTS - 1).start()
        w = stage[i % STAGE_SLOTS].astype(_BF16)
        _, keep, out, k = blocks[i]
        if keep is not None:
            cols = slice(k * d, (k + 1) * d)
            keep[:, cols] = pltpu.bitcast(w, _PACKED)
            cp = pltpu.make_async_copy(keep.at[:, cols], out.at[:, cols], exp_sem.at[len(exports)])
            cp.start()
            exports.append(cp)
        return w

    for i in range(STAGE_SLOTS - 1):
        stage_copy(i).start()

    c_all = jnp.concatenate([cs_ref[...], cp_ref[...]], axis=0).astype(_BF16)
    bc = bc_ref[...]
    mods = []
    for j in range(3):
        m = _dot(c_all, next_weight()) + bc[:, j * d:(j + 1) * d]
        modp_ref[:, j * d:(j + 1) * d] = m[n:, :]
        mods.append(m[0:n, :])
    shift, scale, gate = mods

    for cp in in_copies:
        cp.wait()
    h = jnp.concatenate([(xbuf[t] * (1.0 + scale) + shift).astype(_BF16) for t in range(steps)],
                        axis=0)

    def rows(a, t):
        return a[t * n:(t + 1) * n, :]

    def proj():
        return _dot(h, next_weight())

    results = []

    def send(buf, t, dst):
        cp = pltpu.make_async_copy(buf.at[t], dst.at[:, t, :], res_sem.at[len(results)])
        cp.start()
        results.append(cp)

    vn = _layer_norm(proj(), alg_ref[...], alb_ref[...])
    s_rows = []
    for t in range(steps):
        vbuf[t] = rows(vn, t)
        send(vbuf, t, v_hbm)
        s_cols = []
        for g in range(GROUPS):
            cols = slice(g * gd, (g + 1) * gd)
            acc = jnp.full((n, gd), bs4_ref[g, t], _F32)
            for jj in range(t + 1):
                acc = acc + ws4_ref[g, t * steps + jj] * rows(vn, jj)[:, cols]
            s_cols.append(acc)
        s_rows.append(jnp.concatenate(s_cols, axis=1))
    s = jnp.concatenate(s_rows, axis=0)
    u = proj()
    out_a = (u * s * _silu(proj())).astype(_BF16)

    z = proj()
    z = z * proj()
    hist = [stbuf[k] for k in range(hist_rows)] + [rows(z, t) for t in range(steps)]
    for k in range(hist_rows):
        cvbuf[k] = hist[steps + k]
        send(cvbuf, k, conv_hbm)
    wconv = wconv_ref[...]
    conv = jnp.concatenate(
        [bconv_ref[...] + sum(wconv[k:k + 1, :] * hist[t + k] for k in range(CONV_WIDTH))
         for t in range(steps)], axis=0)
    b_g = proj()
    out_b = (b_g * conv * _silu(proj())).astype(_BF16)

    sg_a = jax.nn.sigmoid(proj())
    sg_b = jax.nn.sigmoid(proj())
    pa = _dot(out_a, next_weight())
    pb = _dot(out_b, next_weight())
    merged = (sg_a * pa + sg_b * pb).astype(_BF16)
    y = _dot(merged, next_weight())
    for t in range(steps):
        ybuf[t] = _layer_norm(alpha * xbuf[t] + (1.0 + gate) * rows(y, t), lng_ref[...], lnb_ref[...])
        send(ybuf, t, y_hbm)

    for cp in exports + results:
        cp.wait()


def _sample_call(x, state, c_s, c_p, w_c, b_c, win, wpa, wpb, wo, ws4, bs4,
                 alg, alb, wconv, bconv, lng, lnb, alpha):
    n, steps, d = x.shape
    nb = c_p.shape[0]
    hist_rows = CONV_WIDTH - 1
    assert hist_rows <= steps <= CHUNK and w_c.shape == (d, 3 * d)
    vmem = pl.BlockSpec(memory_space=pltpu.VMEM)
    smem = pl.BlockSpec(memory_space=pltpu.SMEM)
    hbm = pl.BlockSpec(memory_space=pl.ANY)
    weights = (win, wpa, wpb, wo)
    n_exports = sum(w.shape[1] // d for w in weights)
    n_results = 2 * steps + hist_rows
    y, conv, v, *weights_bf16, mod_p = pl.pallas_call(
        functools.partial(_sample_kernel, alpha=alpha),
        in_specs=[smem, smem] + [vmem] * 9 + [hbm] * 7,
        out_specs=[hbm] * (3 + len(weights)) + [vmem],
        out_shape=[
            jax.ShapeDtypeStruct((n, steps, d), _F32),
            jax.ShapeDtypeStruct((n, hist_rows, d), _F32),
            jax.ShapeDtypeStruct((n, steps, d), _F32),
        ] + [jax.ShapeDtypeStruct(_packed_shape(w), _PACKED) for w in weights] + [
            jax.ShapeDtypeStruct((nb, 3 * d), _F32)],
        scratch_shapes=[pltpu.VMEM(_packed_shape(w), _PACKED) for w in weights] + [
            pltpu.VMEM((STAGE_SLOTS, d, d), _F32),
            pltpu.VMEM((steps, n, d), _F32), pltpu.VMEM((hist_rows, n, d), _F32),
            pltpu.VMEM((steps, n, d), _F32), pltpu.VMEM((hist_rows, n, d), _F32),
            pltpu.VMEM((steps, n, d), _F32),
            pltpu.SemaphoreType.DMA((STAGE_SLOTS,)), pltpu.SemaphoreType.DMA((n_exports,)),
            pltpu.SemaphoreType.DMA((steps + hist_rows,)), pltpu.SemaphoreType.DMA((n_results,))],
        compiler_params=pltpu.CompilerParams(vmem_limit_bytes=VMEM_LIMIT_BYTES),
        name="sample_layer",
    )(ws4, bs4, c_s, c_p, b_c.reshape(1, 3 * d), alg, alb, wconv, bconv, lng, lnb,
      x, state, w_c, win, wpa, wpb, wo)
    return y, conv, v, weights_bf16, mod_p


def kernel(x_prompt, x_sample, state_conv, c_prompt, c_sample, w_c, b_c, w_in, a_ln_g, a_ln_b,
           w_s, b_s, w_conv, b_conv, w_pa, w_pb, w_o, ln_g, ln_b):
    depth = w_in.shape[0]
    d = x_prompt.shape[-1]
    steps = x_sample.shape[1]
    alpha = (2.0 * depth) ** 0.25

    xp, xs = x_prompt, x_sample
    conv_p_rows, conv_s_rows, v_rows = [], [], []
    for l in range(depth):
        row = lambda a: a.reshape(1, d)
        small = (row(a_ln_g[l]), row(a_ln_b[l]), w_conv[l], row(b_conv[l]), row(ln_g[l]), row(ln_b[l]))
        ws4 = w_s[l][:, :steps, :steps].reshape(GROUPS, steps * steps)
        bs4 = b_s[l][:, :steps]
        xs, conv_s, v_s, weights_bf16, mod_p = _sample_call(
            xs, state_conv[l], c_sample, c_prompt, w_c[l], b_c[l],
            w_in[l], w_pa[l], w_pb[l], w_o[l], ws4, bs4, *small, alpha)
        xp, conv_p = _prompt_call(xp, mod_p, *weights_bf16, w_s[l], b_s[l].T, *small, alpha)
        conv_p_rows.append(conv_p)
        conv_s_rows.append(conv_s)
        v_rows.append(v_s)
    stack = (lambda rows: rows[0][None]) if depth == 1 else jnp.stack
    return (xp, xs, stack(conv_p_rows), stack(conv_s_rows), stack(v_rows))
```

```python
import functools

import jax
import jax.numpy as jnp
from jax import lax
from jax.experimental import pallas as pl
from jax.experimental.pallas import tpu as pltpu

CHUNK = 128
GROUPS = 8
CONV_WIDTH = 3
LN_EPS = 1e-5
ROW_TILE = 512
SUB_TILE = 256
STAGE_SLOTS = 3
VMEM_LIMIT_BYTES = 62 * 1024 * 1024

_F32 = jnp.float32
_BF16 = jnp.bfloat16


def _dot(a, b):
    return jnp.dot(a, b, preferred_element_type=_F32)


def _layer_norm(x, gain, bias):
    mu = jnp.mean(x, axis=-1, keepdims=True)
    xc = x - mu
    var = jnp.mean(xc * xc, axis=-1, keepdims=True)
    return xc * lax.rsqrt(var + LN_EPS) * gain + bias


def _silu(x):
    return x * jax.nn.sigmoid(x)


_PACKED = jnp.uint32


def _packed_shape(w):
    return (w.shape[0] // 2, w.shape[1])


def _as_bf16(words):
    return pltpu.bitcast(words, _BF16)


def _prompt_kernel(x_ref, mod_ref, win_hbm, wpa_hbm, wpb_hbm, wo_hbm, ws_ref, bst_ref,
                   alg_ref, alb_ref, wconv_ref, bconv_ref, lng_ref, lnb_ref,
                   y_hbm, conv_ref, win_ref, wpa_ref, wpb_ref, wo_ref,
                   wsm_scr, bias_scr, z_scr, ybuf, mg_scr, xb_scr, gate_scr, ysem, wsem,
                   *, alpha, n_steps, steps_per_seq):
    tm, d = x_ref.shape[1], x_ref.shape[2]
    sub = SUB_TILE
    gd = d // GROUPS
    i = pl.program_id(0)
    slot = i % 2

    def y_copy(step, part):
        row0 = pl.multiple_of(step * tm + (part - 1) * sub, sub)
        return pltpu.make_async_copy(ybuf.at[step % 2, part * sub:(part + 1) * sub, :],
                                     y_hbm.at[pl.ds(row0, sub), :], ysem.at[step % 2, part])

    @pl.when(i == 0)
    def _():
        row = lax.broadcasted_iota(jnp.int32, (CHUNK, CHUNK), 0)
        col = lax.broadcasted_iota(jnp.int32, (CHUNK, CHUNK), 1)
        for g in range(GROUPS):
            wsm_scr[g] = jnp.where(col <= row, ws_ref[g], 0.0).astype(_BF16)
            bias_scr[:, g * gd:(g + 1) * gd] = jnp.broadcast_to(bst_ref[:, g:g + 1], (CHUNK, gd))

    @pl.when(i % steps_per_seq == 0)
    def _():
        z_scr[...] = jnp.zeros(z_scr.shape, _F32)

    @pl.when(i >= 3)
    def _():
        y_copy(i - 2, 0).wait()

    @pl.when(i >= 2)
    def _():
        y_copy(i - 2, 1).wait()

    def drain_dot():
        return _dot(mg_scr[...], _as_bf16(wo_ref[...]))

    def drain_norm(y):
        ybuf[slot, 0:sub, :] = _layer_norm(alpha * xb_scr[...] + (1.0 + gate_scr[0:1, :]) * y,
                                           lng_ref[...], lnb_ref[...])

    weights = ((win_hbm, win_ref), (wpa_hbm, wpa_ref), (wpb_hbm, wpb_ref), (wo_hbm, wo_ref))
    columns = [(src, dst, k) for src, dst in weights for k in range(src.shape[1] // d)]

    def weight_copy(c):
        src, dst, k = columns[c]
        return pltpu.make_async_copy(src.at[:, k * d:(k + 1) * d], dst.at[:, k * d:(k + 1) * d],
                                     wsem.at[c])

    def step(first):
        arrived = [0]

        def need(col):
            if not first:
                return
            while arrived[0] <= col:
                weight_copy(arrived[0]).wait()
                arrived[0] += 1

        n_in = win_hbm.shape[1] // d
        if first:
            for c in range(len(columns)):
                weight_copy(c).start()

        mod = mod_ref[pl.ds(i // steps_per_seq, 1), :]
        shift, scale, gate = mod[:, 0:d], mod[:, d:2 * d], mod[:, 2 * d:3 * d]
        bias = bias_scr[...]
        wconv = wconv_ref[...]
        starts = range(0, tm, sub)
        tails = []

        def modulated(r0):
            return (x_ref[0, r0:r0 + sub, :] * (1.0 + scale) + shift).astype(_BF16)

        def project_block(h, k):
            return _dot(h, _as_bf16(win_ref[:, k * d:(k + 1) * d]))

        def mix(r0, p):
            u, v, z_a, b_g, c_g, h_b, z_b, g_a, g_b = p
            vb = _layer_norm(v, alg_ref[...], alb_ref[...]).astype(_BF16)
            s_cols = []
            for g in range(GROUPS):
                cols = slice(g * gd, (g + 1) * gd)
                pair_rows = []
                for c in range(0, sub // CHUNK, 2):
                    rhs = jnp.concatenate([vb[c * CHUNK:(c + 1) * CHUNK, cols],
                                           vb[(c + 1) * CHUNK:(c + 2) * CHUNK, cols]], axis=1)
                    res = _dot(wsm_scr[g], rhs)
                    pair_rows.append(res[:, :gd] + bias[:, cols])
                    pair_rows.append(res[:, gd:] + bias[:, cols])
                s_cols.append(jnp.concatenate(pair_rows, axis=0))
            s = jnp.concatenate(s_cols, axis=1)
            out_a = (u * s * _silu(z_a)).astype(_BF16)
            z = c_g * h_b
            hist = z_scr[...] if not tails else tails[-1]
            tails.append(z[sub - 8:sub, :])
            top_rows = lax.broadcasted_iota(jnp.int32, (8, d), 0)

            def delayed(k):
                rolled = pltpu.roll(z, k, axis=0)
                top = jnp.where(top_rows < k, pltpu.roll(hist, k, axis=0), rolled[0:8, :])
                return jnp.concatenate([top, rolled[8:, :]], axis=0)

            conv = (bconv_ref[...] + wconv[0:1, :] * delayed(2) + wconv[1:2, :] * delayed(1)
                    + wconv[2:3, :] * z)
            out_b = (b_g * conv * _silu(z_b)).astype(_BF16)
            return out_a, out_b, jax.nn.sigmoid(g_a), jax.nn.sigmoid(g_b)

        def merge(m):
            out_a, out_b, sg_a, sg_b = m
            need(n_in + 1)
            return (sg_a * _dot(out_a, _as_bf16(wpa_ref[...]))
                    + sg_b * _dot(out_b, _as_bf16(wpb_ref[...]))).astype(_BF16)

        hs = [modulated(r0) for r0 in starts]
        if first:
            projected = [[] for _ in starts]
            for k in range(n_in):
                need(k)
                for h, p in zip(hs, projected):
                    p.append(project_block(h, k))
        else:
            y_prev = drain_dot()
            projected = [[project_block(h, k) for k in range(n_in)] for h in hs]
            drain_norm(y_prev)
        mixed = [mix(r0, p) for r0, p in zip(starts, projected)]
        for r0, m in zip(starts[:-1], mixed[:-1]):
            x = x_ref[0, r0:r0 + sub, :]
            mg = merge(m)
            need(n_in + 2)
            ybuf[slot, sub + r0:2 * sub + r0, :] = _layer_norm(
                alpha * x + (1.0 + gate) * _dot(mg, _as_bf16(wo_ref[...])),
                lng_ref[...], lnb_ref[...])
        mg_scr[...] = merge(mixed[-1])
        xb_scr[...] = x_ref[0, tm - sub:tm, :]
        gate_scr[...] = jnp.broadcast_to(gate, gate_scr.shape)

        z_scr[...] = tails[-1]
        conv_ref[0] = tails[-1][8 - (CONV_WIDTH - 1):8, :]

        if not first:
            y_copy(i, 0).start()
        y_copy(i, 1).start()

    @pl.when(i == 0)
    def _():
        step(first=True)

    @pl.when((i > 0) & (i < n_steps))
    def _():
        step(first=False)

    @pl.when(i == n_steps)
    def _():
        drain_norm(drain_dot())
        y_copy(i, 0).start()
        y_copy(i, 0).wait()
        y_copy(i - 1, 0).wait()
        y_copy(i - 1, 1).wait()


def _resident(shape):
    nd = len(shape)
    return pl.BlockSpec(shape, lambda i: (0,) * nd, pipeline_mode=pl.Buffered(1))


def _prompt_call(x, mod, win, wpa, wpb, wo, w_s, b_s_t, alg, alb, wconv, bconv, lng, lnb, alpha):
    nb, length, d = x.shape
    tm = ROW_TILE
    assert tm == 2 * SUB_TILE and length % tm == 0 and SUB_TILE % (2 * CHUNK) == 0
    assert d % GROUPS == 0 and d // GROUPS == CHUNK
    steps_per_seq = length // tm
    n_steps = nb * steps_per_seq
    assert n_steps >= 2
    small = [alg, alb, wconv, bconv, lng, lnb]
    weights = (win, wpa, wpb, wo)
    assert all(w.dtype == _PACKED and w.shape[1] % d == 0 for w in weights)
    hbm = pl.BlockSpec(memory_space=pl.ANY)

    def tile(i):
        i = jnp.minimum(i, n_steps - 1)
        return i // steps_per_seq, i % steps_per_seq

    y, conv = pl.pallas_call(
        functools.partial(_prompt_kernel, alpha=alpha, n_steps=n_steps, steps_per_seq=steps_per_seq),
        grid=(n_steps + 1,),
        in_specs=[
            pl.BlockSpec((1, tm, d), lambda i: (*tile(i), 0)),
            _resident(mod.shape),
            hbm, hbm, hbm, hbm,
            _resident(w_s.shape), _resident(b_s_t.shape),
        ] + [_resident(a.shape) for a in small],
        out_specs=[
            pl.BlockSpec(memory_space=pl.ANY),
            pl.BlockSpec((1, CONV_WIDTH - 1, d), lambda i: (tile(i)[0], 0, 0)),
        ],
        out_shape=[
            jax.ShapeDtypeStruct((nb * length, d), _F32),
            jax.ShapeDtypeStruct((nb, CONV_WIDTH - 1, d), _F32),
        ],
        scratch_shapes=[pltpu.VMEM(w.shape, _PACKED) for w in weights] + [
            pltpu.VMEM((GROUPS, CHUNK, CHUNK), _BF16),
            pltpu.VMEM((CHUNK, d), _F32),
            pltpu.VMEM((8, d), _F32),
            pltpu.VMEM((2, tm, d), _F32),
            pltpu.VMEM((SUB_TILE, d), _BF16),
            pltpu.VMEM((SUB_TILE, d), _F32),
            pltpu.VMEM((8, d), _F32),
            pltpu.SemaphoreType.DMA((2, 2)),
            pltpu.SemaphoreType.DMA((sum(w.shape[1] // d for w in weights),)),
        ],
        compiler_params=pltpu.CompilerParams(
            dimension_semantics=("arbitrary",),
            vmem_limit_bytes=VMEM_LIMIT_BYTES),
        name="prompt_layer",
    )(x, mod, win, wpa, wpb, wo, w_s, b_s_t, *small)
    return y.reshape(nb, length, d), conv


_PROJ_ORDER = (1, 0, 2, 4, 5, 3, 6, 7, 8)


def _sample_kernel(ws4_ref, bs4_ref, cs_ref, cp_ref, bc_ref,
                   alg_ref, alb_ref, wconv_ref, bconv_ref, lng_ref, lnb_ref,
                   x_hbm, st_hbm, wc_hbm, win_hbm, wpa_hbm, wpb_hbm, wo_hbm,
                   y_hbm, conv_hbm, v_hbm, win_out, wpa_out, wpb_out, wo_out, modp_ref,
                   win_ref, wpa_ref, wpb_ref, wo_ref, stage, xbuf, stbuf, ybuf, cvbuf, vbuf,
                   sem, exp_sem, in_sem, res_sem, *, alpha):
    steps, n, d = xbuf.shape
    gd = d // GROUPS
    hist_rows = CONV_WIDTH - 1

    in_copies = ([pltpu.make_async_copy(x_hbm.at[:, t, :], xbuf.at[t], in_sem.at[t])
                  for t in range(steps)]
                 + [pltpu.make_async_copy(st_hbm.at[:, k, :], stbuf.at[k], in_sem.at[steps + k])
                    for k in range(hist_rows)])
    for cp in in_copies:
        cp.start()

    blocks = ([(wc_hbm, None, None, j) for j in range(wc_hbm.shape[1] // d)]
              + [(win_hbm, win_ref, win_out, k) for k in _PROJ_ORDER]
              + [(wpa_hbm, wpa_ref, wpa_out, 0), (wpb_hbm, wpb_ref, wpb_out, 0),
                 (wo_hbm, wo_ref, wo_out, 0)])
    exports = []
    position = [0]

    def stage_copy(i):
        src, _, _, k = blocks[i]
        slot = i % STAGE_SLOTS
        return pltpu.make_async_copy(src.at[:, k * d:(k + 1) * d], stage.at[slot], sem.at[slot])

    def next_weight():
        i = position[0]
        position[0] += 1
        stage_copy(i).wait()
        if i + STAGE_SLOTS - 1 < len(blocks):
            stage_copy(i + STAGE_SLOTS - 1).start()
        w = stage[i % STAGE_SLOTS].astype(_BF16)
        _, keep, out, k = blocks[i]
        if keep is not None:
            cols = slice(k * d, (k + 1) * d)
            keep[:, cols] = pltpu.bitcast(w, _PACKED)
            cp = pltpu.make_async_copy(keep.at[:, cols], out.at[:, cols], exp_sem.at[len(exports)])
            cp.start()
            exports.append(cp)
        return w

    for i in range(STAGE_SLOTS - 1):
        stage_copy(i).start()

    c_all = jnp.concatenate([cs_ref[...], cp_ref[...]], axis=0).astype(_BF16)
    bc = bc_ref[...]
    mods = []
    for j in range(3):
        m = _dot(c_all, next_weight()) + bc[:, j * d:(j + 1) * d]
        modp_ref[:, j * d:(j + 1) * d] = m[n:, :]
        mods.append(m[0:n, :])
    shift, scale, gate = mods

    for cp in in_copies:
        cp.wait()
    h = jnp.concatenate([(xbuf[t] * (1.0 + scale) + shift).astype(_BF16) for t in range(steps)],
                        axis=0)

    def rows(a, t):
        return a[t * n:(t + 1) * n, :]

    def proj():
        return _dot(h, next_weight())

    results = []

    def send(buf, t, dst):
        cp = pltpu.make_async_copy(buf.at[t], dst.at[:, t, :], res_sem.at[len(results)])
        cp.start()
        results.append(cp)

    vn = _layer_norm(proj(), alg_ref[...], alb_ref[...])
    s_rows = []
    for t in range(steps):
        vbuf[t] = rows(vn, t)
        send(vbuf, t, v_hbm)
        s_cols = []
        for g in range(GROUPS):
            cols = slice(g * gd, (g + 1) * gd)
            acc = jnp.full((n, gd), bs4_ref[g, t], _F32)
            for jj in range(t + 1):
                acc = acc + ws4_ref[g, t * steps + jj] * rows(vn, jj)[:, cols]
            s_cols.append(acc)
        s_rows.append(jnp.concatenate(s_cols, axis=1))
    s = jnp.concatenate(s_rows, axis=0)
    u = proj()
    out_a = (u * s * _silu(proj())).astype(_BF16)

    z = proj()
    z = z * proj()
    hist = [stbuf[k] for k in range(hist_rows)] + [rows(z, t) for t in range(steps)]
    for k in range(hist_rows):
        cvbuf[k] = hist[steps + k]
        send(cvbuf, k, conv_hbm)
    wconv = wconv_ref[...]
    conv = jnp.concatenate(
        [bconv_ref[...] + sum(wconv[k:k + 1, :] * hist[t + k] for k in range(CONV_WIDTH))
         for t in range(steps)], axis=0)
    b_g = proj()
    out_b = (b_g * conv * _silu(proj())).astype(_BF16)

    sg_a = jax.nn.sigmoid(proj())
    sg_b = jax.nn.sigmoid(proj())
    pa = _dot(out_a, next_weight())
    pb = _dot(out_b, next_weight())
    merged = (sg_a * pa + sg_b * pb).astype(_BF16)
    y = _dot(merged, next_weight())
    for t in range(steps):
        ybuf[t] = _layer_norm(alpha * xbuf[t] + (1.0 + gate) * rows(y, t), lng_ref[...], lnb_ref[...])
        send(ybuf, t, y_hbm)

    for cp in exports + results:
        cp.wait()


def _sample_call(x, state, c_s, c_p, w_c, b_c, win, wpa, wpb, wo, ws4, bs4,
                 alg, alb, wconv, bconv, lng, lnb, alpha):
    n, steps, d = x.shape
    nb = c_p.shape[0]
    hist_rows = CONV_WIDTH - 1
    assert hist_rows <= steps <= CHUNK and w_c.shape == (d, 3 * d)
    vmem = pl.BlockSpec(memory_space=pltpu.VMEM)
    smem = pl.BlockSpec(memory_space=pltpu.SMEM)
    hbm = pl.BlockSpec(memory_space=pl.ANY)
    weights = (win, wpa, wpb, wo)
    n_exports = sum(w.shape[1] // d for w in weights)
    n_results = 2 * steps + hist_rows
    y, conv, v, *weights_bf16, mod_p = pl.pallas_call(
        functools.partial(_sample_kernel, alpha=alpha),
        in_specs=[smem, smem] + [vmem] * 9 + [hbm] * 7,
        out_specs=[hbm] * (3 + len(weights)) + [vmem],
        out_shape=[
            jax.ShapeDtypeStruct((n, steps, d), _F32),
            jax.ShapeDtypeStruct((n, hist_rows, d), _F32),
            jax.ShapeDtypeStruct((n, steps, d), _F32),
        ] + [jax.ShapeDtypeStruct(_packed_shape(w), _PACKED) for w in weights] + [
            jax.ShapeDtypeStruct((nb, 3 * d), _F32)],
        scratch_shapes=[pltpu.VMEM(_packed_shape(w), _PACKED) for w in weights] + [
            pltpu.VMEM((STAGE_SLOTS, d, d), _F32),
            pltpu.VMEM((steps, n, d), _F32), pltpu.VMEM((hist_rows, n, d), _F32),
            pltpu.VMEM((steps, n, d), _F32), pltpu.VMEM((hist_rows, n, d), _F32),
            pltpu.VMEM((steps, n, d), _F32),
            pltpu.SemaphoreType.DMA((STAGE_SLOTS,)), pltpu.SemaphoreType.DMA((n_exports,)),
            pltpu.SemaphoreType.DMA((steps + hist_rows,)), pltpu.SemaphoreType.DMA((n_results,))],
        compiler_params=pltpu.CompilerParams(vmem_limit_bytes=VMEM_LIMIT_BYTES),
        name="sample_layer",
    )(ws4, bs4, c_s, c_p, b_c.reshape(1, 3 * d), alg, alb, wconv, bconv, lng, lnb,
      x, state, w_c, win, wpa, wpb, wo)
    return y, conv, v, weights_bf16, mod_p


def kernel(x_prompt, x_sample, state_conv, c_prompt, c_sample, w_c, b_c, w_in, a_ln_g, a_ln_b,
           w_s, b_s, w_conv, b_conv, w_pa, w_pb, w_o, ln_g, ln_b):
    depth = w_in.shape[0]
    d = x_prompt.shape[-1]
    steps = x_sample.shape[1]
    alpha = (2.0 * depth) ** 0.25

    xp, xs = x_prompt, x_sample
    conv_p_rows, conv_s_rows, v_rows = [], [], []
    for l in range(depth):
        row = lambda a: a.reshape(1, d)
        small = (row(a_ln_g[l]), row(a_ln_b[l]), w_conv[l], row(b_conv[l]), row(ln_g[l]), row(ln_b[l]))
        ws4 = w_s[l][:, :steps, :steps].reshape(GROUPS, steps * steps)
        bs4 = b_s[l][:, :steps]
        xs, conv_s, v_s, weights_bf16, mod_p = _sample_call(
            xs, state_conv[l], c_sample, c_prompt, w_c[l], b_c[l],
            w_in[l], w_pa[l], w_pb[l], w_o[l], ws4, bs4, *small, alpha)
        xp, conv_p = _prompt_call(xp, mod_p, *weights_bf16, w_s[l], b_s[l].T, *small, alpha)
        conv_p_rows.append(conv_p)
        conv_s_rows.append(conv_s)
        v_rows.append(v_s)
    stack = (lambda rows: rows[0][None]) if depth == 1 else jnp.stack
    return (xp, xs, stack(conv_p_rows), stack(conv_s_rows), stack(v_rows))
```

```python
import functools

import jax
import jax.numpy as jnp
from jax import lax
from jax.experimental import pallas as pl
from jax.experimental.pallas import tpu as pltpu

CHUNK = 128
GROUPS = 8
CONV_WIDTH = 3
LN_EPS = 1e-5
ROW_TILE = 512
SUB_TILE = 256
STAGE_SLOTS = 3
VMEM_LIMIT_BYTES = 62 * 1024 * 1024

_F32 = jnp.float32
_BF16 = jnp.bfloat16


def _dot(a, b):
    return jnp.dot(a, b, preferred_element_type=_F32)


def _layer_norm(x, gain, bias):
    mu = jnp.mean(x, axis=-1, keepdims=True)
    xc = x - mu
    var = jnp.mean(xc * xc, axis=-1, keepdims=True)
    return xc * lax.rsqrt(var + LN_EPS) * gain + bias


def _silu(x):
    return x * jax.nn.sigmoid(x)


_PACKED = jnp.uint32


def _packed_shape(w):
    return (w.shape[0] // 2, w.shape[1])


def _as_bf16(words):
    return pltpu.bitcast(words, _BF16)


def _prompt_kernel(x_ref, mod_ref, win_hbm, wpa_hbm, wpb_hbm, wo_hbm, ws_ref, bst_ref,
                   alg_ref, alb_ref, wconv_ref, bconv_ref, lng_ref, lnb_ref,
                   y_hbm, conv_ref, win_ref, wpa_ref, wpb_ref, wo_ref,
                   wsm_scr, bias_scr, z_scr, ybuf, mg_scr, xb_scr, gate_scr, ysem, wsem,
                   *, alpha, n_steps, steps_per_seq):
    tm, d = x_ref.shape[1], x_ref.shape[2]
    sub = SUB_TILE
    gd = d // GROUPS
    i = pl.program_id(0)
    slot = i % 2

    def y_copy(step, part):
        row0 = pl.multiple_of(step * tm + (part - 1) * sub, sub)
        return pltpu.make_async_copy(ybuf.at[step % 2, part * sub:(part + 1) * sub, :],
                                     y_hbm.at[pl.ds(row0, sub), :], ysem.at[step % 2, part])

    weights = ((win_hbm, win_ref), (wpa_hbm, wpa_ref), (wpb_hbm, wpb_ref), (wo_hbm, wo_ref))
    columns = [(src, dst, k) for src, dst in weights for k in range(src.shape[1] // d)]

    def weight_copy(c):
        src, dst, k = columns[c]
        return pltpu.make_async_copy(src.at[:, k * d:(k + 1) * d], dst.at[:, k * d:(k + 1) * d],
                                     wsem.at[c])

    @pl.when(i == 0)
    def _():
        for c in range(len(columns)):
            weight_copy(c).start()
        row = lax.broadcasted_iota(jnp.int32, (CHUNK, CHUNK), 0)
        col = lax.broadcasted_iota(jnp.int32, (CHUNK, CHUNK), 1)
        for g in range(GROUPS):
            wsm_scr[g] = jnp.where(col <= row, ws_ref[g], 0.0).astype(_BF16)
            bias_scr[:, g * gd:(g + 1) * gd] = jnp.broadcast_to(bst_ref[:, g:g + 1], (CHUNK, gd))
        mg_scr[...] = jnp.zeros(mg_scr.shape, _BF16)
        xb_scr[...] = jnp.zeros(xb_scr.shape, _F32)
        gate_scr[...] = jnp.zeros(gate_scr.shape, _F32)
        for c in range(len(columns)):
            weight_copy(c).wait()

    @pl.when(i % steps_per_seq == 0)
    def _():
        z_scr[...] = jnp.zeros(z_scr.shape, _F32)

    @pl.when(i >= 3)
    def _():
        y_copy(i - 2, 0).wait()

    @pl.when(i >= 2)
    def _():
        y_copy(i - 2, 1).wait()

    def drain_dot():
        return _dot(mg_scr[...], _as_bf16(wo_ref[...]))

    def drain_norm(y):
        ybuf[slot, 0:sub, :] = _layer_norm(alpha * xb_scr[...] + (1.0 + gate_scr[0:1, :]) * y,
                                           lng_ref[...], lnb_ref[...])

    def step(first):
        arrived = [0]

        def need(col):
            if not first:
                return
            while arrived[0] <= col:
                weight_copy(arrived[0]).wait()
                arrived[0] += 1

        n_in = win_hbm.shape[1] // d
        mod = mod_ref[pl.ds(i // steps_per_seq, 1), :]
        shift, scale, gate = mod[:, 0:d], mod[:, d:2 * d], mod[:, 2 * d:3 * d]
        bias = bias_scr[...]
        wconv = wconv_ref[...]
        starts = range(0, tm, sub)
        tails = []

        def modulated(r0):
            return (x_ref[0, r0:r0 + sub, :] * (1.0 + scale) + shift).astype(_BF16)

        def project_block(h, k):
            return _dot(h, _as_bf16(win_ref[:, k * d:(k + 1) * d]))

        def mix(r0, p):
            u, v, z_a, b_g, c_g, h_b, z_b, g_a, g_b = p
            vb = _layer_norm(v, alg_ref[...], alb_ref[...]).astype(_BF16)
            s_cols = []
            for g in range(GROUPS):
                cols = slice(g * gd, (g + 1) * gd)
                pair_rows = []
                for c in range(0, sub // CHUNK, 2):
                    rhs = jnp.concatenate([vb[c * CHUNK:(c + 1) * CHUNK, cols],
                                           vb[(c + 1) * CHUNK:(c + 2) * CHUNK, cols]], axis=1)
                    res = _dot(wsm_scr[g], rhs)
                    pair_rows.append(res[:, :gd] + bias[:, cols])
                    pair_rows.append(res[:, gd:] + bias[:, cols])
                s_cols.append(jnp.concatenate(pair_rows, axis=0))
            s = jnp.concatenate(s_cols, axis=1)
            out_a = (u * s * _silu(z_a)).astype(_BF16)
            z = c_g * h_b
            hist = z_scr[...] if not tails else tails[-1]
            tails.append(z[sub - 8:sub, :])
            top_rows = lax.broadcasted_iota(jnp.int32, (8, d), 0)

            def delayed(k):
                rolled = pltpu.roll(z, k, axis=0)
                top = jnp.where(top_rows < k, pltpu.roll(hist, k, axis=0), rolled[0:8, :])
                return jnp.concatenate([top, rolled[8:, :]], axis=0)

            conv = (bconv_ref[...] + wconv[0:1, :] * delayed(2) + wconv[1:2, :] * delayed(1)
                    + wconv[2:3, :] * z)
            out_b = (b_g * conv * _silu(z_b)).astype(_BF16)
            return out_a, out_b, jax.nn.sigmoid(g_a), jax.nn.sigmoid(g_b)

        def merge(m):
            out_a, out_b, sg_a, sg_b = m
            need(n_in + 1)
            return (sg_a * _dot(out_a, _as_bf16(wpa_ref[...]))
                    + sg_b * _dot(out_b, _as_bf16(wpb_ref[...]))).astype(_BF16)

        hs = [modulated(r0) for r0 in starts]
        if first:
            projected = [[] for _ in starts]
            for k in range(n_in):
                need(k)
                for h, p in zip(hs, projected):
                    p.append(project_block(h, k))
        else:
            y_prev = drain_dot()
            projected = [[project_block(h, k) for k in range(n_in)] for h in hs]
            drain_norm(y_prev)
        mixed = [mix(r0, p) for r0, p in zip(starts, projected)]
        for r0, m in zip(starts[:-1], mixed[:-1]):
            x = x_ref[0, r0:r0 + sub, :]
            mg = merge(m)
            need(n_in + 2)
            ybuf[slot, sub + r0:2 * sub + r0, :] = _layer_norm(
                alpha * x + (1.0 + gate) * _dot(mg, _as_bf16(wo_ref[...])),
                lng_ref[...], lnb_ref[...])
        mg_scr[...] = merge(mixed[-1])
        xb_scr[...] = x_ref[0, tm - sub:tm, :]
        gate_scr[...] = jnp.broadcast_to(gate, gate_scr.shape)

        z_scr[...] = tails[-1]
        conv_ref[0] = tails[-1][8 - (CONV_WIDTH - 1):8, :]

        @pl.when(i > 0)
        def _():
            y_copy(i, 0).start()

        y_copy(i, 1).start()

    @pl.when(i < n_steps)
    def _():
        step(first=False)

    @pl.when(i == n_steps)
    def _():
        drain_norm(drain_dot())
        y_copy(i, 0).start()
        y_copy(i, 0).wait()
        y_copy(i - 1, 0).wait()
        y_copy(i - 1, 1).wait()


def _resident(shape):
    nd = len(shape)
    return pl.BlockSpec(shape, lambda i: (0,) * nd, pipeline_mode=pl.Buffered(1))


def _prompt_call(x, mod, win, wpa, wpb, wo, w_s, b_s_t, alg, alb, wconv, bconv, lng, lnb, alpha):
    nb, length, d = x.shape
    tm = ROW_TILE
    assert tm == 2 * SUB_TILE and length % tm == 0 and SUB_TILE % (2 * CHUNK) == 0
    assert d % GROUPS == 0 and d // GROUPS == CHUNK
    steps_per_seq = length // tm
    n_steps = nb * steps_per_seq
    assert n_steps >= 2
    small = [alg, alb, wconv, bconv, lng, lnb]
    weights = (win, wpa, wpb, wo)
    assert all(w.dtype == _PACKED and w.shape[1] % d == 0 for w in weights)
    hbm = pl.BlockSpec(memory_space=pl.ANY)

    def tile(i):
        i = jnp.minimum(i, n_steps - 1)
        return i // steps_per_seq, i % steps_per_seq

    y, conv = pl.pallas_call(
        functools.partial(_prompt_kernel, alpha=alpha, n_steps=n_steps, steps_per_seq=steps_per_seq),
        grid=(n_steps + 1,),
        in_specs=[
            pl.BlockSpec((1, tm, d), lambda i: (*tile(i), 0)),
            _resident(mod.shape),
            hbm, hbm, hbm, hbm,
            _resident(w_s.shape), _resident(b_s_t.shape),
        ] + [_resident(a.shape) for a in small],
        out_specs=[
            pl.BlockSpec(memory_space=pl.ANY),
            pl.BlockSpec((1, CONV_WIDTH - 1, d), lambda i: (tile(i)[0], 0, 0)),
        ],
        out_shape=[
            jax.ShapeDtypeStruct((nb * length, d), _F32),
            jax.ShapeDtypeStruct((nb, CONV_WIDTH - 1, d), _F32),
        ],
        scratch_shapes=[pltpu.VMEM(w.shape, _PACKED) for w in weights] + [
            pltpu.VMEM((GROUPS, CHUNK, CHUNK), _BF16),
            pltpu.VMEM((CHUNK, d), _F32),
            pltpu.VMEM((8, d), _F32),
            pltpu.VMEM((2, tm, d), _F32),
            pltpu.VMEM((SUB_TILE, d), _BF16),
            pltpu.VMEM((SUB_TILE, d), _F32),
            pltpu.VMEM((8, d), _F32),
            pltpu.SemaphoreType.DMA((2, 2)),
            pltpu.SemaphoreType.DMA((sum(w.shape[1] // d for w in weights),)),
        ],
        compiler_params=pltpu.CompilerParams(
            dimension_semantics=("arbitrary",),
            vmem_limit_bytes=VMEM_LIMIT_BYTES),
        name="prompt_layer",
    )(x, mod, win, wpa, wpb, wo, w_s, b_s_t, *small)
    return y.reshape(nb, length, d), conv


_PROJ_ORDER = (1, 0, 2, 4, 5, 3, 6, 7, 8)


def _sample_kernel(ws4_ref, bs4_ref, cs_ref, cp_ref, bc_ref,
                   alg_ref, alb_ref, wconv_ref, bconv_ref, lng_ref, lnb_ref,
                   x_hbm, st_hbm, wc_hbm, win_hbm, wpa_hbm, wpb_hbm, wo_hbm,
                   y_hbm, conv_hbm, v_hbm, win_out, wpa_out, wpb_out, wo_out, modp_ref,
                   win_ref, wpa_ref, wpb_ref, wo_ref, stage, xbuf, stbuf, ybuf, cvbuf, vbuf,
                   sem, exp_sem, in_sem, res_sem, *, alpha):
    steps, n, d = xbuf.shape
    gd = d // GROUPS
    hist_rows = CONV_WIDTH - 1

    in_copies = ([pltpu.make_async_copy(x_hbm.at[:, t, :], xbuf.at[t], in_sem.at[t])
                  for t in range(steps)]
                 + [pltpu.make_async_copy(st_hbm.at[:, k, :], stbuf.at[k], in_sem.at[steps + k])
                    for k in range(hist_rows)])
    for cp in in_copies:
        cp.start()

    blocks = ([(wc_hbm, None, None, j) for j in range(wc_hbm.shape[1] // d)]
              + [(win_hbm, win_ref, win_out, k) for k in _PROJ_ORDER]
              + [(wpa_hbm, wpa_ref, wpa_out, 0), (wpb_hbm, wpb_ref, wpb_out, 0),
                 (wo_hbm, wo_ref, wo_out, 0)])
    exports = []
    position = [0]

    def stage_copy(i):
        src, _, _, k = blocks[i]
        slot = i % STAGE_SLOTS
        return pltpu.make_async_copy(src.at[:, k * d:(k + 1) * d], stage.at[slot], sem.at[slot])

    def next_weight():
        i = position[0]
        position[0] += 1
        stage_copy(i).wait()
        if i + STAGE_SLOTS - 1 < len(blocks):
            stage_copy(i + STAGE_SLOTS - 1).start()
        w = stage[i % STAGE_SLOTS].astype(_BF16)
        _, keep, out, k = blocks[i]
        if keep is not None:
            cols = slice(k * d, (k + 1) * d)
            keep[:, cols] = pltpu.bitcast(w, _PACKED)
            cp = pltpu.make_async_copy(keep.at[:, cols], out.at[:, cols], exp_sem.at[len(exports)])
            cp.start()
            exports.append(cp)
        return w

    for i in range(STAGE_SLOTS - 1):
        stage_copy(i).start()

    c_all = jnp.concatenate([cs_ref[...], cp_ref[...]], axis=0).astype(_BF16)
    bc = bc_ref[...]
    mods = []
    for j in range(3):
        m = _dot(c_all, next_weight()) + bc[:, j * d:(j + 1) * d]
        modp_ref[:, j * d:(j + 1) * d] = m[n:, :]
        mods.append(m[0:n, :])
    shift, scale, gate = mods

    for cp in in_copies:
        cp.wait()
    h = jnp.concatenate([(xbuf[t] * (1.0 + scale) + shift).astype(_BF16) for t in range(steps)],
                        axis=0)

    def rows(a, t):
        return a[t * n:(t + 1) * n, :]

    def proj():
        return _dot(h, next_weight())

    results = []

    def send(buf, t, dst):
        cp = pltpu.make_async_copy(buf.at[t], dst.at[:, t, :], res_sem.at[len(results)])
        cp.start()
        results.append(cp)

    vn = _layer_norm(proj(), alg_ref[...], alb_ref[...])
    s_rows = []
    for t in range(steps):
        vbuf[t] = rows(vn, t)
        send(vbuf, t, v_hbm)
        s_cols = []
        for g in range(GROUPS):
            cols = slice(g * gd, (g + 1) * gd)
            acc = jnp.full((n, gd), bs4_ref[g, t], _F32)
            for jj in range(t + 1):
                acc = acc + ws4_ref[g, t * steps + jj] * rows(vn, jj)[:, cols]
            s_cols.append(acc)
        s_rows.append(jnp.concatenate(s_cols, axis=1))
    s = jnp.concatenate(s_rows, axis=0)
    u = proj()
    out_a = (u * s * _silu(proj())).astype(_BF16)

    z = proj()
    z = z * proj()
    hist = [stbuf[k] for k in range(hist_rows)] + [rows(z, t) for t in range(steps)]
    for k in range(hist_rows):
        cvbuf[k] = hist[steps + k]
        send(cvbuf, k, conv_hbm)
    wconv = wconv_ref[...]
    conv = jnp.concatenate(
        [bconv_ref[...] + sum(wconv[k:k + 1, :] * hist[t + k] for k in range(CONV_WIDTH))
         for t in range(steps)], axis=0)
    b_g = proj()
    out_b = (b_g * conv * _silu(proj())).astype(_BF16)

    sg_a = jax.nn.sigmoid(proj())
    sg_b = jax.nn.sigmoid(proj())
    pa = _dot(out_a, next_weight())
    pb = _dot(out_b, next_weight())
    merged = (sg_a * pa + sg_b * pb).astype(_BF16)
    y = _dot(merged, next_weight())
    for t in range(steps):
        ybuf[t] = _layer_norm(alpha * xbuf[t] + (1.0 + gate) * rows(y, t), lng_ref[...], lnb_ref[...])
        send(ybuf, t, y_hbm)

    for cp in exports + results:
        cp.wait()


def _sample_call(x, state, c_s, c_p, w_c, b_c, win, wpa, wpb, wo, ws4, bs4,
                 alg, alb, wconv, bconv, lng, lnb, alpha):
    n, steps, d = x.shape
    nb = c_p.shape[0]
    hist_rows = CONV_WIDTH - 1
    assert hist_rows <= steps <= CHUNK and w_c.shape == (d, 3 * d)
    vmem = pl.BlockSpec(memory_space=pltpu.VMEM)
    smem = pl.BlockSpec(memory_space=pltpu.SMEM)
    hbm = pl.BlockSpec(memory_space=pl.ANY)
    weights = (win, wpa, wpb, wo)
    n_exports = sum(w.shape[1] // d for w in weights)
    n_results = 2 * steps + hist_rows
    y, conv, v, *weights_bf16, mod_p = pl.pallas_call(
        functools.partial(_sample_kernel, alpha=alpha),
        in_specs=[smem, smem] + [vmem] * 9 + [hbm] * 7,
        out_specs=[hbm] * (3 + len(weights)) + [vmem],
        out_shape=[
            jax.ShapeDtypeStruct((n, steps, d), _F32),
            jax.ShapeDtypeStruct((n, hist_rows, d), _F32),
            jax.ShapeDtypeStruct((n, steps, d), _F32),
        ] + [jax.ShapeDtypeStruct(_packed_shape(w), _PACKED) for w in weights] + [
            jax.ShapeDtypeStruct((nb, 3 * d), _F32)],
        scratch_shapes=[pltpu.VMEM(_packed_shape(w), _PACKED) for w in weights] + [
            pltpu.VMEM((STAGE_SLOTS, d, d), _F32),
            pltpu.VMEM((steps, n, d), _F32), pltpu.VMEM((hist_rows, n, d), _F32),
            pltpu.VMEM((steps, n, d), _F32), pltpu.VMEM((hist_rows, n, d), _F32),
            pltpu.VMEM((steps, n, d), _F32),
            pltpu.SemaphoreType.DMA((STAGE_SLOTS,)), pltpu.SemaphoreType.DMA((n_exports,)),
            pltpu.SemaphoreType.DMA((steps + hist_rows,)), pltpu.SemaphoreType.DMA((n_results,))],
        compiler_params=pltpu.CompilerParams(vmem_limit_bytes=VMEM_LIMIT_BYTES),
        name="sample_layer",
    )(ws4, bs4, c_s, c_p, b_c.reshape(1, 3 * d), alg, alb, wconv, bconv, lng, lnb,
      x, state, w_c, win, wpa, wpb, wo)
    return y, conv, v, weights_bf16, mod_p


def kernel(x_prompt, x_sample, state_conv, c_prompt, c_sample, w_c, b_c, w_in, a_ln_g, a_ln_b,
           w_s, b_s, w_conv, b_conv, w_pa, w_pb, w_o, ln_g, ln_b):
    depth = w_in.shape[0]
    d = x_prompt.shape[-1]
    steps = x_sample.shape[1]
    alpha = (2.0 * depth) ** 0.25

    xp, xs = x_prompt, x_sample
    conv_p_rows, conv_s_rows, v_rows = [], [], []
    for l in range(depth):
        row = lambda a: a.reshape(1, d)
        small = (row(a_ln_g[l]), row(a_ln_b[l]), w_conv[l], row(b_conv[l]), row(ln_g[l]), row(ln_b[l]))
        ws4 = w_s[l][:, :steps, :steps].reshape(GROUPS, steps * steps)
        bs4 = b_s[l][:, :steps]
        xs, conv_s, v_s, weights_bf16, mod_p = _sample_call(
            xs, state_conv[l], c_sample, c_prompt, w_c[l], b_c[l],
            w_in[l], w_pa[l], w_pb[l], w_o[l], ws4, bs4, *small, alpha)
        xp, conv_p = _prompt_call(xp, mod_p, *weights_bf16, w_s[l], b_s[l].T, *small, alpha)
        conv_p_rows.append(conv_p)
        conv_s_rows.append(conv_s)
        v_rows.append(v_s)
    stack = (lambda rows: rows[0][None]) if depth == 1 else jnp.stack
    return (xp, xs, stack(conv_p_rows), stack(conv_s_rows), stack(v_rows))
```

```python
import functools

import jax
import jax.numpy as jnp
from jax import lax
from jax.experimental import pallas as pl
from jax.experimental.pallas import tpu as pltpu

CHUNK = 128
GROUPS = 8
CONV_WIDTH = 3
LN_EPS = 1e-5
ROW_TILE = 512
SUB_TILE = 256
STAGE_SLOTS = 3
VMEM_LIMIT_BYTES = 62 * 1024 * 1024

_F32 = jnp.float32
_BF16 = jnp.bfloat16


def _dot(a, b):
    return jnp.dot(a, b, preferred_element_type=_F32)


def _layer_norm(x, gain, bias):
    mu = jnp.mean(x, axis=-1, keepdims=True)
    xc = x - mu
    var = jnp.mean(xc * xc, axis=-1, keepdims=True)
    return xc * lax.rsqrt(var + LN_EPS) * gain + bias


def _silu(x):
    return x * jax.nn.sigmoid(x)


_PACKED = jnp.uint32


def _packed_shape(w):
    return (w.shape[0] // 2, w.shape[1])


def _as_bf16(words):
    return pltpu.bitcast(words, _BF16)


def _prompt_kernel(x_ref, mod_ref, win_ref, wpa_ref, wpb_ref, wo_ref, ws_ref, bst_ref,
                   alg_ref, alb_ref, wconv_ref, bconv_ref, lng_ref, lnb_ref,
                   y_ref, conv_ref, wsm_scr, bias_scr, z_scr, *, alpha, steps_per_seq):
    tm, d = x_ref.shape
    sub = SUB_TILE
    gd = d // GROUPS
    i = pl.program_id(0)

    @pl.when(i == 0)
    def _():
        row = lax.broadcasted_iota(jnp.int32, (CHUNK, CHUNK), 0)
        col = lax.broadcasted_iota(jnp.int32, (CHUNK, CHUNK), 1)
        for g in range(GROUPS):
            wsm_scr[g] = jnp.where(col <= row, ws_ref[g], 0.0).astype(_BF16)
            bias_scr[:, g * gd:(g + 1) * gd] = jnp.broadcast_to(bst_ref[:, g:g + 1], (CHUNK, gd))

    @pl.when(i % steps_per_seq == 0)
    def _():
        z_scr[...] = jnp.zeros(z_scr.shape, _F32)

    mod = mod_ref[pl.ds(i // steps_per_seq, 1), :]
    shift, scale, gate = mod[:, 0:d], mod[:, d:2 * d], mod[:, 2 * d:3 * d]
    bias = bias_scr[...]
    wconv = wconv_ref[...]
    starts = range(0, tm, sub)
    tails = []

    def project(r0):
        h = (x_ref[r0:r0 + sub, :] * (1.0 + scale) + shift).astype(_BF16)
        return [_dot(h, _as_bf16(win_ref[:, k * d:(k + 1) * d]))
                for k in range(win_ref.shape[1] // d)]

    def mix(r0, p):
        u, v, z_a, b_g, c_g, h_b, z_b, g_a, g_b = p
        vb = _layer_norm(v, alg_ref[...], alb_ref[...]).astype(_BF16)
        s_cols = []
        for g in range(GROUPS):
            cols = slice(g * gd, (g + 1) * gd)
            pair_rows = []
            for c in range(0, sub // CHUNK, 2):
                rhs = jnp.concatenate([vb[c * CHUNK:(c + 1) * CHUNK, cols],
                                       vb[(c + 1) * CHUNK:(c + 2) * CHUNK, cols]], axis=1)
                res = _dot(wsm_scr[g], rhs)
                pair_rows.append(res[:, :gd] + bias[:, cols])
                pair_rows.append(res[:, gd:] + bias[:, cols])
            s_cols.append(jnp.concatenate(pair_rows, axis=0))
        s = jnp.concatenate(s_cols, axis=1)
        out_a = (u * s * _silu(z_a)).astype(_BF16)
        z = c_g * h_b
        hist = z_scr[...] if not tails else tails[-1]
        tails.append(z[sub - 8:sub, :])
        top_rows = lax.broadcasted_iota(jnp.int32, (8, d), 0)

        def delayed(k):
            rolled = pltpu.roll(z, k, axis=0)
            top = jnp.where(top_rows < k, pltpu.roll(hist, k, axis=0), rolled[0:8, :])
            return jnp.concatenate([top, rolled[8:, :]], axis=0)

        conv = (bconv_ref[...] + wconv[0:1, :] * delayed(2) + wconv[1:2, :] * delayed(1)
                + wconv[2:3, :] * z)
        out_b = (b_g * conv * _silu(z_b)).astype(_BF16)
        return out_a, out_b, jax.nn.sigmoid(g_a), jax.nn.sigmoid(g_b)

    def merge(m):
        out_a, out_b, sg_a, sg_b = m
        return (sg_a * _dot(out_a, _as_bf16(wpa_ref[...]))
                + sg_b * _dot(out_b, _as_bf16(wpb_ref[...]))).astype(_BF16)

    projected = [project(r0) for r0 in starts]
    mixed = [mix(r0, p) for r0, p in zip(starts, projected)]
    merged = [merge(m) for m in mixed]
    for r0, mg in zip(starts, merged):
        x = x_ref[r0:r0 + sub, :]
        y_ref[r0:r0 + sub, :] = _layer_norm(
            alpha * x + (1.0 + gate) * _dot(mg, _as_bf16(wo_ref[...])), lng_ref[...], lnb_ref[...])

    z_scr[...] = tails[-1]
    conv_ref[0] = tails[-1][8 - (CONV_WIDTH - 1):8, :]


def _resident(shape):
    nd = len(shape)
    return pl.BlockSpec(shape, lambda i: (0,) * nd, pipeline_mode=pl.Buffered(1))


def _prompt_call(x, mod, win, wpa, wpb, wo, w_s, b_s_t, alg, alb, wconv, bconv, lng, lnb, alpha):
    nb, length, d = x.shape
    tm = ROW_TILE
    assert length % tm == 0 and tm % SUB_TILE == 0 and SUB_TILE % (2 * CHUNK) == 0
    assert d % GROUPS == 0 and d // GROUPS == CHUNK
    steps_per_seq = length // tm
    small = [alg, alb, wconv, bconv, lng, lnb]
    y, conv = pl.pallas_call(
        functools.partial(_prompt_kernel, alpha=alpha, steps_per_seq=steps_per_seq),
        grid=(nb * steps_per_seq,),
        in_specs=[
            pl.BlockSpec((tm, d), lambda i: (i, 0)),
            _resident(mod.shape),
            _resident(win.shape), _resident(wpa.shape), _resident(wpb.shape), _resident(wo.shape),
            _resident(w_s.shape), _resident(b_s_t.shape),
        ] + [_resident(a.shape) for a in small],
        out_specs=[
            pl.BlockSpec((tm, d), lambda i: (i, 0)),
            pl.BlockSpec((1, CONV_WIDTH - 1, d), lambda i: (i // steps_per_seq, 0, 0)),
        ],
        out_shape=[
            jax.ShapeDtypeStruct((nb * length, d), _F32),
            jax.ShapeDtypeStruct((nb, CONV_WIDTH - 1, d), _F32),
        ],
        scratch_shapes=[
            pltpu.VMEM((GROUPS, CHUNK, CHUNK), _BF16),
            pltpu.VMEM((CHUNK, d), _F32),
            pltpu.VMEM((8, d), _F32),
        ],
        compiler_params=pltpu.CompilerParams(
            dimension_semantics=("arbitrary",),
            vmem_limit_bytes=VMEM_LIMIT_BYTES),
        name="prompt_layer",
    )(x.reshape(nb * length, d), mod, win, wpa, wpb, wo, w_s, b_s_t, *small)
    return y.reshape(nb, length, d), conv


_PROJ_ORDER = (1, 0, 2, 4, 5, 3, 6, 7, 8)


def _sample_kernel(ws4_ref, bs4_ref, cs_ref, cp_ref, bc_ref,
                   alg_ref, alb_ref, wconv_ref, bconv_ref, lng_ref, lnb_ref,
                   x_hbm, st_hbm, wc_hbm, win_hbm, wpa_hbm, wpb_hbm, wo_hbm,
                   y_hbm, conv_hbm, v_hbm, win_out, wpa_out, wpb_out, wo_out, modp_ref,
                   win_ref, wpa_ref, wpb_ref, wo_ref, stage, xbuf, stbuf, ybuf, cvbuf, vbuf,
                   sem, exp_sem, in_sem, res_sem, *, alpha):
    steps, n, d = xbuf.shape
    gd = d // GROUPS
    hist_rows = CONV_WIDTH - 1

    in_copies = ([pltpu.make_async_copy(x_hbm.at[:, t, :], xbuf.at[t], in_sem.at[t])
                  for t in range(steps)]
                 + [pltpu.make_async_copy(st_hbm.at[:, k, :], stbuf.at[k], in_sem.at[steps + k])
                    for k in range(hist_rows)])
    for cp in in_copies:
        cp.start()

    blocks = ([(wc_hbm, None, None, j) for j in range(wc_hbm.shape[1] // d)]
              + [(win_hbm, win_ref, win_out, k) for k in _PROJ_ORDER]
              + [(wpa_hbm, wpa_ref, wpa_out, 0), (wpb_hbm, wpb_ref, wpb_out, 0),
                 (wo_hbm, wo_ref, wo_out, 0)])
    exports = []
    position = [0]

    def stage_copy(i):
        src, _, _, k = blocks[i]
        slot = i % STAGE_SLOTS
        return pltpu.make_async_copy(src.at[:, k * d:(k + 1) * d], stage.at[slot], sem.at[slot])

    def next_weight():
        i = position[0]
        position[0] += 1
        stage_copy(i).wait()
        if i + STAGE_SLOTS - 1 < len(blocks):
            stage_copy(i + STAGE_SLOTS - 1).start()
        w = stage[i % STAGE_SLOTS].astype(_BF16)
        _, keep, out, k = blocks[i]
        if keep is not None:
            cols = slice(k * d, (k + 1) * d)
            keep[:, cols] = pltpu.bitcast(w, _PACKED)
            cp = pltpu.make_async_copy(keep.at[:, cols], out.at[:, cols], exp_sem.at[len(exports)])
            cp.start()
            exports.append(cp)
        return w

    for i in range(STAGE_SLOTS - 1):
        stage_copy(i).start()

    c_all = jnp.concatenate([cs_ref[...], cp_ref[...]], axis=0).astype(_BF16)
    bc = bc_ref[...]
    mods = []
    for j in range(3):
        m = _dot(c_all, next_weight()) + bc[:, j * d:(j + 1) * d]
        modp_ref[:, j * d:(j + 1) * d] = m[n:, :]
        mods.append(m[0:n, :])
    shift, scale, gate = mods

    for cp in in_copies:
        cp.wait()
    h = jnp.concatenate([(xbuf[t] * (1.0 + scale) + shift).astype(_BF16) for t in range(steps)],
                        axis=0)

    def rows(a, t):
        return a[t * n:(t + 1) * n, :]

    def proj():
        return _dot(h, next_weight())

    results = []

    def send(buf, t, dst):
        cp = pltpu.make_async_copy(buf.at[t], dst.at[:, t, :], res_sem.at[len(results)])
        cp.start()
        results.append(cp)

    vn = _layer_norm(proj(), alg_ref[...], alb_ref[...])
    s_rows = []
    for t in range(steps):
        vbuf[t] = rows(vn, t)
        send(vbuf, t, v_hbm)
        s_cols = []
        for g in range(GROUPS):
            cols = slice(g * gd, (g + 1) * gd)
            acc = jnp.full((n, gd), bs4_ref[g, t], _F32)
            for jj in range(t + 1):
                acc = acc + ws4_ref[g, t * steps + jj] * rows(vn, jj)[:, cols]
            s_cols.append(acc)
        s_rows.append(jnp.concatenate(s_cols, axis=1))
    s = jnp.concatenate(s_rows, axis=0)
    u = proj()
    out_a = (u * s * _silu(proj())).astype(_BF16)

    z = proj()
    z = z * proj()
    hist = [stbuf[k] for k in range(hist_rows)] + [rows(z, t) for t in range(steps)]
    for k in range(hist_rows):
        cvbuf[k] = hist[steps + k]
        send(cvbuf, k, conv_hbm)
    wconv = wconv_ref[...]
    conv = jnp.concatenate(
        [bconv_ref[...] + sum(wconv[k:k + 1, :] * hist[t + k] for k in range(CONV_WIDTH))
         for t in range(steps)], axis=0)
    b_g = proj()
    out_b = (b_g * conv * _silu(proj())).astype(_BF16)

    sg_a = jax.nn.sigmoid(proj())
    sg_b = jax.nn.sigmoid(proj())
    pa = _dot(out_a, next_weight())
    pb = _dot(out_b, next_weight())
    merged = (sg_a * pa + sg_b * pb).astype(_BF16)
    y = _dot(merged, next_weight())
    for t in range(steps):
        ybuf[t] = _layer_norm(alpha * xbuf[t] + (1.0 + gate) * rows(y, t), lng_ref[...], lnb_ref[...])
        send(ybuf, t, y_hbm)

    for cp in exports + results:
        cp.wait()


def _sample_call(x, state, c_s, c_p, w_c, b_c, win, wpa, wpb, wo, ws4, bs4,
                 alg, alb, wconv, bconv, lng, lnb, alpha):
    n, steps, d = x.shape
    nb = c_p.shape[0]
    hist_rows = CONV_WIDTH - 1
    assert hist_rows <= steps <= CHUNK and w_c.shape == (d, 3 * d)
    vmem = pl.BlockSpec(memory_space=pltpu.VMEM)
    smem = pl.BlockSpec(memory_space=pltpu.SMEM)
    hbm = pl.BlockSpec(memory_space=pl.ANY)
    weights = (win, wpa, wpb, wo)
    n_exports = sum(w.shape[1] // d for w in weights)
    n_results = 2 * steps + hist_rows
    y, conv, v, *weights_bf16, mod_p = pl.pallas_call(
        functools.partial(_sample_kernel, alpha=alpha),
        in_specs=[smem, smem] + [vmem] * 9 + [hbm] * 7,
        out_specs=[hbm] * (3 + len(weights)) + [vmem],
        out_shape=[
            jax.ShapeDtypeStruct((n, steps, d), _F32),
            jax.ShapeDtypeStruct((n, hist_rows, d), _F32),
            jax.ShapeDtypeStruct((n, steps, d), _F32),
        ] + [jax.ShapeDtypeStruct(_packed_shape(w), _PACKED) for w in weights] + [
            jax.ShapeDtypeStruct((nb, 3 * d), _F32)],
        scratch_shapes=[pltpu.VMEM(_packed_shape(w), _PACKED) for w in weights] + [
            pltpu.VMEM((STAGE_SLOTS, d, d), _F32),
            pltpu.VMEM((steps, n, d), _F32), pltpu.VMEM((hist_rows, n, d), _F32),
            pltpu.VMEM((steps, n, d), _F32), pltpu.VMEM((hist_rows, n, d), _F32),
            pltpu.VMEM((steps, n, d), _F32),
            pltpu.SemaphoreType.DMA((STAGE_SLOTS,)), pltpu.SemaphoreType.DMA((n_exports,)),
            pltpu.SemaphoreType.DMA((steps + hist_rows,)), pltpu.SemaphoreType.DMA((n_results,))],
        compiler_params=pltpu.CompilerParams(vmem_limit_bytes=VMEM_LIMIT_BYTES),
        name="sample_layer",
    )(ws4, bs4, c_s, c_p, b_c.reshape(1, 3 * d), alg, alb, wconv, bconv, lng, lnb,
      x, state, w_c, win, wpa, wpb, wo)
    return y, conv, v, weights_bf16, mod_p


def kernel(x_prompt, x_sample, state_conv, c_prompt, c_sample, w_c, b_c, w_in, a_ln_g, a_ln_b,
           w_s, b_s, w_conv, b_conv, w_pa, w_pb, w_o, ln_g, ln_b):
    depth = w_in.shape[0]
    d = x_prompt.shape[-1]
    steps = x_sample.shape[1]
    alpha = (2.0 * depth) ** 0.25

    xp, xs = x_prompt, x_sample
    conv_p_rows, conv_s_rows, v_rows = [], [], []
    for l in range(depth):
        row = lambda a: a.reshape(1, d)
        small = (row(a_ln_g[l]), row(a_ln_b[l]), w_conv[l], row(b_conv[l]), row(ln_g[l]), row(ln_b[l]))
        ws4 = w_s[l][:, :steps, :steps].reshape(GROUPS, steps * steps)
        bs4 = b_s[l][:, :steps]
        xs, conv_s, v_s, weights_bf16, mod_p = _sample_call(
            xs, state_conv[l], c_sample, c_prompt, w_c[l], b_c[l],
            w_in[l], w_pa[l], w_pb[l], w_o[l], ws4, bs4, *small, alpha)
        xp, conv_p = _prompt_call(xp, mod_p, *weights_bf16, w_s[l], b_s[l].T, *small, alpha)
        conv_p_rows.append(conv_p)
        conv_s_rows.append(conv_s)
        v_rows.append(v_s)
    stack = (lambda rows: rows[0][None]) if depth == 1 else jnp.stack
    return (xp, xs, stack(conv_p_rows), stack(conv_s_rows), stack(v_rows))
```

```python
import functools

import jax
import jax.numpy as jnp
from jax import lax
from jax.experimental import pallas as pl
from jax.experimental.pallas import tpu as pltpu

CHUNK = 128
GROUPS = 8
CONV_WIDTH = 3
LN_EPS = 1e-5
ROW_TILE = 512
SUB_TILE = 256
STAGE_SLOTS = 3
VMEM_LIMIT_BYTES = 62 * 1024 * 1024

_F32 = jnp.float32
_BF16 = jnp.bfloat16


def _dot(a, b):
    return jnp.dot(a, b, preferred_element_type=_F32)


def _layer_norm(x, gain, bias):
    mu = jnp.mean(x, axis=-1, keepdims=True)
    xc = x - mu
    var = jnp.mean(xc * xc, axis=-1, keepdims=True)
    return xc * lax.rsqrt(var + LN_EPS) * gain + bias


def _silu(x):
    return x * jax.nn.sigmoid(x)


_PACKED = jnp.uint32


def _packed_shape(w):
    return (w.shape[0] // 2, w.shape[1])


def _as_bf16(words):
    return pltpu.bitcast(words, _BF16)


def _prompt_kernel(x_ref, mod_ref, win_ref, wpa_ref, wpb_ref, wo_ref, ws_ref, bst_ref,
                   alg_ref, alb_ref, wconv_ref, bconv_ref, lng_ref, lnb_ref,
                   y_ref, conv_ref, wsm_scr, bias_scr, z_scr, *, alpha, steps_per_seq):
    tm, d = x_ref.shape
    sub = SUB_TILE
    gd = d // GROUPS
    i = pl.program_id(0)

    @pl.when(i == 0)
    def _():
        row = lax.broadcasted_iota(jnp.int32, (CHUNK, CHUNK), 0)
        col = lax.broadcasted_iota(jnp.int32, (CHUNK, CHUNK), 1)
        for g in range(GROUPS):
            wsm_scr[g] = jnp.where(col <= row, ws_ref[g], 0.0).astype(_BF16)
            bias_scr[:, g * gd:(g + 1) * gd] = jnp.broadcast_to(bst_ref[:, g:g + 1], (CHUNK, gd))

    @pl.when(i % steps_per_seq == 0)
    def _():
        z_scr[...] = jnp.zeros(z_scr.shape, _F32)

    mod = mod_ref[pl.ds(i // steps_per_seq, 1), :]
    shift, scale, gate = mod[:, 0:d], mod[:, d:2 * d], mod[:, 2 * d:3 * d]
    bias = bias_scr[...]
    wconv = wconv_ref[...]
    starts = range(0, tm, sub)
    tails = []

    def project(r0):
        h = (x_ref[r0:r0 + sub, :] * (1.0 + scale) + shift).astype(_BF16)
        return [_dot(h, _as_bf16(win_ref[:, k * d:(k + 1) * d]))
                for k in range(win_ref.shape[1] // d)]

    def mix(r0, p):
        u, v, z_a, b_g, c_g, h_b, z_b, g_a, g_b = p
        vb = _layer_norm(v, alg_ref[...], alb_ref[...]).astype(_BF16)
        s_cols = []
        for g in range(GROUPS):
            cols = slice(g * gd, (g + 1) * gd)
            pair_rows = []
            for c in range(0, sub // CHUNK, 2):
                rhs = jnp.concatenate([vb[c * CHUNK:(c + 1) * CHUNK, cols],
                                       vb[(c + 1) * CHUNK:(c + 2) * CHUNK, cols]], axis=1)
                res = _dot(wsm_scr[g], rhs)
                pair_rows.append(res[:, :gd] + bias[:, cols])
                pair_rows.append(res[:, gd:] + bias[:, cols])
            s_cols.append(jnp.concatenate(pair_rows, axis=0))
        s = jnp.concatenate(s_cols, axis=1)
        out_a = (u * s * _silu(z_a)).astype(_BF16)
        z = c_g * h_b
        hist = z_scr[...] if not tails else tails[-1]
        tails.append(z[sub - 8:sub, :])
        top_rows = lax.broadcasted_iota(jnp.int32, (8, d), 0)

        def delayed(k):
            rolled = pltpu.roll(z, k, axis=0)
            top = jnp.where(top_rows < k, pltpu.roll(hist, k, axis=0), rolled[0:8, :])
            return jnp.concatenate([top, rolled[8:, :]], axis=0)

        conv = (bconv_ref[...] + wconv[0:1, :] * delayed(2) + wconv[1:2, :] * delayed(1)
                + wconv[2:3, :] * z)
        out_b = (b_g * conv * _silu(z_b)).astype(_BF16)
        return out_a, out_b, jax.nn.sigmoid(g_a), jax.nn.sigmoid(g_b)

    def merge(m):
        out_a, out_b, sg_a, sg_b = m
        return (sg_a * _dot(out_a, _as_bf16(wpa_ref[...]))
                + sg_b * _dot(out_b, _as_bf16(wpb_ref[...]))).astype(_BF16)

    projected = [project(r0) for r0 in starts]
    mixed = [mix(r0, p) for r0, p in zip(starts, projected)]
    merged = [merge(m) for m in mixed]
    for r0, mg in zip(starts, merged):
        x = x_ref[r0:r0 + sub, :]
        y_ref[r0:r0 + sub, :] = _layer_norm(
            alpha * x + (1.0 + gate) * _dot(mg, _as_bf16(wo_ref[...])), lng_ref[...], lnb_ref[...])

    z_scr[...] = tails[-1]
    conv_ref[0] = tails[-1][8 - (CONV_WIDTH - 1):8, :]


def _resident(shape):
    nd = len(shape)
    return pl.BlockSpec(shape, lambda i: (0,) * nd, pipeline_mode=pl.Buffered(1))


def _prompt_call(x, mod, win, wpa, wpb, wo, w_s, b_s_t, alg, alb, wconv, bconv, lng, lnb, alpha):
    nb, length, d = x.shape
    tm = ROW_TILE
    assert length % tm == 0 and tm % SUB_TILE == 0 and SUB_TILE % (2 * CHUNK) == 0
    assert d % GROUPS == 0 and d // GROUPS == CHUNK
    steps_per_seq = length // tm
    small = [alg, alb, wconv, bconv, lng, lnb]
    y, conv = pl.pallas_call(
        functools.partial(_prompt_kernel, alpha=alpha, steps_per_seq=steps_per_seq),
        grid=(nb * steps_per_seq,),
        in_specs=[
            pl.BlockSpec((tm, d), lambda i: (i, 0)),
            _resident(mod.shape),
            _resident(win.shape), _resident(wpa.shape), _resident(wpb.shape), _resident(wo.shape),
            _resident(w_s.shape), _resident(b_s_t.shape),
        ] + [_resident(a.shape) for a in small],
        out_specs=[
            pl.BlockSpec((tm, d), lambda i: (i, 0)),
            pl.BlockSpec((1, CONV_WIDTH - 1, d), lambda i: (i // steps_per_seq, 0, 0)),
        ],
        out_shape=[
            jax.ShapeDtypeStruct((nb * length, d), _F32),
            jax.ShapeDtypeStruct((nb, CONV_WIDTH - 1, d), _F32),
        ],
        scratch_shapes=[
            pltpu.VMEM((GROUPS, CHUNK, CHUNK), _BF16),
            pltpu.VMEM((CHUNK, d), _F32),
            pltpu.VMEM((8, d), _F32),
        ],
        compiler_params=pltpu.CompilerParams(
            dimension_semantics=("arbitrary",),
            vmem_limit_bytes=VMEM_LIMIT_BYTES),
        name="prompt_layer",
    )(x.reshape(nb * length, d), mod, win, wpa, wpb, wo, w_s, b_s_t, *small)
    return y.reshape(nb, length, d), conv


_PROJ_ORDER = (1, 0, 2, 4, 5, 3, 6, 7, 8)


def _sample_kernel(ws4_ref, bs4_ref, cs_ref, cp_ref, bc_ref,
                   alg_ref, alb_ref, wconv_ref, bconv_ref, lng_ref, lnb_ref,
                   x_hbm, st_hbm, wc_hbm, win_hbm, wpa_hbm, wpb_hbm, wo_hbm,
                   y_hbm, conv_hbm, v_hbm, win_out, wpa_out, wpb_out, wo_out, modp_ref,
                   win_ref, wpa_ref, wpb_ref, wo_ref, stage, xbuf, stbuf, ybuf, cvbuf, vbuf,
                   sem, exp_sem, in_sem, res_sem, *, alpha):
    steps, n, d = xbuf.shape
    gd = d // GROUPS
    hist_rows = CONV_WIDTH - 1

    in_copies = ([pltpu.make_async_copy(x_hbm.at[:, t, :], xbuf.at[t], in_sem.at[t])
                  for t in range(steps)]
                 + [pltpu.make_async_copy(st_hbm.at[:, k, :], stbuf.at[k], in_sem.at[steps + k])
                    for k in range(hist_rows)])
    for cp in in_copies:
        cp.start()

    blocks = ([(wc_hbm, None, None, j) for j in range(wc_hbm.shape[1] // d)]
              + [(win_hbm, win_ref, win_out, k) for k in _PROJ_ORDER]
              + [(wpa_hbm, wpa_ref, wpa_out, 0), (wpb_hbm, wpb_ref, wpb_out, 0),
                 (wo_hbm, wo_ref, wo_out, 0)])
    exports = []
    position = [0]

    def stage_copy(i):
        src, _, _, k = blocks[i]
        slot = i % STAGE_SLOTS
        return pltpu.make_async_copy(src.at[:, k * d:(k + 1) * d], stage.at[slot], sem.at[slot])

    def load_weight(i):
        stage_copy(i).wait()
        if i + STAGE_SLOTS - 1 < len(blocks):
            stage_copy(i + STAGE_SLOTS - 1).start()
        w = stage[i % STAGE_SLOTS].astype(_BF16)
        _, keep, out, k = blocks[i]
        if keep is not None:
            cols = slice(k * d, (k + 1) * d)
            keep[:, cols] = pltpu.bitcast(w, _PACKED)
            cp = pltpu.make_async_copy(keep.at[:, cols], out.at[:, cols], exp_sem.at[len(exports)])
            cp.start()
            exports.append(cp)
        return w

    def next_weight():
        i = position[0]
        position[0] += 1
        w = ahead.pop()
        if i + 1 < len(blocks):
            ahead.append(load_weight(i + 1))
        return w

    for i in range(STAGE_SLOTS - 1):
        stage_copy(i).start()
    ahead = [load_weight(0)]

    c_all = jnp.concatenate([cs_ref[...], cp_ref[...]], axis=0).astype(_BF16)
    bc = bc_ref[...]
    mods = []
    for j in range(3):
        m = _dot(c_all, next_weight()) + bc[:, j * d:(j + 1) * d]
        modp_ref[:, j * d:(j + 1) * d] = m[n:, :]
        mods.append(m[0:n, :])
    shift, scale, gate = mods

    for cp in in_copies:
        cp.wait()
    h = jnp.concatenate([(xbuf[t] * (1.0 + scale) + shift).astype(_BF16) for t in range(steps)],
                        axis=0)

    def rows(a, t):
        return a[t * n:(t + 1) * n, :]

    def proj():
        return _dot(h, next_weight())

    results = []

    def send(buf, t, dst):
        cp = pltpu.make_async_copy(buf.at[t], dst.at[:, t, :], res_sem.at[len(results)])
        cp.start()
        results.append(cp)

    vn = _layer_norm(proj(), alg_ref[...], alb_ref[...])
    s_rows = []
    for t in range(steps):
        vbuf[t] = rows(vn, t)
        send(vbuf, t, v_hbm)
        s_cols = []
        for g in range(GROUPS):
            cols = slice(g * gd, (g + 1) * gd)
            acc = jnp.full((n, gd), bs4_ref[g, t], _F32)
            for jj in range(t + 1):
                acc = acc + ws4_ref[g, t * steps + jj] * rows(vn, jj)[:, cols]
            s_cols.append(acc)
        s_rows.append(jnp.concatenate(s_cols, axis=1))
    s = jnp.concatenate(s_rows, axis=0)
    u = proj()
    out_a = (u * s * _silu(proj())).astype(_BF16)

    z = proj()
    z = z * proj()
    hist = [stbuf[k] for k in range(hist_rows)] + [rows(z, t) for t in range(steps)]
    for k in range(hist_rows):
        cvbuf[k] = hist[steps + k]
        send(cvbuf, k, conv_hbm)
    wconv = wconv_ref[...]
    conv = jnp.concatenate(
        [bconv_ref[...] + sum(wconv[k:k + 1, :] * hist[t + k] for k in range(CONV_WIDTH))
         for t in range(steps)], axis=0)
    b_g = proj()
    out_b = (b_g * conv * _silu(proj())).astype(_BF16)

    sg_a = jax.nn.sigmoid(proj())
    sg_b = jax.nn.sigmoid(proj())
    pa = _dot(out_a, next_weight())
    pb = _dot(out_b, next_weight())
    merged = (sg_a * pa + sg_b * pb).astype(_BF16)
    y = _dot(merged, next_weight())
    for t in range(steps):
        ybuf[t] = _layer_norm(alpha * xbuf[t] + (1.0 + gate) * rows(y, t), lng_ref[...], lnb_ref[...])
        send(ybuf, t, y_hbm)

    for cp in exports + results:
        cp.wait()


def _sample_call(x, state, c_s, c_p, w_c, b_c, win, wpa, wpb, wo, ws4, bs4,
                 alg, alb, wconv, bconv, lng, lnb, alpha):
    n, steps, d = x.shape
    nb = c_p.shape[0]
    hist_rows = CONV_WIDTH - 1
    assert hist_rows <= steps <= CHUNK and w_c.shape == (d, 3 * d)
    vmem = pl.BlockSpec(memory_space=pltpu.VMEM)
    smem = pl.BlockSpec(memory_space=pltpu.SMEM)
    hbm = pl.BlockSpec(memory_space=pl.ANY)
    weights = (win, wpa, wpb, wo)
    n_exports = sum(w.shape[1] // d for w in weights)
    n_results = 2 * steps + hist_rows
    y, conv, v, *weights_bf16, mod_p = pl.pallas_call(
        functools.partial(_sample_kernel, alpha=alpha),
        in_specs=[smem, smem] + [vmem] * 9 + [hbm] * 7,
        out_specs=[hbm] * (3 + len(weights)) + [vmem],
        out_shape=[
            jax.ShapeDtypeStruct((n, steps, d), _F32),
            jax.ShapeDtypeStruct((n, hist_rows, d), _F32),
            jax.ShapeDtypeStruct((n, steps, d), _F32),
        ] + [jax.ShapeDtypeStruct(_packed_shape(w), _PACKED) for w in weights] + [
            jax.ShapeDtypeStruct((nb, 3 * d), _F32)],
        scratch_shapes=[pltpu.VMEM(_packed_shape(w), _PACKED) for w in weights] + [
            pltpu.VMEM((STAGE_SLOTS, d, d), _F32),
            pltpu.VMEM((steps, n, d), _F32), pltpu.VMEM((hist_rows, n, d), _F32),
            pltpu.VMEM((steps, n, d), _F32), pltpu.VMEM((hist_rows, n, d), _F32),
            pltpu.VMEM((steps, n, d), _F32),
            pltpu.SemaphoreType.DMA((STAGE_SLOTS,)), pltpu.SemaphoreType.DMA((n_exports,)),
            pltpu.SemaphoreType.DMA((steps + hist_rows,)), pltpu.SemaphoreType.DMA((n_results,))],
        compiler_params=pltpu.CompilerParams(vmem_limit_bytes=VMEM_LIMIT_BYTES),
        name="sample_layer",
    )(ws4, bs4, c_s, c_p, b_c.reshape(1, 3 * d), alg, alb, wconv, bconv, lng, lnb,
      x, state, w_c, win, wpa, wpb, wo)
    return y, conv, v, weights_bf16, mod_p


def kernel(x_prompt, x_sample, state_conv, c_prompt, c_sample, w_c, b_c, w_in, a_ln_g, a_ln_b,
           w_s, b_s, w_conv, b_conv, w_pa, w_pb, w_o, ln_g, ln_b):
    depth = w_in.shape[0]
    d = x_prompt.shape[-1]
    steps = x_sample.shape[1]
    alpha = (2.0 * depth) ** 0.25

    xp, xs = x_prompt, x_sample
    conv_p_rows, conv_s_rows, v_rows = [], [], []
    for l in range(depth):
        row = lambda a: a.reshape(1, d)
        small = (row(a_ln_g[l]), row(a_ln_b[l]), w_conv[l], row(b_conv[l]), row(ln_g[l]), row(ln_b[l]))
        ws4 = w_s[l][:, :steps, :steps].reshape(GROUPS, steps * steps)
        bs4 = b_s[l][:, :steps]
        xs, conv_s, v_s, weights_bf16, mod_p = _sample_call(
            xs, state_conv[l], c_sample, c_prompt, w_c[l], b_c[l],
            w_in[l], w_pa[l], w_pb[l], w_o[l], ws4, bs4, *small, alpha)
        xp, conv_p = _prompt_call(xp, mod_p, *weights_bf16, w_s[l], b_s[l].T, *small, alpha)
        conv_p_rows.append(conv_p)
        conv_s_rows.append(conv_s)
        v_rows.append(v_s)
    stack = (lambda rows: rows[0][None]) if depth == 1 else jnp.stack
    return (xp, xs, stack(conv_p_rows), stack(conv_s_rows), stack(v_rows))
```

```python
import functools

import jax
import jax.numpy as jnp
from jax import lax
from jax.experimental import pallas as pl
from jax.experimental.pallas import tpu as pltpu

CHUNK = 128
GROUPS = 8
CONV_WIDTH = 3
LN_EPS = 1e-5
ROW_TILE = 512
SUB_TILE = 256
STAGE_SLOTS = 4
VMEM_LIMIT_BYTES = 62 * 1024 * 1024

_F32 = jnp.float32
_BF16 = jnp.bfloat16


def _dot(a, b):
    return jnp.dot(a, b, preferred_element_type=_F32)


def _layer_norm(x, gain, bias):
    mu = jnp.mean(x, axis=-1, keepdims=True)
    xc = x - mu
    var = jnp.mean(xc * xc, axis=-1, keepdims=True)
    return xc * lax.rsqrt(var + LN_EPS) * gain + bias


def _silu(x):
    return x * jax.nn.sigmoid(x)


_PACKED = jnp.uint32


def _packed_shape(w):
    return (w.shape[0] // 2, w.shape[1])


def _as_bf16(words):
    return pltpu.bitcast(words, _BF16)


def _prompt_kernel(x_ref, mod_ref, win_ref, wpa_ref, wpb_ref, wo_ref, ws_ref, bst_ref,
                   alg_ref, alb_ref, wconv_ref, bconv_ref, lng_ref, lnb_ref,
                   y_ref, conv_ref, wsm_scr, bias_scr, z_scr, *, alpha, steps_per_seq):
    tm, d = x_ref.shape
    sub = SUB_TILE
    gd = d // GROUPS
    i = pl.program_id(0)

    @pl.when(i == 0)
    def _():
        row = lax.broadcasted_iota(jnp.int32, (CHUNK, CHUNK), 0)
        col = lax.broadcasted_iota(jnp.int32, (CHUNK, CHUNK), 1)
        for g in range(GROUPS):
            wsm_scr[g] = jnp.where(col <= row, ws_ref[g], 0.0).astype(_BF16)
            bias_scr[:, g * gd:(g + 1) * gd] = jnp.broadcast_to(bst_ref[:, g:g + 1], (CHUNK, gd))

    @pl.when(i % steps_per_seq == 0)
    def _():
        z_scr[...] = jnp.zeros(z_scr.shape, _F32)

    mod = mod_ref[pl.ds(i // steps_per_seq, 1), :]
    shift, scale, gate = mod[:, 0:d], mod[:, d:2 * d], mod[:, 2 * d:3 * d]
    bias = bias_scr[...]
    wconv = wconv_ref[...]
    starts = range(0, tm, sub)
    tails = []

    def project(r0):
        h = (x_ref[r0:r0 + sub, :] * (1.0 + scale) + shift).astype(_BF16)
        return [_dot(h, _as_bf16(win_ref[:, k * d:(k + 1) * d]))
                for k in range(win_ref.shape[1] // d)]

    def mix(r0, p):
        u, v, z_a, b_g, c_g, h_b, z_b, g_a, g_b = p
        vb = _layer_norm(v, alg_ref[...], alb_ref[...]).astype(_BF16)
        s_cols = []
        for g in range(GROUPS):
            cols = slice(g * gd, (g + 1) * gd)
            pair_rows = []
            for c in range(0, sub // CHUNK, 2):
                rhs = jnp.concatenate([vb[c * CHUNK:(c + 1) * CHUNK, cols],
                                       vb[(c + 1) * CHUNK:(c + 2) * CHUNK, cols]], axis=1)
                res = _dot(wsm_scr[g], rhs)
                pair_rows.append(res[:, :gd] + bias[:, cols])
                pair_rows.append(res[:, gd:] + bias[:, cols])
            s_cols.append(jnp.concatenate(pair_rows, axis=0))
        s = jnp.concatenate(s_cols, axis=1)
        out_a = (u * s * _silu(z_a)).astype(_BF16)
        z = c_g * h_b
        hist = z_scr[...] if not tails else tails[-1]
        tails.append(z[sub - 8:sub, :])
        top_rows = lax.broadcasted_iota(jnp.int32, (8, d), 0)

        def delayed(k):
            rolled = pltpu.roll(z, k, axis=0)
            top = jnp.where(top_rows < k, pltpu.roll(hist, k, axis=0), rolled[0:8, :])
            return jnp.concatenate([top, rolled[8:, :]], axis=0)

        conv = (bconv_ref[...] + wconv[0:1, :] * delayed(2) + wconv[1:2, :] * delayed(1)
                + wconv[2:3, :] * z)
        out_b = (b_g * conv * _silu(z_b)).astype(_BF16)
        return out_a, out_b, jax.nn.sigmoid(g_a), jax.nn.sigmoid(g_b)

    def merge(m):
        out_a, out_b, sg_a, sg_b = m
        return (sg_a * _dot(out_a, _as_bf16(wpa_ref[...]))
                + sg_b * _dot(out_b, _as_bf16(wpb_ref[...]))).astype(_BF16)

    projected = [project(r0) for r0 in starts]
    mixed = [mix(r0, p) for r0, p in zip(starts, projected)]
    merged = [merge(m) for m in mixed]
    for r0, mg in zip(starts, merged):
        x = x_ref[r0:r0 + sub, :]
        y_ref[r0:r0 + sub, :] = _layer_norm(
            alpha * x + (1.0 + gate) * _dot(mg, _as_bf16(wo_ref[...])), lng_ref[...], lnb_ref[...])

    z_scr[...] = tails[-1]
    conv_ref[0] = tails[-1][8 - (CONV_WIDTH - 1):8, :]


def _resident(shape):
    nd = len(shape)
    return pl.BlockSpec(shape, lambda i: (0,) * nd, pipeline_mode=pl.Buffered(1))


def _prompt_call(x, mod, win, wpa, wpb, wo, w_s, b_s_t, alg, alb, wconv, bconv, lng, lnb, alpha):
    nb, length, d = x.shape
    tm = ROW_TILE
    assert length % tm == 0 and tm % SUB_TILE == 0 and SUB_TILE % (2 * CHUNK) == 0
    assert d % GROUPS == 0 and d // GROUPS == CHUNK
    steps_per_seq = length // tm
    small = [alg, alb, wconv, bconv, lng, lnb]
    y, conv = pl.pallas_call(
        functools.partial(_prompt_kernel, alpha=alpha, steps_per_seq=steps_per_seq),
        grid=(nb * steps_per_seq,),
        in_specs=[
            pl.BlockSpec((tm, d), lambda i: (i, 0)),
            _resident(mod.shape),
            _resident(win.shape), _resident(wpa.shape), _resident(wpb.shape), _resident(wo.shape),
            _resident(w_s.shape), _resident(b_s_t.shape),
        ] + [_resident(a.shape) for a in small],
        out_specs=[
            pl.BlockSpec((tm, d), lambda i: (i, 0)),
            pl.BlockSpec((1, CONV_WIDTH - 1, d), lambda i: (i // steps_per_seq, 0, 0)),
        ],
        out_shape=[
            jax.ShapeDtypeStruct((nb * length, d), _F32),
            jax.ShapeDtypeStruct((nb, CONV_WIDTH - 1, d), _F32),
        ],
        scratch_shapes=[
            pltpu.VMEM((GROUPS, CHUNK, CHUNK), _BF16),
            pltpu.VMEM((CHUNK, d), _F32),
            pltpu.VMEM((8, d), _F32),
        ],
        compiler_params=pltpu.CompilerParams(
            dimension_semantics=("arbitrary",),
            vmem_limit_bytes=VMEM_LIMIT_BYTES),
        name="prompt_layer",
    )(x.reshape(nb * length, d), mod, win, wpa, wpb, wo, w_s, b_s_t, *small)
    return y.reshape(nb, length, d), conv


_PROJ_ORDER = (1, 0, 2, 4, 5, 3, 6, 7, 8)


def _sample_kernel(ws4_ref, bs4_ref, cs_ref, cp_ref, bc_ref,
                   alg_ref, alb_ref, wconv_ref, bconv_ref, lng_ref, lnb_ref,
                   x_hbm, st_hbm, wc_hbm, win_hbm, wpa_hbm, wpb_hbm, wo_hbm,
                   y_hbm, conv_hbm, v_hbm, win_out, wpa_out, wpb_out, wo_out, modp_ref,
                   win_ref, wpa_ref, wpb_ref, wo_ref, stage, xbuf, stbuf, ybuf, cvbuf, vbuf,
                   sem, exp_sem, in_sem, res_sem, *, alpha):
    steps, n, d = xbuf.shape
    gd = d // GROUPS
    hist_rows = CONV_WIDTH - 1

    in_copies = ([pltpu.make_async_copy(x_hbm.at[:, t, :], xbuf.at[t], in_sem.at[t])
                  for t in range(steps)]
                 + [pltpu.make_async_copy(st_hbm.at[:, k, :], stbuf.at[k], in_sem.at[steps + k])
                    for k in range(hist_rows)])
    for cp in in_copies:
        cp.start()

    blocks = ([(wc_hbm, None, None, j) for j in range(wc_hbm.shape[1] // d)]
              + [(win_hbm, win_ref, win_out, k) for k in _PROJ_ORDER]
              + [(wpa_hbm, wpa_ref, wpa_out, 0), (wpb_hbm, wpb_ref, wpb_out, 0),
                 (wo_hbm, wo_ref, wo_out, 0)])
    exports = []
    position = [0]

    def stage_copy(i):
        src, _, _, k = blocks[i]
        slot = i % STAGE_SLOTS
        return pltpu.make_async_copy(src.at[:, k * d:(k + 1) * d], stage.at[slot], sem.at[slot])

    def next_weight():
        i = position[0]
        position[0] += 1
        stage_copy(i).wait()
        if i + STAGE_SLOTS - 1 < len(blocks):
            stage_copy(i + STAGE_SLOTS - 1).start()
        w = stage[i % STAGE_SLOTS].astype(_BF16)
        _, keep, out, k = blocks[i]
        if keep is not None:
            cols = slice(k * d, (k + 1) * d)
            keep[:, cols] = pltpu.bitcast(w, _PACKED)
            cp = pltpu.make_async_copy(keep.at[:, cols], out.at[:, cols], exp_sem.at[len(exports)])
            cp.start()
            exports.append(cp)
        return w

    for i in range(STAGE_SLOTS - 1):
        stage_copy(i).start()

    c_all = jnp.concatenate([cs_ref[...], cp_ref[...]], axis=0).astype(_BF16)
    bc = bc_ref[...]
    mods = []
    for j in range(3):
        m = _dot(c_all, next_weight()) + bc[:, j * d:(j + 1) * d]
        modp_ref[:, j * d:(j + 1) * d] = m[n:, :]
        mods.append(m[0:n, :])
    shift, scale, gate = mods

    for cp in in_copies:
        cp.wait()
    h = jnp.concatenate([(xbuf[t] * (1.0 + scale) + shift).astype(_BF16) for t in range(steps)],
                        axis=0)

    def rows(a, t):
        return a[t * n:(t + 1) * n, :]

    def proj():
        return _dot(h, next_weight())

    results = []

    def send(buf, t, dst):
        cp = pltpu.make_async_copy(buf.at[t], dst.at[:, t, :], res_sem.at[len(results)])
        cp.start()
        results.append(cp)

    vn = _layer_norm(proj(), alg_ref[...], alb_ref[...])
    s_rows = []
    for t in range(steps):
        vbuf[t] = rows(vn, t)
        send(vbuf, t, v_hbm)
        s_cols = []
        for g in range(GROUPS):
            cols = slice(g * gd, (g + 1) * gd)
            acc = jnp.full((n, gd), bs4_ref[g, t], _F32)
            for jj in range(t + 1):
                acc = acc + ws4_ref[g, t * steps + jj] * rows(vn, jj)[:, cols]
            s_cols.append(acc)
        s_rows.append(jnp.concatenate(s_cols, axis=1))
    s = jnp.concatenate(s_rows, axis=0)
    u = proj()
    out_a = (u * s * _silu(proj())).astype(_BF16)

    z = proj()
    z = z * proj()
    hist = [stbuf[k] for k in range(hist_rows)] + [rows(z, t) for t in range(steps)]
    for k in range(hist_rows):
        cvbuf[k] = hist[steps + k]
        send(cvbuf, k, conv_hbm)
    wconv = wconv_ref[...]
    conv = jnp.concatenate(
        [bconv_ref[...] + sum(wconv[k:k + 1, :] * hist[t + k] for k in range(CONV_WIDTH))
         for t in range(steps)], axis=0)
    b_g = proj()
    out_b = (b_g * conv * _silu(proj())).astype(_BF16)

    sg_a = jax.nn.sigmoid(proj())
    sg_b = jax.nn.sigmoid(proj())
    pa = _dot(out_a, next_weight())
    pb = _dot(out_b, next_weight())
    merged = (sg_a * pa + sg_b * pb).astype(_BF16)
    y = _dot(merged, next_weight())
    for t in range(steps):
        ybuf[t] = _layer_norm(alpha * xbuf[t] + (1.0 + gate) * rows(y, t), lng_ref[...], lnb_ref[...])
        send(ybuf, t, y_hbm)

    for cp in exports + results:
        cp.wait()


def _sample_call(x, state, c_s, c_p, w_c, b_c, win, wpa, wpb, wo, ws4, bs4,
                 alg, alb, wconv, bconv, lng, lnb, alpha):
    n, steps, d = x.shape
    nb = c_p.shape[0]
    hist_rows = CONV_WIDTH - 1
    assert hist_rows <= steps <= CHUNK and w_c.shape == (d, 3 * d)
    vmem = pl.BlockSpec(memory_space=pltpu.VMEM)
    smem = pl.BlockSpec(memory_space=pltpu.SMEM)
    hbm = pl.BlockSpec(memory_space=pl.ANY)
    weights = (win, wpa, wpb, wo)
    n_exports = sum(w.shape[1] // d for w in weights)
    n_results = 2 * steps + hist_rows
    y, conv, v, *weights_bf16, mod_p = pl.pallas_call(
        functools.partial(_sample_kernel, alpha=alpha),
        in_specs=[smem, smem] + [vmem] * 9 + [hbm] * 7,
        out_specs=[hbm] * (3 + len(weights)) + [vmem],
        out_shape=[
            jax.ShapeDtypeStruct((n, steps, d), _F32),
            jax.ShapeDtypeStruct((n, hist_rows, d), _F32),
            jax.ShapeDtypeStruct((n, steps, d), _F32),
        ] + [jax.ShapeDtypeStruct(_packed_shape(w), _PACKED) for w in weights] + [
            jax.ShapeDtypeStruct((nb, 3 * d), _F32)],
        scratch_shapes=[pltpu.VMEM(_packed_shape(w), _PACKED) for w in weights] + [
            pltpu.VMEM((STAGE_SLOTS, d, d), _F32),
            pltpu.VMEM((steps, n, d), _F32), pltpu.VMEM((hist_rows, n, d), _F32),
            pltpu.VMEM((steps, n, d), _F32), pltpu.VMEM((hist_rows, n, d), _F32),
            pltpu.VMEM((steps, n, d), _F32),
            pltpu.SemaphoreType.DMA((STAGE_SLOTS,)), pltpu.SemaphoreType.DMA((n_exports,)),
            pltpu.SemaphoreType.DMA((steps + hist_rows,)), pltpu.SemaphoreType.DMA((n_results,))],
        compiler_params=pltpu.CompilerParams(vmem_limit_bytes=VMEM_LIMIT_BYTES),
        name="sample_layer",
    )(ws4, bs4, c_s, c_p, b_c.reshape(1, 3 * d), alg, alb, wconv, bconv, lng, lnb,
      x, state, w_c, win, wpa, wpb, wo)
    return y, conv, v, weights_bf16, mod_p


def kernel(x_prompt, x_sample, state_conv, c_prompt, c_sample, w_c, b_c, w_in, a_ln_g, a_ln_b,
           w_s, b_s, w_conv, b_conv, w_pa, w_pb, w_o, ln_g, ln_b):
    depth = w_in.shape[0]
    d = x_prompt.shape[-1]
    steps = x_sample.shape[1]
    alpha = (2.0 * depth) ** 0.25

    xp, xs = x_prompt, x_sample
    conv_p_rows, conv_s_rows, v_rows = [], [], []
    for l in range(depth):
        row = lambda a: a.reshape(1, d)
        small = (row(a_ln_g[l]), row(a_ln_b[l]), w_conv[l], row(b_conv[l]), row(ln_g[l]), row(ln_b[l]))
        ws4 = w_s[l][:, :steps, :steps].reshape(GROUPS, steps * steps)
        bs4 = b_s[l][:, :steps]
        xs, conv_s, v_s, weights_bf16, mod_p = _sample_call(
            xs, state_conv[l], c_sample, c_prompt, w_c[l], b_c[l],
            w_in[l], w_pa[l], w_pb[l], w_o[l], ws4, bs4, *small, alpha)
        xp, conv_p = _prompt_call(xp, mod_p, *weights_bf16, w_s[l], b_s[l].T, *small, alpha)
        conv_p_rows.append(conv_p)
        conv_s_rows.append(conv_s)
        v_rows.append(v_s)
    stack = (lambda rows: rows[0][None]) if depth == 1 else jnp.stack
    return (xp, xs, stack(conv_p_rows), stack(conv_s_rows), stack(v_rows))
```

```python
import functools

import jax
import jax.numpy as jnp
from jax import lax
from jax.experimental import pallas as pl
from jax.experimental.pallas import tpu as pltpu

CHUNK = 128
GROUPS = 8
CONV_WIDTH = 3
LN_EPS = 1e-5
ROW_TILE = 512
SUB_TILE = 256
STAGE_SLOTS = 4
VMEM_LIMIT_BYTES = 62 * 1024 * 1024

_F32 = jnp.float32
_BF16 = jnp.bfloat16


def _dot(a, b):
    return jnp.dot(a, b, preferred_element_type=_F32)


def _layer_norm(x, gain, bias):
    mu = jnp.mean(x, axis=-1, keepdims=True)
    xc = x - mu
    var = jnp.mean(xc * xc, axis=-1, keepdims=True)
    return xc * lax.rsqrt(var + LN_EPS) * gain + bias


def _silu(x):
    return x * jax.nn.sigmoid(x)


_PACKED = jnp.uint32


def _packed_shape(w):
    return (w.shape[0] // 2, w.shape[1])


def _as_bf16(words):
    return pltpu.bitcast(words, _BF16)


def _prompt_kernel(x_ref, mod_ref, win_ref, wpa_ref, wpb_ref, wo_ref, ws_ref, bst_ref,
                   alg_ref, alb_ref, wconv_ref, bconv_ref, lng_ref, lnb_ref,
                   y_ref, conv_ref, wsm_scr, bias_scr, z_scr, *, alpha, steps_per_seq):
    tm, d = x_ref.shape
    sub = SUB_TILE
    gd = d // GROUPS
    i = pl.program_id(0)

    @pl.when(i == 0)
    def _():
        row = lax.broadcasted_iota(jnp.int32, (CHUNK, CHUNK), 0)
        col = lax.broadcasted_iota(jnp.int32, (CHUNK, CHUNK), 1)
        for g in range(GROUPS):
            wsm_scr[g] = jnp.where(col <= row, ws_ref[g], 0.0).astype(_BF16)
            bias_scr[:, g * gd:(g + 1) * gd] = jnp.broadcast_to(bst_ref[:, g:g + 1], (CHUNK, gd))

    @pl.when(i % steps_per_seq == 0)
    def _():
        z_scr[...] = jnp.zeros(z_scr.shape, _F32)

    mod = mod_ref[pl.ds(i // steps_per_seq, 1), :]
    shift, scale, gate = mod[:, 0:d], mod[:, d:2 * d], mod[:, 2 * d:3 * d]
    bias = bias_scr[...]
    wconv = [wconv_ref[:, k * d:(k + 1) * d] for k in range(CONV_WIDTH)]
    starts = range(0, tm, sub)
    tails = []

    def project(r0):
        h = (x_ref[r0:r0 + sub, :] * (1.0 + scale) + shift).astype(_BF16)
        return [_dot(h, _as_bf16(win_ref[:, k * d:(k + 1) * d]))
                for k in range(win_ref.shape[1] // d)]

    def mix(r0, p):
        u, v, z_a, b_g, c_g, h_b, z_b, g_a, g_b = p
        vb = _layer_norm(v, alg_ref[...], alb_ref[...]).astype(_BF16)
        s_cols = []
        for g in range(GROUPS):
            cols = slice(g * gd, (g + 1) * gd)
            pair_rows = []
            for c in range(0, sub // CHUNK, 2):
                rhs = jnp.concatenate([vb[c * CHUNK:(c + 1) * CHUNK, cols],
                                       vb[(c + 1) * CHUNK:(c + 2) * CHUNK, cols]], axis=1)
                res = _dot(wsm_scr[g], rhs)
                pair_rows.append(res[:, :gd] + bias[:, cols])
                pair_rows.append(res[:, gd:] + bias[:, cols])
            s_cols.append(jnp.concatenate(pair_rows, axis=0))
        s = jnp.concatenate(s_cols, axis=1)
        out_a = (u * s * _silu(z_a)).astype(_BF16)
        z = c_g * h_b
        hist = z_scr[...] if not tails else tails[-1]
        tails.append(z[sub - 8:sub, :])
        top_rows = lax.broadcasted_iota(jnp.int32, (8, d), 0)

        def delayed(k):
            rolled = pltpu.roll(z, k, axis=0)
            top = jnp.where(top_rows < k, pltpu.roll(hist, k, axis=0), rolled[0:8, :])
            return jnp.concatenate([top, rolled[8:, :]], axis=0)

        conv = bconv_ref[...] + wconv[0] * delayed(2) + wconv[1] * delayed(1) + wconv[2] * z
        out_b = (b_g * conv * _silu(z_b)).astype(_BF16)
        return out_a, out_b, jax.nn.sigmoid(g_a), jax.nn.sigmoid(g_b)

    def merge(m):
        out_a, out_b, sg_a, sg_b = m
        return (sg_a * _dot(out_a, _as_bf16(wpa_ref[...]))
                + sg_b * _dot(out_b, _as_bf16(wpb_ref[...]))).astype(_BF16)

    projected = [project(r0) for r0 in starts]
    mixed = [mix(r0, p) for r0, p in zip(starts, projected)]
    merged = [merge(m) for m in mixed]
    for r0, mg in zip(starts, merged):
        x = x_ref[r0:r0 + sub, :]
        y_ref[r0:r0 + sub, :] = _layer_norm(
            alpha * x + (1.0 + gate) * _dot(mg, _as_bf16(wo_ref[...])), lng_ref[...], lnb_ref[...])

    z_scr[...] = tails[-1]
    conv_ref[0] = tails[-1][8 - (CONV_WIDTH - 1):8, :]


def _resident(shape):
    nd = len(shape)
    return pl.BlockSpec(shape, lambda i: (0,) * nd, pipeline_mode=pl.Buffered(1))


def _prompt_call(x, mod, win, wpa, wpb, wo, w_s, b_s_t, alg, alb, wconv, bconv, lng, lnb, alpha):
    nb, length, d = x.shape
    tm = ROW_TILE
    assert length % tm == 0 and tm % SUB_TILE == 0 and SUB_TILE % (2 * CHUNK) == 0
    assert d % GROUPS == 0 and d // GROUPS == CHUNK
    steps_per_seq = length // tm
    small = [alg, alb, wconv, bconv, lng, lnb]
    y, conv = pl.pallas_call(
        functools.partial(_prompt_kernel, alpha=alpha, steps_per_seq=steps_per_seq),
        grid=(nb * steps_per_seq,),
        in_specs=[
            pl.BlockSpec((tm, d), lambda i: (i, 0)),
            _resident(mod.shape),
            _resident(win.shape), _resident(wpa.shape), _resident(wpb.shape), _resident(wo.shape),
            _resident(w_s.shape), _resident(b_s_t.shape),
        ] + [_resident(a.shape) for a in small],
        out_specs=[
            pl.BlockSpec((tm, d), lambda i: (i, 0)),
            pl.BlockSpec((1, CONV_WIDTH - 1, d), lambda i: (i // steps_per_seq, 0, 0)),
        ],
        out_shape=[
            jax.ShapeDtypeStruct((nb * length, d), _F32),
            jax.ShapeDtypeStruct((nb, CONV_WIDTH - 1, d), _F32),
        ],
        scratch_shapes=[
            pltpu.VMEM((GROUPS, CHUNK, CHUNK), _BF16),
            pltpu.VMEM((CHUNK, d), _F32),
            pltpu.VMEM((8, d), _F32),
        ],
        compiler_params=pltpu.CompilerParams(
            dimension_semantics=("arbitrary",),
            vmem_limit_bytes=VMEM_LIMIT_BYTES),
        name="prompt_layer",
    )(x.reshape(nb * length, d), mod, win, wpa, wpb, wo, w_s, b_s_t, *small)
    return y.reshape(nb, length, d), conv


_PROJ_ORDER = (1, 0, 2, 4, 5, 3, 6, 7, 8)


def _sample_kernel(bs_ref, cs_ref, cp_ref, bc_ref,
                   alg_ref, alb_ref, wconv_ref, bconv_ref, lng_ref, lnb_ref,
                   x_hbm, st_hbm, ws_hbm, wc_hbm, win_hbm, wpa_hbm, wpb_hbm, wo_hbm,
                   y_hbm, conv_hbm, v_hbm, win_out, wpa_out, wpb_out, wo_out, modp_ref,
                   win_ref, wpa_ref, wpb_ref, wo_ref, stage, xbuf, stbuf, ybuf, cvbuf, vbuf, ws_smem,
                   sem, exp_sem, in_sem, res_sem, ws_sem, *, alpha):
    steps, n, d = xbuf.shape
    gd = d // GROUPS
    hist_rows = CONV_WIDTH - 1

    ws_copy = pltpu.make_async_copy(ws_hbm.at[:, 0:ws_smem.shape[1], :], ws_smem, ws_sem.at[0])
    ws_copy.start()

    in_copies = ([pltpu.make_async_copy(x_hbm.at[:, t, :], xbuf.at[t], in_sem.at[t])
                  for t in range(steps)]
                 + [pltpu.make_async_copy(st_hbm.at[:, k, :], stbuf.at[k], in_sem.at[steps + k])
                    for k in range(hist_rows)])
    for cp in in_copies:
        cp.start()

    blocks = ([(wc_hbm, None, None, j) for j in range(wc_hbm.shape[1] // d)]
              + [(win_hbm, win_ref, win_out, k) for k in _PROJ_ORDER]
              + [(wpa_hbm, wpa_ref, wpa_out, 0), (wpb_hbm, wpb_ref, wpb_out, 0),
                 (wo_hbm, wo_ref, wo_out, 0)])
    exports = []
    position = [0]

    def stage_copy(i):
        src, _, _, k = blocks[i]
        slot = i % STAGE_SLOTS
        return pltpu.make_async_copy(src.at[:, k * d:(k + 1) * d], stage.at[slot], sem.at[slot])

    def next_weight():
        i = position[0]
        position[0] += 1
        stage_copy(i).wait()
        if i + STAGE_SLOTS - 1 < len(blocks):
            stage_copy(i + STAGE_SLOTS - 1).start()
        w = stage[i % STAGE_SLOTS].astype(_BF16)
        _, keep, out, k = blocks[i]
        if keep is not None:
            cols = slice(k * d, (k + 1) * d)
            keep[:, cols] = pltpu.bitcast(w, _PACKED)
            cp = pltpu.make_async_copy(keep.at[:, cols], out.at[:, cols], exp_sem.at[len(exports)])
            cp.start()
            exports.append(cp)
        return w

    for i in range(STAGE_SLOTS - 1):
        stage_copy(i).start()

    c_all = jnp.concatenate([cs_ref[...], cp_ref[...]], axis=0).astype(_BF16)
    bc = bc_ref[...]
    mods = []
    for j in range(3):
        m = _dot(c_all, next_weight()) + bc[:, j * d:(j + 1) * d]
        modp_ref[:, j * d:(j + 1) * d] = m[n:, :]
        mods.append(m[0:n, :])
    shift, scale, gate = mods

    for cp in in_copies:
        cp.wait()
    ws_copy.wait()
    h = jnp.concatenate([(xbuf[t] * (1.0 + scale) + shift).astype(_BF16) for t in range(steps)],
                        axis=0)

    def rows(a, t):
        return a[t * n:(t + 1) * n, :]

    def proj():
        return _dot(h, next_weight())

    results = []

    def send(buf, t, dst):
        cp = pltpu.make_async_copy(buf.at[t], dst.at[:, t, :], res_sem.at[len(results)])
        cp.start()
        results.append(cp)

    vn = _layer_norm(proj(), alg_ref[...], alb_ref[...])
    s_rows = []
    for t in range(steps):
        vbuf[t] = rows(vn, t)
        send(vbuf, t, v_hbm)
        s_cols = []
        for g in range(GROUPS):
            cols = slice(g * gd, (g + 1) * gd)
            acc = jnp.full((n, gd), bs_ref[g, t], _F32)
            for jj in range(t + 1):
                acc = acc + ws_smem[g, t, jj] * rows(vn, jj)[:, cols]
            s_cols.append(acc)
        s_rows.append(jnp.concatenate(s_cols, axis=1))
    s = jnp.concatenate(s_rows, axis=0)
    u = proj()
    out_a = (u * s * _silu(proj())).astype(_BF16)

    z = proj()
    z = z * proj()
    hist = [stbuf[k] for k in range(hist_rows)] + [rows(z, t) for t in range(steps)]
    for k in range(hist_rows):
        cvbuf[k] = hist[steps + k]
        send(cvbuf, k, conv_hbm)
    wconv = [wconv_ref[:, k * d:(k + 1) * d] for k in range(CONV_WIDTH)]
    conv = jnp.concatenate(
        [bconv_ref[...] + sum(wconv[k] * hist[t + k] for k in range(CONV_WIDTH))
         for t in range(steps)], axis=0)
    b_g = proj()
    out_b = (b_g * conv * _silu(proj())).astype(_BF16)

    sg_a = jax.nn.sigmoid(proj())
    sg_b = jax.nn.sigmoid(proj())
    pa = _dot(out_a, next_weight())
    pb = _dot(out_b, next_weight())
    merged = (sg_a * pa + sg_b * pb).astype(_BF16)
    y = _dot(merged, next_weight())
    for t in range(steps):
        ybuf[t] = _layer_norm(alpha * xbuf[t] + (1.0 + gate) * rows(y, t), lng_ref[...], lnb_ref[...])
        send(ybuf, t, y_hbm)

    for cp in exports + results:
        cp.wait()


def _sample_call(x, state, c_s, c_p, w_c, b_c, win, wpa, wpb, wo, w_s, b_s,
                 alg, alb, wconv, bconv, lng, lnb, alpha):
    n, steps, d = x.shape
    nb = c_p.shape[0]
    hist_rows = CONV_WIDTH - 1
    assert hist_rows <= steps <= CHUNK and w_c.shape == (d, 3 * d)
    vmem = pl.BlockSpec(memory_space=pltpu.VMEM)
    smem = pl.BlockSpec(memory_space=pltpu.SMEM)
    hbm = pl.BlockSpec(memory_space=pl.ANY)
    weights = (win, wpa, wpb, wo)
    n_exports = sum(w.shape[1] // d for w in weights)
    n_results = 2 * steps + hist_rows
    ws_rows = -(-steps // 8) * 8
    y, conv, v, *weights_bf16, mod_p = pl.pallas_call(
        functools.partial(_sample_kernel, alpha=alpha),
        in_specs=[smem] + [vmem] * 9 + [hbm] * 8,
        out_specs=[hbm] * (3 + len(weights)) + [vmem],
        out_shape=[
            jax.ShapeDtypeStruct((n, steps, d), _F32),
            jax.ShapeDtypeStruct((n, hist_rows, d), _F32),
            jax.ShapeDtypeStruct((n, steps, d), _F32),
        ] + [jax.ShapeDtypeStruct(_packed_shape(w), _PACKED) for w in weights] + [
            jax.ShapeDtypeStruct((nb, 3 * d), _F32)],
        scratch_shapes=[pltpu.VMEM(_packed_shape(w), _PACKED) for w in weights] + [
            pltpu.VMEM((STAGE_SLOTS, d, d), _F32),
            pltpu.VMEM((steps, n, d), _F32), pltpu.VMEM((hist_rows, n, d), _F32),
            pltpu.VMEM((steps, n, d), _F32), pltpu.VMEM((hist_rows, n, d), _F32),
            pltpu.VMEM((steps, n, d), _F32),
            pltpu.SMEM((w_s.shape[0], ws_rows, w_s.shape[2]), _F32),
            pltpu.SemaphoreType.DMA((STAGE_SLOTS,)), pltpu.SemaphoreType.DMA((n_exports,)),
            pltpu.SemaphoreType.DMA((steps + hist_rows,)), pltpu.SemaphoreType.DMA((n_results,)),
            pltpu.SemaphoreType.DMA((1,))],
        compiler_params=pltpu.CompilerParams(vmem_limit_bytes=VMEM_LIMIT_BYTES),
        name="sample_layer",
    )(b_s, c_s, c_p, b_c.reshape(1, 3 * d), alg, alb, wconv, bconv, lng, lnb,
      x, state, w_s, w_c, win, wpa, wpb, wo)
    return y, conv, v, weights_bf16, mod_p


def kernel(x_prompt, x_sample, state_conv, c_prompt, c_sample, w_c, b_c, w_in, a_ln_g, a_ln_b,
           w_s, b_s, w_conv, b_conv, w_pa, w_pb, w_o, ln_g, ln_b):
    depth = w_in.shape[0]
    d = x_prompt.shape[-1]
    alpha = (2.0 * depth) ** 0.25

    xp, xs = x_prompt, x_sample
    conv_p_rows, conv_s_rows, v_rows = [], [], []
    for l in range(depth):
        row = lambda a: a.reshape(1, d)
        small = (row(a_ln_g[l]), row(a_ln_b[l]), w_conv[l].reshape(1, CONV_WIDTH * d),
                 row(b_conv[l]), row(ln_g[l]), row(ln_b[l]))
        xs, conv_s, v_s, weights_bf16, mod_p = _sample_call(
            xs, state_conv[l], c_sample, c_prompt, w_c[l], b_c[l],
            w_in[l], w_pa[l], w_pb[l], w_o[l], w_s[l], b_s[l], *small, alpha)
        xp, conv_p = _prompt_call(xp, mod_p, *weights_bf16, w_s[l], b_s[l].T, *small, alpha)
        conv_p_rows.append(conv_p)
        conv_s_rows.append(conv_s)
        v_rows.append(v_s)
    stack = (lambda rows: rows[0][None]) if depth == 1 else jnp.stack
    return (xp, xs, stack(conv_p_rows), stack(conv_s_rows), stack(v_rows))
```

```python
import functools

import jax
import jax.numpy as jnp
from jax import lax
from jax.experimental import pallas as pl
from jax.experimental.pallas import tpu as pltpu

CHUNK = 128
GROUPS = 8
CONV_WIDTH = 3
LN_EPS = 1e-5
ROW_TILE = 512
SUB_TILE = 256
STAGE_SLOTS = 4
VMEM_LIMIT_BYTES = 62 * 1024 * 1024

_F32 = jnp.float32
_BF16 = jnp.bfloat16


def _dot(a, b):
    return jnp.dot(a, b, preferred_element_type=_F32)


def _layer_norm(x, gain, bias):
    mu = jnp.mean(x, axis=-1, keepdims=True)
    xc = x - mu
    var = jnp.mean(xc * xc, axis=-1, keepdims=True)
    return xc * lax.rsqrt(var + LN_EPS) * gain + bias


def _silu(x):
    return x * jax.nn.sigmoid(x)


_PACKED = jnp.uint32


def _packed_shape(w):
    return (w.shape[0] // 2, w.shape[1])


def _as_bf16(words):
    return pltpu.bitcast(words, _BF16)


def _prompt_kernel(x_ref, mod_ref, win_ref, wpa_ref, wpb_ref, wo_ref, wsm_ref, bias_ref,
                   alg_ref, alb_ref, wconv_ref, bconv_ref, lng_ref, lnb_ref,
                   y_ref, conv_ref, z_scr, *, alpha, steps_per_seq):
    tm, d = x_ref.shape
    sub = SUB_TILE
    gd = d // GROUPS
    i = pl.program_id(0)

    mod = mod_ref[pl.ds(i // steps_per_seq, 1), :]
    shift, scale, gate = mod[:, 0:d], mod[:, d:2 * d], mod[:, 2 * d:3 * d]
    bias = bias_ref[...]
    carried = jnp.where(i % steps_per_seq == 0, 0.0, z_scr[...])
    wconv = [wconv_ref[:, k * d:(k + 1) * d] for k in range(CONV_WIDTH)]
    starts = range(0, tm, sub)
    tails = []

    def project(r0):
        h = (x_ref[r0:r0 + sub, :] * (1.0 + scale) + shift).astype(_BF16)
        return [_dot(h, _as_bf16(win_ref[:, k * d:(k + 1) * d]))
                for k in range(win_ref.shape[1] // d)]

    def mix(r0, p):
        u, v, z_a, b_g, c_g, h_b, z_b, g_a, g_b = p
        vb = _layer_norm(v, alg_ref[...], alb_ref[...]).astype(_BF16)
        s_cols = []
        for g in range(GROUPS):
            cols = slice(g * gd, (g + 1) * gd)
            pair_rows = []
            for c in range(0, sub // CHUNK, 2):
                rhs = jnp.concatenate([vb[c * CHUNK:(c + 1) * CHUNK, cols],
                                       vb[(c + 1) * CHUNK:(c + 2) * CHUNK, cols]], axis=1)
                res = _dot(_as_bf16(wsm_ref[g]), rhs)
                pair_rows.append(res[:, :gd] + bias[:, cols])
                pair_rows.append(res[:, gd:] + bias[:, cols])
            s_cols.append(jnp.concatenate(pair_rows, axis=0))
        s = jnp.concatenate(s_cols, axis=1)
        out_a = (u * s * _silu(z_a)).astype(_BF16)
        z = c_g * h_b
        hist = carried if not tails else tails[-1]
        tails.append(z[sub - 8:sub, :])
        top_rows = lax.broadcasted_iota(jnp.int32, (8, d), 0)

        def delayed(k):
            rolled = pltpu.roll(z, k, axis=0)
            top = jnp.where(top_rows < k, pltpu.roll(hist, k, axis=0), rolled[0:8, :])
            return jnp.concatenate([top, rolled[8:, :]], axis=0)

        conv = bconv_ref[...] + wconv[0] * delayed(2) + wconv[1] * delayed(1) + wconv[2] * z
        out_b = (b_g * conv * _silu(z_b)).astype(_BF16)
        return out_a, out_b, jax.nn.sigmoid(g_a), jax.nn.sigmoid(g_b)

    def merge(m):
        out_a, out_b, sg_a, sg_b = m
        return (sg_a * _dot(out_a, _as_bf16(wpa_ref[...]))
                + sg_b * _dot(out_b, _as_bf16(wpb_ref[...]))).astype(_BF16)

    projected = [project(r0) for r0 in starts]
    mixed = [mix(r0, p) for r0, p in zip(starts, projected)]
    merged = [merge(m) for m in mixed]
    for r0, mg in zip(starts, merged):
        x = x_ref[r0:r0 + sub, :]
        y_ref[r0:r0 + sub, :] = _layer_norm(
            alpha * x + (1.0 + gate) * _dot(mg, _as_bf16(wo_ref[...])), lng_ref[...], lnb_ref[...])

    z_scr[...] = tails[-1]
    conv_ref[0] = tails[-1][8 - (CONV_WIDTH - 1):8, :]


def _resident(shape):
    nd = len(shape)
    return pl.BlockSpec(shape, lambda i: (0,) * nd, pipeline_mode=pl.Buffered(1))


def _prompt_call(x, mod, win, wpa, wpb, wo, wsm, bias, alg, alb, wconv, bconv, lng, lnb, alpha):
    nb, length, d = x.shape
    tm = ROW_TILE
    assert length % tm == 0 and tm % SUB_TILE == 0 and SUB_TILE % (2 * CHUNK) == 0
    assert d % GROUPS == 0 and d // GROUPS == CHUNK
    steps_per_seq = length // tm
    small = [alg, alb, wconv, bconv, lng, lnb]
    y, conv = pl.pallas_call(
        functools.partial(_prompt_kernel, alpha=alpha, steps_per_seq=steps_per_seq),
        grid=(nb * steps_per_seq,),
        in_specs=[
            pl.BlockSpec((tm, d), lambda i: (i, 0)),
            _resident(mod.shape),
            _resident(win.shape), _resident(wpa.shape), _resident(wpb.shape), _resident(wo.shape),
            _resident(wsm.shape), _resident(bias.shape),
        ] + [_resident(a.shape) for a in small],
        out_specs=[
            pl.BlockSpec((tm, d), lambda i: (i, 0)),
            pl.BlockSpec((1, CONV_WIDTH - 1, d), lambda i: (i // steps_per_seq, 0, 0)),
        ],
        out_shape=[
            jax.ShapeDtypeStruct((nb * length, d), _F32),
            jax.ShapeDtypeStruct((nb, CONV_WIDTH - 1, d), _F32),
        ],
        scratch_shapes=[pltpu.VMEM((8, d), _F32)],
        compiler_params=pltpu.CompilerParams(
            dimension_semantics=("arbitrary",),
            vmem_limit_bytes=VMEM_LIMIT_BYTES),
        name="prompt_layer",
    )(x.reshape(nb * length, d), mod, win, wpa, wpb, wo, wsm, bias, *small)
    return y.reshape(nb, length, d), conv


_PROJ_ORDER = (1, 0, 2, 4, 5, 3, 6, 7, 8)


def _sample_kernel(bs_ref, ws_ref, cs_ref, cp_ref, bc_ref,
                   alg_ref, alb_ref, wconv_ref, bconv_ref, lng_ref, lnb_ref,
                   x_hbm, st_hbm, ws_hbm, wc_hbm, win_hbm, wpa_hbm, wpb_hbm, wo_hbm,
                   y_hbm, conv_hbm, v_hbm, win_out, wpa_out, wpb_out, wo_out, modp_ref,
                   wsm_ref, bias_ref,
                   win_ref, wpa_ref, wpb_ref, wo_ref, stage, xbuf, stbuf, ybuf, cvbuf, vbuf, ws_smem,
                   sem, exp_sem, in_sem, res_sem, ws_sem, *, alpha):
    steps, n, d = xbuf.shape
    gd = d // GROUPS
    hist_rows = CONV_WIDTH - 1

    ws_copy = pltpu.make_async_copy(ws_hbm.at[:, 0:ws_smem.shape[1], :], ws_smem, ws_sem.at[0])
    ws_copy.start()

    in_copies = ([pltpu.make_async_copy(x_hbm.at[:, t, :], xbuf.at[t], in_sem.at[t])
                  for t in range(steps)]
                 + [pltpu.make_async_copy(st_hbm.at[:, k, :], stbuf.at[k], in_sem.at[steps + k])
                    for k in range(hist_rows)])
    for cp in in_copies:
        cp.start()

    blocks = ([(wc_hbm, None, None, j) for j in range(wc_hbm.shape[1] // d)]
              + [(win_hbm, win_ref, win_out, k) for k in _PROJ_ORDER]
              + [(wpa_hbm, wpa_ref, wpa_out, 0), (wpb_hbm, wpb_ref, wpb_out, 0),
                 (wo_hbm, wo_ref, wo_out, 0)])
    exports = []
    position = [0]

    def stage_copy(i):
        src, _, _, k = blocks[i]
        slot = i % STAGE_SLOTS
        return pltpu.make_async_copy(src.at[:, k * d:(k + 1) * d], stage.at[slot], sem.at[slot])

    def next_weight():
        i = position[0]
        position[0] += 1
        stage_copy(i).wait()
        if i + STAGE_SLOTS - 1 < len(blocks):
            stage_copy(i + STAGE_SLOTS - 1).start()
        w = stage[i % STAGE_SLOTS].astype(_BF16)
        _, keep, out, k = blocks[i]
        if keep is not None:
            cols = slice(k * d, (k + 1) * d)
            keep[:, cols] = pltpu.bitcast(w, _PACKED)
            cp = pltpu.make_async_copy(keep.at[:, cols], out.at[:, cols], exp_sem.at[len(exports)])
            cp.start()
            exports.append(cp)
        return w

    for i in range(STAGE_SLOTS - 1):
        stage_copy(i).start()

    c_all = jnp.concatenate([cs_ref[...], cp_ref[...]], axis=0).astype(_BF16)
    bc = bc_ref[...]
    mods = []
    for j in range(3):
        m = _dot(c_all, next_weight()) + bc[:, j * d:(j + 1) * d]
        modp_ref[:, j * d:(j + 1) * d] = m[n:, :]
        mods.append(m[0:n, :])
    shift, scale, gate = mods

    row = lax.broadcasted_iota(jnp.int32, (CHUNK, CHUNK), 0)
    col = lax.broadcasted_iota(jnp.int32, (CHUNK, CHUNK), 1)
    for g in range(GROUPS):
        wsm_ref[g] = pltpu.bitcast(jnp.where(col <= row, ws_ref[g], 0.0).astype(_BF16), _PACKED)

    for t in range(CHUNK):
        for g in range(GROUPS):
            bias_ref[t:t + 1, g * gd:(g + 1) * gd] = jnp.full((1, gd), bs_ref[g, t], _F32)

    for cp in in_copies:
        cp.wait()
    ws_copy.wait()
    h = jnp.concatenate([(xbuf[t] * (1.0 + scale) + shift).astype(_BF16) for t in range(steps)],
                        axis=0)

    def rows(a, t):
        return a[t * n:(t + 1) * n, :]

    def proj():
        return _dot(h, next_weight())

    results = []

    def send(buf, t, dst):
        cp = pltpu.make_async_copy(buf.at[t], dst.at[:, t, :], res_sem.at[len(results)])
        cp.start()
        results.append(cp)

    vn = _layer_norm(proj(), alg_ref[...], alb_ref[...])
    s_rows = []
    for t in range(steps):
        vbuf[t] = rows(vn, t)
        send(vbuf, t, v_hbm)
        s_cols = []
        for g in range(GROUPS):
            cols = slice(g * gd, (g + 1) * gd)
            acc = jnp.full((n, gd), bs_ref[g, t], _F32)
            for jj in range(t + 1):
                acc = acc + ws_smem[g, t, jj] * rows(vn, jj)[:, cols]
            s_cols.append(acc)
        s_rows.append(jnp.concatenate(s_cols, axis=1))
    s = jnp.concatenate(s_rows, axis=0)
    u = proj()
    out_a = (u * s * _silu(proj())).astype(_BF16)

    z = proj()
    z = z * proj()
    hist = [stbuf[k] for k in range(hist_rows)] + [rows(z, t) for t in range(steps)]
    for k in range(hist_rows):
        cvbuf[k] = hist[steps + k]
        send(cvbuf, k, conv_hbm)
    wconv = [wconv_ref[:, k * d:(k + 1) * d] for k in range(CONV_WIDTH)]
    conv = jnp.concatenate(
        [bconv_ref[...] + sum(wconv[k] * hist[t + k] for k in range(CONV_WIDTH))
         for t in range(steps)], axis=0)
    b_g = proj()
    out_b = (b_g * conv * _silu(proj())).astype(_BF16)

    sg_a = jax.nn.sigmoid(proj())
    sg_b = jax.nn.sigmoid(proj())
    pa = _dot(out_a, next_weight())
    pb = _dot(out_b, next_weight())
    merged = (sg_a * pa + sg_b * pb).astype(_BF16)
    y = _dot(merged, next_weight())
    for t in range(steps):
        ybuf[t] = _layer_norm(alpha * xbuf[t] + (1.0 + gate) * rows(y, t), lng_ref[...], lnb_ref[...])
        send(ybuf, t, y_hbm)

    for cp in exports + results:
        cp.wait()


def _sample_call(x, state, c_s, c_p, w_c, b_c, win, wpa, wpb, wo, w_s, b_s,
                 alg, alb, wconv, bconv, lng, lnb, alpha):
    n, steps, d = x.shape
    nb = c_p.shape[0]
    hist_rows = CONV_WIDTH - 1
    assert hist_rows <= steps <= CHUNK and w_c.shape == (d, 3 * d)
    assert w_s.shape == (GROUPS, CHUNK, CHUNK) and d == GROUPS * CHUNK
    vmem = pl.BlockSpec(memory_space=pltpu.VMEM)
    smem = pl.BlockSpec(memory_space=pltpu.SMEM)
    hbm = pl.BlockSpec(memory_space=pl.ANY)
    weights = (win, wpa, wpb, wo)
    n_exports = sum(w.shape[1] // d for w in weights)
    n_results = 2 * steps + hist_rows
    ws_rows = -(-steps // 8) * 8
    y, conv, v, *weights_bf16, mod_p, wsm, bias = pl.pallas_call(
        functools.partial(_sample_kernel, alpha=alpha),
        in_specs=[smem] + [vmem] * 10 + [hbm] * 8,
        out_specs=[hbm] * (3 + len(weights)) + [vmem] * 3,
        out_shape=[
            jax.ShapeDtypeStruct((n, steps, d), _F32),
            jax.ShapeDtypeStruct((n, hist_rows, d), _F32),
            jax.ShapeDtypeStruct((n, steps, d), _F32),
        ] + [jax.ShapeDtypeStruct(_packed_shape(w), _PACKED) for w in weights] + [
            jax.ShapeDtypeStruct((nb, 3 * d), _F32),
            jax.ShapeDtypeStruct((GROUPS, CHUNK // 2, CHUNK), _PACKED),
            jax.ShapeDtypeStruct((CHUNK, d), _F32)],
        scratch_shapes=[pltpu.VMEM(_packed_shape(w), _PACKED) for w in weights] + [
            pltpu.VMEM((STAGE_SLOTS, d, d), _F32),
            pltpu.VMEM((steps, n, d), _F32), pltpu.VMEM((hist_rows, n, d), _F32),
            pltpu.VMEM((steps, n, d), _F32), pltpu.VMEM((hist_rows, n, d), _F32),
            pltpu.VMEM((steps, n, d), _F32),
            pltpu.SMEM((w_s.shape[0], ws_rows, w_s.shape[2]), _F32),
            pltpu.SemaphoreType.DMA((STAGE_SLOTS,)), pltpu.SemaphoreType.DMA((n_exports,)),
            pltpu.SemaphoreType.DMA((steps + hist_rows,)), pltpu.SemaphoreType.DMA((n_results,)),
            pltpu.SemaphoreType.DMA((1,))],
        compiler_params=pltpu.CompilerParams(vmem_limit_bytes=VMEM_LIMIT_BYTES),
        name="sample_layer",
    )(b_s, w_s, c_s, c_p, b_c.reshape(1, 3 * d), alg, alb, wconv, bconv, lng, lnb,
      x, state, w_s, w_c, win, wpa, wpb, wo)
    return y, conv, v, weights_bf16, mod_p, wsm, bias


def kernel(x_prompt, x_sample, state_conv, c_prompt, c_sample, w_c, b_c, w_in, a_ln_g, a_ln_b,
           w_s, b_s, w_conv, b_conv, w_pa, w_pb, w_o, ln_g, ln_b):
    depth = w_in.shape[0]
    d = x_prompt.shape[-1]
    alpha = (2.0 * depth) ** 0.25

    xp, xs = x_prompt, x_sample
    conv_p_rows, conv_s_rows, v_rows = [], [], []
    for l in range(depth):
        row = lambda a: a.reshape(1, d)
        small = (row(a_ln_g[l]), row(a_ln_b[l]), w_conv[l].reshape(1, CONV_WIDTH * d),
                 row(b_conv[l]), row(ln_g[l]), row(ln_b[l]))
        xs, conv_s, v_s, weights_bf16, mod_p, wsm, bias = _sample_call(
            xs, state_conv[l], c_sample, c_prompt, w_c[l], b_c[l],
            w_in[l], w_pa[l], w_pb[l], w_o[l], w_s[l], b_s[l], *small, alpha)
        xp, conv_p = _prompt_call(xp, mod_p, *weights_bf16, wsm, bias, *small, alpha)
        conv_p_rows.append(conv_p)
        conv_s_rows.append(conv_s)
        v_rows.append(v_s)
    stack = (lambda rows: rows[0][None]) if depth == 1 else jnp.stack
    return (xp, xs, stack(conv_p_rows), stack(conv_s_rows), stack(v_rows))
```

```python
import functools

import jax
import jax.numpy as jnp
from jax import lax
from jax.experimental import pallas as pl
from jax.experimental.pallas import tpu as pltpu

CHUNK = 128
GROUPS = 8
CONV_WIDTH = 3
LN_EPS = 1e-5
ROW_TILE = 512
SUB_TILE = 256
STAGE_SLOTS = 4
VMEM_LIMIT_BYTES = 62 * 1024 * 1024

_F32 = jnp.float32
_BF16 = jnp.bfloat16


def _dot(a, b):
    return jnp.dot(a, b, preferred_element_type=_F32)


def _layer_norm(x, gain, bias):
    mu = jnp.mean(x, axis=-1, keepdims=True)
    xc = x - mu
    var = jnp.mean(xc * xc, axis=-1, keepdims=True)
    return xc * lax.rsqrt(var + LN_EPS) * gain + bias


def _silu(x):
    return x * jax.nn.sigmoid(x)


_PACKED = jnp.uint32


def _packed_shape(w):
    return (w.shape[0] // 2, w.shape[1])


def _as_bf16(words):
    return pltpu.bitcast(words, _BF16)


def _prompt_kernel(x_ref, mod_ref, win_ref, wpa_ref, wpb_ref, wo_ref, wsm_ref, bias_ref,
                   alg_ref, alb_ref, wconv_ref, bconv_ref, lng_ref, lnb_ref,
                   y_ref, conv_ref, z_scr, *, alpha, steps_per_seq):
    tm, d = x_ref.shape
    sub = SUB_TILE
    gd = d // GROUPS
    i = pl.program_id(0)

    mod = mod_ref[pl.ds(i // steps_per_seq, 1), :]
    shift, scale, gate = mod[:, 0:d], mod[:, d:2 * d], mod[:, 2 * d:3 * d]
    bias = bias_ref[...]
    carried = jnp.where(i % steps_per_seq == 0, 0.0, z_scr[...])
    wconv = [wconv_ref[:, k * d:(k + 1) * d] for k in range(CONV_WIDTH)]
    starts = range(0, tm, sub)
    tails = []

    def project(r0):
        h = (x_ref[r0:r0 + sub, :] * (1.0 + scale) + shift).astype(_BF16)
        return [_dot(h, _as_bf16(win_ref[:, k * d:(k + 1) * d]))
                for k in range(win_ref.shape[1] // d)]

    def mix(r0, p):
        u, v, z_a, b_g, c_g, h_b, z_b, g_a, g_b = p
        vb = _layer_norm(v, alg_ref[...], alb_ref[...]).astype(_BF16)
        s_cols = []
        for g in range(GROUPS):
            cols = slice(g * gd, (g + 1) * gd)
            pair_rows = []
            for c in range(0, sub // CHUNK, 2):
                rhs = jnp.concatenate([vb[c * CHUNK:(c + 1) * CHUNK, cols],
                                       vb[(c + 1) * CHUNK:(c + 2) * CHUNK, cols]], axis=1)
                res = _dot(_as_bf16(wsm_ref[g]), rhs)
                pair_rows.append(res[:, :gd] + bias[:, cols])
                pair_rows.append(res[:, gd:] + bias[:, cols])
            s_cols.append(jnp.concatenate(pair_rows, axis=0))
        s = jnp.concatenate(s_cols, axis=1)
        out_a = (u * s * _silu(z_a)).astype(_BF16)
        z = c_g * h_b
        hist = carried if not tails else tails[-1]
        tails.append(z[sub - 8:sub, :])
        top_rows = lax.broadcasted_iota(jnp.int32, (8, d), 0)

        def delayed(k):
            rolled = pltpu.roll(z, k, axis=0)
            top = jnp.where(top_rows < k, pltpu.roll(hist, k, axis=0), rolled[0:8, :])
            return jnp.concatenate([top, rolled[8:, :]], axis=0)

        conv = bconv_ref[...] + wconv[0] * delayed(2) + wconv[1] * delayed(1) + wconv[2] * z
        out_b = (b_g * conv * _silu(z_b)).astype(_BF16)
        return out_a, out_b, jax.nn.sigmoid(g_a), jax.nn.sigmoid(g_b)

    def merge(m):
        out_a, out_b, sg_a, sg_b = m
        return (sg_a * _dot(out_a, _as_bf16(wpa_ref[...]))
                + sg_b * _dot(out_b, _as_bf16(wpb_ref[...]))).astype(_BF16)

    projected = [project(r0) for r0 in starts]
    mixed = [mix(r0, p) for r0, p in zip(starts, projected)]
    merged = [merge(m) for m in mixed]
    for r0, mg in zip(starts, merged):
        x = x_ref[r0:r0 + sub, :]
        y_ref[r0:r0 + sub, :] = _layer_norm(
            alpha * x + (1.0 + gate) * _dot(mg, _as_bf16(wo_ref[...])), lng_ref[...], lnb_ref[...])

    z_scr[...] = tails[-1]
    conv_ref[0] = tails[-1][8 - (CONV_WIDTH - 1):8, :]


def _resident(shape):
    nd = len(shape)
    return pl.BlockSpec(shape, lambda i: (0,) * nd, pipeline_mode=pl.Buffered(1))


def _prompt_call(x, mod, win, wpa, wpb, wo, wsm, bias, alg, alb, wconv, bconv, lng, lnb, alpha):
    nb, length, d = x.shape
    tm = ROW_TILE
    assert length % tm == 0 and tm % SUB_TILE == 0 and SUB_TILE % (2 * CHUNK) == 0
    assert d % GROUPS == 0 and d // GROUPS == CHUNK
    steps_per_seq = length // tm
    small = [alg, alb, wconv, bconv, lng, lnb]
    y, conv = pl.pallas_call(
        functools.partial(_prompt_kernel, alpha=alpha, steps_per_seq=steps_per_seq),
        grid=(nb * steps_per_seq,),
        in_specs=[
            pl.BlockSpec((tm, d), lambda i: (i, 0)),
            _resident(mod.shape),
            _resident(win.shape), _resident(wpa.shape), _resident(wpb.shape), _resident(wo.shape),
            _resident(wsm.shape), _resident(bias.shape),
        ] + [_resident(a.shape) for a in small],
        out_specs=[
            pl.BlockSpec((tm, d), lambda i: (i, 0)),
            pl.BlockSpec((1, CONV_WIDTH - 1, d), lambda i: (i // steps_per_seq, 0, 0)),
        ],
        out_shape=[
            jax.ShapeDtypeStruct((nb * length, d), _F32),
            jax.ShapeDtypeStruct((nb, CONV_WIDTH - 1, d), _F32),
        ],
        scratch_shapes=[pltpu.VMEM((8, d), _F32)],
        compiler_params=pltpu.CompilerParams(
            dimension_semantics=("arbitrary",),
            vmem_limit_bytes=VMEM_LIMIT_BYTES),
        name="prompt_layer",
    )(x.reshape(nb * length, d), mod, win, wpa, wpb, wo, wsm, bias, *small)
    return y.reshape(nb, length, d), conv


_PROJ_ORDER = (1, 0, 2, 4, 5, 3, 6, 7, 8)


def _sample_kernel(bs_ref, ws_ref, cs_ref, cp_ref, bc_ref,
                   alg_ref, alb_ref, wconv_ref, bconv_ref, lng_ref, lnb_ref,
                   x_hbm, st_hbm, ws_hbm, wc_hbm, win_hbm, wpa_hbm, wpb_hbm, wo_hbm,
                   y_hbm, conv_hbm, v_hbm, win_out, wpa_out, wpb_out, wo_out, modp_ref,
                   wsm_ref, bias_ref,
                   win_ref, wpa_ref, wpb_ref, wo_ref, stage, xbuf, stbuf, ybuf, cvbuf, vbuf, ws_smem,
                   sem, exp_sem, in_sem, res_sem, ws_sem, *, alpha):
    steps, n, d = xbuf.shape
    gd = d // GROUPS
    hist_rows = CONV_WIDTH - 1

    ws_copy = pltpu.make_async_copy(ws_hbm.at[:, 0:ws_smem.shape[1], :], ws_smem, ws_sem.at[0])
    ws_copy.start()

    in_copies = ([pltpu.make_async_copy(x_hbm.at[:, t, :], xbuf.at[t], in_sem.at[t])
                  for t in range(steps)]
                 + [pltpu.make_async_copy(st_hbm.at[:, k, :], stbuf.at[k], in_sem.at[steps + k])
                    for k in range(hist_rows)])
    for cp in in_copies:
        cp.start()

    blocks = ([(wc_hbm, None, None, j) for j in range(wc_hbm.shape[1] // d)]
              + [(win_hbm, win_ref, win_out, k) for k in _PROJ_ORDER]
              + [(wpa_hbm, wpa_ref, wpa_out, 0), (wpb_hbm, wpb_ref, wpb_out, 0),
                 (wo_hbm, wo_ref, wo_out, 0)])
    exports = []
    position = [0]

    def stage_copy(i):
        src, _, _, k = blocks[i]
        slot = i % STAGE_SLOTS
        return pltpu.make_async_copy(src.at[:, k * d:(k + 1) * d], stage.at[slot], sem.at[slot])

    def next_weight():
        i = position[0]
        position[0] += 1
        stage_copy(i).wait()
        if i + STAGE_SLOTS - 1 < len(blocks):
            stage_copy(i + STAGE_SLOTS - 1).start()
        w = stage[i % STAGE_SLOTS].astype(_BF16)
        _, keep, out, k = blocks[i]
        if keep is not None:
            cols = slice(k * d, (k + 1) * d)
            keep[:, cols] = pltpu.bitcast(w, _PACKED)
            cp = pltpu.make_async_copy(keep.at[:, cols], out.at[:, cols], exp_sem.at[len(exports)])
            cp.start()
            exports.append(cp)
        return w

    for i in range(STAGE_SLOTS - 1):
        stage_copy(i).start()

    row = lax.broadcasted_iota(jnp.int32, (CHUNK, CHUNK), 0)
    col = lax.broadcasted_iota(jnp.int32, (CHUNK, CHUNK), 1)
    for g in range(GROUPS):
        wsm_ref[g] = pltpu.bitcast(jnp.where(col <= row, ws_ref[g], 0.0).astype(_BF16), _PACKED)

    for t in range(CHUNK):
        for g in range(GROUPS):
            bias_ref[t:t + 1, g * gd:(g + 1) * gd] = jnp.full((1, gd), bs_ref[g, t], _F32)

    c_all = jnp.concatenate([cs_ref[...], cp_ref[...]], axis=0).astype(_BF16)
    bc = bc_ref[...]
    mods = []
    for j in range(3):
        m = _dot(c_all, next_weight()) + bc[:, j * d:(j + 1) * d]
        modp_ref[:, j * d:(j + 1) * d] = m[n:, :]
        mods.append(m[0:n, :])
    shift, scale, gate = mods

    for cp in in_copies:
        cp.wait()
    ws_copy.wait()
    h = jnp.concatenate([(xbuf[t] * (1.0 + scale) + shift).astype(_BF16) for t in range(steps)],
                        axis=0)

    def rows(a, t):
        return a[t * n:(t + 1) * n, :]

    def proj():
        return _dot(h, next_weight())

    results = []

    def send(buf, t, dst):
        cp = pltpu.make_async_copy(buf.at[t], dst.at[:, t, :], res_sem.at[len(results)])
        cp.start()
        results.append(cp)

    vn = _layer_norm(proj(), alg_ref[...], alb_ref[...])
    s_rows = []
    for t in range(steps):
        vbuf[t] = rows(vn, t)
        send(vbuf, t, v_hbm)
        s_cols = []
        for g in range(GROUPS):
            cols = slice(g * gd, (g + 1) * gd)
            acc = jnp.full((n, gd), bs_ref[g, t], _F32)
            for jj in range(t + 1):
                acc = acc + ws_smem[g, t, jj] * rows(vn, jj)[:, cols]
            s_cols.append(acc)
        s_rows.append(jnp.concatenate(s_cols, axis=1))
    s = jnp.concatenate(s_rows, axis=0)
    u = proj()
    out_a = (u * s * _silu(proj())).astype(_BF16)

    z = proj()
    z = z * proj()
    hist = [stbuf[k] for k in range(hist_rows)] + [rows(z, t) for t in range(steps)]
    for k in range(hist_rows):
        cvbuf[k] = hist[steps + k]
        send(cvbuf, k, conv_hbm)
    wconv = [wconv_ref[:, k * d:(k + 1) * d] for k in range(CONV_WIDTH)]
    conv = jnp.concatenate(
        [bconv_ref[...] + sum(wconv[k] * hist[t + k] for k in range(CONV_WIDTH))
         for t in range(steps)], axis=0)
    b_g = proj()
    out_b = (b_g * conv * _silu(proj())).astype(_BF16)

    sg_a = jax.nn.sigmoid(proj())
    sg_b = jax.nn.sigmoid(proj())
    pa = _dot(out_a, next_weight())
    pb = _dot(out_b, next_weight())
    merged = (sg_a * pa + sg_b * pb).astype(_BF16)
    y = _dot(merged, next_weight())
    for t in range(steps):
        ybuf[t] = _layer_norm(alpha * xbuf[t] + (1.0 + gate) * rows(y, t), lng_ref[...], lnb_ref[...])
        send(ybuf, t, y_hbm)

    for cp in exports + results:
        cp.wait()


def _sample_call(x, state, c_s, c_p, w_c, b_c, win, wpa, wpb, wo, w_s, b_s,
                 alg, alb, wconv, bconv, lng, lnb, alpha):
    n, steps, d = x.shape
    nb = c_p.shape[0]
    hist_rows = CONV_WIDTH - 1
    assert hist_rows <= steps <= CHUNK and w_c.shape == (d, 3 * d)
    assert w_s.shape == (GROUPS, CHUNK, CHUNK) and d == GROUPS * CHUNK
    vmem = pl.BlockSpec(memory_space=pltpu.VMEM)
    smem = pl.BlockSpec(memory_space=pltpu.SMEM)
    hbm = pl.BlockSpec(memory_space=pl.ANY)
    weights = (win, wpa, wpb, wo)
    n_exports = sum(w.shape[1] // d for w in weights)
    n_results = 2 * steps + hist_rows
    ws_rows = -(-steps // 8) * 8
    y, conv, v, *weights_bf16, mod_p, wsm, bias = pl.pallas_call(
        functools.partial(_sample_kernel, alpha=alpha),
        in_specs=[smem] + [vmem] * 10 + [hbm] * 8,
        out_specs=[hbm] * (3 + len(weights)) + [vmem] * 3,
        out_shape=[
            jax.ShapeDtypeStruct((n, steps, d), _F32),
            jax.ShapeDtypeStruct((n, hist_rows, d), _F32),
            jax.ShapeDtypeStruct((n, steps, d), _F32),
        ] + [jax.ShapeDtypeStruct(_packed_shape(w), _PACKED) for w in weights] + [
            jax.ShapeDtypeStruct((nb, 3 * d), _F32),
            jax.ShapeDtypeStruct((GROUPS, CHUNK // 2, CHUNK), _PACKED),
            jax.ShapeDtypeStruct((CHUNK, d), _F32)],
        scratch_shapes=[pltpu.VMEM(_packed_shape(w), _PACKED) for w in weights] + [
            pltpu.VMEM((STAGE_SLOTS, d, d), _F32),
            pltpu.VMEM((steps, n, d), _F32), pltpu.VMEM((hist_rows, n, d), _F32),
            pltpu.VMEM((steps, n, d), _F32), pltpu.VMEM((hist_rows, n, d), _F32),
            pltpu.VMEM((steps, n, d), _F32),
            pltpu.SMEM((w_s.shape[0], ws_rows, w_s.shape[2]), _F32),
            pltpu.SemaphoreType.DMA((STAGE_SLOTS,)), pltpu.SemaphoreType.DMA((n_exports,)),
            pltpu.SemaphoreType.DMA((steps + hist_rows,)), pltpu.SemaphoreType.DMA((n_results,)),
            pltpu.SemaphoreType.DMA((1,))],
        compiler_params=pltpu.CompilerParams(vmem_limit_bytes=VMEM_LIMIT_BYTES),
        name="sample_layer",
    )(b_s, w_s, c_s, c_p, b_c.reshape(1, 3 * d), alg, alb, wconv, bconv, lng, lnb,
      x, state, w_s, w_c, win, wpa, wpb, wo)
    return y, conv, v, weights_bf16, mod_p, wsm, bias


def kernel(x_prompt, x_sample, state_conv, c_prompt, c_sample, w_c, b_c, w_in, a_ln_g, a_ln_b,
           w_s, b_s, w_conv, b_conv, w_pa, w_pb, w_o, ln_g, ln_b):
    depth = w_in.shape[0]
    d = x_prompt.shape[-1]
    alpha = (2.0 * depth) ** 0.25

    xp, xs = x_prompt, x_sample
    conv_p_rows, conv_s_rows, v_rows = [], [], []
    for l in range(depth):
        row = lambda a: a.reshape(1, d)
        small = (row(a_ln_g[l]), row(a_ln_b[l]), w_conv[l].reshape(1, CONV_WIDTH * d),
                 row(b_conv[l]), row(ln_g[l]), row(ln_b[l]))
        xs, conv_s, v_s, weights_bf16, mod_p, wsm, bias = _sample_call(
            xs, state_conv[l], c_sample, c_prompt, w_c[l], b_c[l],
            w_in[l], w_pa[l], w_pb[l], w_o[l], w_s[l], b_s[l], *small, alpha)
        xp, conv_p = _prompt_call(xp, mod_p, *weights_bf16, wsm, bias, *small, alpha)
        conv_p_rows.append(conv_p)
        conv_s_rows.append(conv_s)
        v_rows.append(v_s)
    stack = (lambda rows: rows[0][None]) if depth == 1 else jnp.stack
    return (xp, xs, stack(conv_p_rows), stack(conv_s_rows), stack(v_rows))
```

```python
import functools

import jax
import jax.numpy as jnp
from jax import lax
from jax.experimental import pallas as pl
from jax.experimental.pallas import tpu as pltpu

CHUNK = 128
GROUPS = 8
CONV_WIDTH = 3
LN_EPS = 1e-5
ROW_TILE = 512
SUB_TILE = 256
STAGE_SLOTS = 4
VMEM_LIMIT_BYTES = 62 * 1024 * 1024

_F32 = jnp.float32
_BF16 = jnp.bfloat16


def _dot(a, b):
    return jnp.dot(a, b, preferred_element_type=_F32)


def _layer_norm(x, gain, bias):
    mu = jnp.mean(x, axis=-1, keepdims=True)
    xc = x - mu
    var = jnp.mean(xc * xc, axis=-1, keepdims=True)
    return xc * lax.rsqrt(var + LN_EPS) * gain + bias


def _silu(x):
    return x * jax.nn.sigmoid(x)


_PACKED = jnp.uint32


def _packed_shape(w):
    return (w.shape[0] // 2, w.shape[1])


def _as_bf16(words):
    return pltpu.bitcast(words, _BF16)


def _prompt_kernel(x_ref, mod_ref, win_ref, wpa_ref, wpb_ref, wo_ref, wsm_ref, bias_ref,
                   alg_ref, alb_ref, wconv_ref, bconv_ref, lng_ref, lnb_ref,
                   y_ref, conv_ref, z_scr, *, alpha, steps_per_seq):
    tm, d = x_ref.shape
    sub = SUB_TILE
    gd = d // GROUPS
    i = pl.program_id(0)

    mod = mod_ref[pl.ds(i // steps_per_seq, 1), :]
    shift, scale, gate = mod[:, 0:d], mod[:, d:2 * d], mod[:, 2 * d:3 * d]
    bias = bias_ref[...]
    carried = jnp.where(i % steps_per_seq == 0, 0.0, z_scr[...])
    wconv = [wconv_ref[:, k * d:(k + 1) * d] for k in range(CONV_WIDTH)]
    starts = range(0, tm, sub)
    tails = []

    def modulated(r0):
        return (x_ref[r0:r0 + sub, :] * (1.0 + scale) + shift).astype(_BF16)

    def project(h, ks):
        return [_dot(h, _as_bf16(win_ref[:, k * d:(k + 1) * d])) for k in ks]

    def mixer_a(u, v, z_a):
        vb = _layer_norm(v, alg_ref[...], alb_ref[...]).astype(_BF16)
        s_cols = []
        for g in range(GROUPS):
            cols = slice(g * gd, (g + 1) * gd)
            pair_rows = []
            for c in range(0, sub // CHUNK, 2):
                rhs = jnp.concatenate([vb[c * CHUNK:(c + 1) * CHUNK, cols],
                                       vb[(c + 1) * CHUNK:(c + 2) * CHUNK, cols]], axis=1)
                res = _dot(_as_bf16(wsm_ref[g]), rhs)
                pair_rows.append(res[:, :gd] + bias[:, cols])
                pair_rows.append(res[:, gd:] + bias[:, cols])
            s_cols.append(jnp.concatenate(pair_rows, axis=0))
        s = jnp.concatenate(s_cols, axis=1)
        return (u * s * _silu(z_a)).astype(_BF16)

    def mixer_b(b_g, c_g, h_b, z_b):
        z = c_g * h_b
        hist = carried if not tails else tails[-1]
        tails.append(z[sub - 8:sub, :])
        top_rows = lax.broadcasted_iota(jnp.int32, (8, d), 0)

        def delayed(k):
            rolled = pltpu.roll(z, k, axis=0)
            top = jnp.where(top_rows < k, pltpu.roll(hist, k, axis=0), rolled[0:8, :])
            return jnp.concatenate([top, rolled[8:, :]], axis=0)

        conv = bconv_ref[...] + wconv[0] * delayed(2) + wconv[1] * delayed(1) + wconv[2] * z
        return (b_g * conv * _silu(z_b)).astype(_BF16)

    def merge(out_a, out_b, g_a, g_b):
        return (jax.nn.sigmoid(g_a) * _dot(out_a, _as_bf16(wpa_ref[...]))
                + jax.nn.sigmoid(g_b) * _dot(out_b, _as_bf16(wpb_ref[...]))).astype(_BF16)

    def finish(r0, mg):
        x = x_ref[r0:r0 + sub, :]
        y_ref[r0:r0 + sub, :] = _layer_norm(
            alpha * x + (1.0 + gate) * _dot(mg, _as_bf16(wo_ref[...])), lng_ref[...], lnb_ref[...])

    r_a, r_b = starts
    u, v, z_a, b_g, c_g, h_b, z_b, g_a, g_b = project(modulated(r_a), range(9))
    hb = modulated(r_b)
    u2, v2, z_a2 = project(hb, range(0, 3))
    out_a = mixer_a(u, v, z_a)
    b_g2, c_g2, h_b2, z_b2 = project(hb, range(3, 7))
    out_b = mixer_b(b_g, c_g, h_b, z_b)
    g_a2, g_b2 = project(hb, range(7, 9))
    out_a2 = mixer_a(u2, v2, z_a2)
    mg = merge(out_a, out_b, g_a, g_b)
    out_b2 = mixer_b(b_g2, c_g2, h_b2, z_b2)
    finish(r_a, mg)
    finish(r_b, merge(out_a2, out_b2, g_a2, g_b2))

    z_scr[...] = tails[-1]
    conv_ref[0] = tails[-1][8 - (CONV_WIDTH - 1):8, :]


def _resident(shape):
    nd = len(shape)
    return pl.BlockSpec(shape, lambda i: (0,) * nd, pipeline_mode=pl.Buffered(1))


def _prompt_call(x, mod, win, wpa, wpb, wo, wsm, bias, alg, alb, wconv, bconv, lng, lnb, alpha):
    nb, length, d = x.shape
    tm = ROW_TILE
    assert length % tm == 0 and tm == 2 * SUB_TILE and SUB_TILE % (2 * CHUNK) == 0
    assert d % GROUPS == 0 and d // GROUPS == CHUNK
    steps_per_seq = length // tm
    small = [alg, alb, wconv, bconv, lng, lnb]
    y, conv = pl.pallas_call(
        functools.partial(_prompt_kernel, alpha=alpha, steps_per_seq=steps_per_seq),
        grid=(nb * steps_per_seq,),
        in_specs=[
            pl.BlockSpec((tm, d), lambda i: (i, 0)),
            _resident(mod.shape),
            _resident(win.shape), _resident(wpa.shape), _resident(wpb.shape), _resident(wo.shape),
            _resident(wsm.shape), _resident(bias.shape),
        ] + [_resident(a.shape) for a in small],
        out_specs=[
            pl.BlockSpec((tm, d), lambda i: (i, 0)),
            pl.BlockSpec((1, CONV_WIDTH - 1, d), lambda i: (i // steps_per_seq, 0, 0)),
        ],
        out_shape=[
            jax.ShapeDtypeStruct((nb * length, d), _F32),
            jax.ShapeDtypeStruct((nb, CONV_WIDTH - 1, d), _F32),
        ],
        scratch_shapes=[pltpu.VMEM((8, d), _F32)],
        compiler_params=pltpu.CompilerParams(
            dimension_semantics=("arbitrary",),
            vmem_limit_bytes=VMEM_LIMIT_BYTES),
        name="prompt_layer",
    )(x.reshape(nb * length, d), mod, win, wpa, wpb, wo, wsm, bias, *small)
    return y.reshape(nb, length, d), conv


_PROJ_ORDER = (1, 0, 2, 4, 5, 3, 6, 7, 8)


def _sample_kernel(bs_ref, ws_ref, cs_ref, cp_ref, bc_ref,
                   alg_ref, alb_ref, wconv_ref, bconv_ref, lng_ref, lnb_ref,
                   x_hbm, st_hbm, ws_hbm, wc_hbm, win_hbm, wpa_hbm, wpb_hbm, wo_hbm,
                   y_hbm, conv_hbm, v_hbm, win_out, wpa_out, wpb_out, wo_out, modp_ref,
                   wsm_ref, bias_ref,
                   win_ref, wpa_ref, wpb_ref, wo_ref, stage, xbuf, stbuf, ybuf, cvbuf, vbuf, ws_smem,
                   sem, exp_sem, in_sem, res_sem, ws_sem, *, alpha):
    steps, n, d = xbuf.shape
    gd = d // GROUPS
    hist_rows = CONV_WIDTH - 1

    ws_copy = pltpu.make_async_copy(ws_hbm.at[:, 0:ws_smem.shape[1], :], ws_smem, ws_sem.at[0])
    ws_copy.start()

    in_copies = ([pltpu.make_async_copy(x_hbm.at[:, t, :], xbuf.at[t], in_sem.at[t])
                  for t in range(steps)]
                 + [pltpu.make_async_copy(st_hbm.at[:, k, :], stbuf.at[k], in_sem.at[steps + k])
                    for k in range(hist_rows)])
    for cp in in_copies:
        cp.start()

    blocks = ([(wc_hbm, None, None, j) for j in range(wc_hbm.shape[1] // d)]
              + [(win_hbm, win_ref, win_out, k) for k in _PROJ_ORDER]
              + [(wpa_hbm, wpa_ref, wpa_out, 0), (wpb_hbm, wpb_ref, wpb_out, 0),
                 (wo_hbm, wo_ref, wo_out, 0)])
    exports = []
    position = [0]

    def stage_copy(i):
        src, _, _, k = blocks[i]
        slot = i % STAGE_SLOTS
        return pltpu.make_async_copy(src.at[:, k * d:(k + 1) * d], stage.at[slot], sem.at[slot])

    def next_weight():
        i = position[0]
        position[0] += 1
        stage_copy(i).wait()
        if i + STAGE_SLOTS - 1 < len(blocks):
            stage_copy(i + STAGE_SLOTS - 1).start()
        w = stage[i % STAGE_SLOTS].astype(_BF16)
        _, keep, out, k = blocks[i]
        if keep is not None:
            cols = slice(k * d, (k + 1) * d)
            keep[:, cols] = pltpu.bitcast(w, _PACKED)
            cp = pltpu.make_async_copy(keep.at[:, cols], out.at[:, cols], exp_sem.at[len(exports)])
            cp.start()
            exports.append(cp)
        return w

    for i in range(STAGE_SLOTS - 1):
        stage_copy(i).start()

    c_all = jnp.concatenate([cs_ref[...], cp_ref[...]], axis=0).astype(_BF16)
    bc = bc_ref[...]
    mods = []
    for j in range(3):
        m = _dot(c_all, next_weight()) + bc[:, j * d:(j + 1) * d]
        modp_ref[:, j * d:(j + 1) * d] = m[n:, :]
        mods.append(m[0:n, :])
    shift, scale, gate = mods

    row = lax.broadcasted_iota(jnp.int32, (CHUNK, CHUNK), 0)
    col = lax.broadcasted_iota(jnp.int32, (CHUNK, CHUNK), 1)
    for g in range(GROUPS):
        wsm_ref[g] = pltpu.bitcast(jnp.where(col <= row, ws_ref[g], 0.0).astype(_BF16), _PACKED)

    for t in range(CHUNK):
        for g in range(GROUPS):
            bias_ref[t:t + 1, g * gd:(g + 1) * gd] = jnp.full((1, gd), bs_ref[g, t], _F32)

    for cp in in_copies:
        cp.wait()
    ws_copy.wait()
    h = jnp.concatenate([(xbuf[t] * (1.0 + scale) + shift).astype(_BF16) for t in range(steps)],
                        axis=0)

    def rows(a, t):
        return a[t * n:(t + 1) * n, :]

    def proj():
        return _dot(h, next_weight())

    results = []

    def send(buf, t, dst):
        cp = pltpu.make_async_copy(buf.at[t], dst.at[:, t, :], res_sem.at[len(results)])
        cp.start()
        results.append(cp)

    vn = _layer_norm(proj(), alg_ref[...], alb_ref[...])
    s_rows = []
    for t in range(steps):
        vbuf[t] = rows(vn, t)
        send(vbuf, t, v_hbm)
        s_cols = []
        for g in range(GROUPS):
            cols = slice(g * gd, (g + 1) * gd)
            acc = jnp.full((n, gd), bs_ref[g, t], _F32)
            for jj in range(t + 1):
                acc = acc + ws_smem[g, t, jj] * rows(vn, jj)[:, cols]
            s_cols.append(acc)
        s_rows.append(jnp.concatenate(s_cols, axis=1))
    s = jnp.concatenate(s_rows, axis=0)
    u = proj()
    out_a = (u * s * _silu(proj())).astype(_BF16)

    z = proj()
    z = z * proj()
    hist = [stbuf[k] for k in range(hist_rows)] + [rows(z, t) for t in range(steps)]
    for k in range(hist_rows):
        cvbuf[k] = hist[steps + k]
        send(cvbuf, k, conv_hbm)
    wconv = [wconv_ref[:, k * d:(k + 1) * d] for k in range(CONV_WIDTH)]
    conv = jnp.concatenate(
        [bconv_ref[...] + sum(wconv[k] * hist[t + k] for k in range(CONV_WIDTH))
         for t in range(steps)], axis=0)
    b_g = proj()
    out_b = (b_g * conv * _silu(proj())).astype(_BF16)

    sg_a = jax.nn.sigmoid(proj())
    sg_b = jax.nn.sigmoid(proj())
    pa = _dot(out_a, next_weight())
    pb = _dot(out_b, next_weight())
    merged = (sg_a * pa + sg_b * pb).astype(_BF16)
    y = _dot(merged, next_weight())
    for t in range(steps):
        ybuf[t] = _layer_norm(alpha * xbuf[t] + (1.0 + gate) * rows(y, t), lng_ref[...], lnb_ref[...])
        send(ybuf, t, y_hbm)

    for cp in exports + results:
        cp.wait()


def _sample_call(x, state, c_s, c_p, w_c, b_c, win, wpa, wpb, wo, w_s, b_s,
                 alg, alb, wconv, bconv, lng, lnb, alpha):
    n, steps, d = x.shape
    nb = c_p.shape[0]
    hist_rows = CONV_WIDTH - 1
    assert hist_rows <= steps <= CHUNK and w_c.shape == (d, 3 * d)
    assert w_s.shape == (GROUPS, CHUNK, CHUNK) and d == GROUPS * CHUNK
    vmem = pl.BlockSpec(memory_space=pltpu.VMEM)
    smem = pl.BlockSpec(memory_space=pltpu.SMEM)
    hbm = pl.BlockSpec(memory_space=pl.ANY)
    weights = (win, wpa, wpb, wo)
    n_exports = sum(w.shape[1] // d for w in weights)
    n_results = 2 * steps + hist_rows
    ws_rows = -(-steps // 8) * 8
    y, conv, v, *weights_bf16, mod_p, wsm, bias = pl.pallas_call(
        functools.partial(_sample_kernel, alpha=alpha),
        in_specs=[smem] + [vmem] * 10 + [hbm] * 8,
        out_specs=[hbm] * (3 + len(weights)) + [vmem] * 3,
        out_shape=[
            jax.ShapeDtypeStruct((n, steps, d), _F32),
            jax.ShapeDtypeStruct((n, hist_rows, d), _F32),
            jax.ShapeDtypeStruct((n, steps, d), _F32),
        ] + [jax.ShapeDtypeStruct(_packed_shape(w), _PACKED) for w in weights] + [
            jax.ShapeDtypeStruct((nb, 3 * d), _F32),
            jax.ShapeDtypeStruct((GROUPS, CHUNK // 2, CHUNK), _PACKED),
            jax.ShapeDtypeStruct((CHUNK, d), _F32)],
        scratch_shapes=[pltpu.VMEM(_packed_shape(w), _PACKED) for w in weights] + [
            pltpu.VMEM((STAGE_SLOTS, d, d), _F32),
            pltpu.VMEM((steps, n, d), _F32), pltpu.VMEM((hist_rows, n, d), _F32),
            pltpu.VMEM((steps, n, d), _F32), pltpu.VMEM((hist_rows, n, d), _F32),
            pltpu.VMEM((steps, n, d), _F32),
            pltpu.SMEM((w_s.shape[0], ws_rows, w_s.shape[2]), _F32),
            pltpu.SemaphoreType.DMA((STAGE_SLOTS,)), pltpu.SemaphoreType.DMA((n_exports,)),
            pltpu.SemaphoreType.DMA((steps + hist_rows,)), pltpu.SemaphoreType.DMA((n_results,)),
            pltpu.SemaphoreType.DMA((1,))],
        compiler_params=pltpu.CompilerParams(vmem_limit_bytes=VMEM_LIMIT_BYTES),
        name="sample_layer",
    )(b_s, w_s, c_s, c_p, b_c.reshape(1, 3 * d), alg, alb, wconv, bconv, lng, lnb,
      x, state, w_s, w_c, win, wpa, wpb, wo)
    return y, conv, v, weights_bf16, mod_p, wsm, bias


def kernel(x_prompt, x_sample, state_conv, c_prompt, c_sample, w_c, b_c, w_in, a_ln_g, a_ln_b,
           w_s, b_s, w_conv, b_conv, w_pa, w_pb, w_o, ln_g, ln_b):
    depth = w_in.shape[0]
    d = x_prompt.shape[-1]
    alpha = (2.0 * depth) ** 0.25

    xp, xs = x_prompt, x_sample
    conv_p_rows, conv_s_rows, v_rows = [], [], []
    for l in range(depth):
        row = lambda a: a.reshape(1, d)
        small = (row(a_ln_g[l]), row(a_ln_b[l]), w_conv[l].reshape(1, CONV_WIDTH * d),
                 row(b_conv[l]), row(ln_g[l]), row(ln_b[l]))
        xs, conv_s, v_s, weights_bf16, mod_p, wsm, bias = _sample_call(
            xs, state_conv[l], c_sample, c_prompt, w_c[l], b_c[l],
            w_in[l], w_pa[l], w_pb[l], w_o[l], w_s[l], b_s[l], *small, alpha)
        xp, conv_p = _prompt_call(xp, mod_p, *weights_bf16, wsm, bias, *small, alpha)
        conv_p_rows.append(conv_p)
        conv_s_rows.append(conv_s)
        v_rows.append(v_s)
    stack = (lambda rows: rows[0][None]) if depth == 1 else jnp.stack
    return (xp, xs, stack(conv_p_rows), stack(conv_s_rows), stack(v_rows))
```

```python
import functools

import jax
import jax.numpy as jnp
from jax import lax
from jax.experimental import pallas as pl
from jax.experimental.pallas import tpu as pltpu

CHUNK = 128
GROUPS = 8
CONV_WIDTH = 3
LN_EPS = 1e-5
ROW_TILE = 512
SUB_TILE = 256
STAGE_SLOTS = 4
VMEM_LIMIT_BYTES = 62 * 1024 * 1024

_F32 = jnp.float32
_BF16 = jnp.bfloat16


def _dot(a, b):
    return jnp.dot(a, b, preferred_element_type=_F32)


def _layer_norm(x, gain, bias):
    mu = jnp.mean(x, axis=-1, keepdims=True)
    xc = x - mu
    var = jnp.mean(xc * xc, axis=-1, keepdims=True)
    return xc * lax.rsqrt(var + LN_EPS) * gain + bias


def _silu(x):
    return x * jax.nn.sigmoid(x)


_PACKED = jnp.uint32


def _packed_shape(w):
    return (w.shape[0] // 2, w.shape[1])


def _as_bf16(words):
    return pltpu.bitcast(words, _BF16)


def _prompt_kernel(x_ref, mod_ref, win_ref, wpa_ref, wpb_ref, wo_ref, wsm_ref, bias_ref,
                   alg_ref, alb_ref, wconv_ref, bconv_ref, lng_ref, lnb_ref,
                   y_ref, conv_ref, z_scr, *, alpha, steps_per_seq):
    tm, d = x_ref.shape
    sub = SUB_TILE
    gd = d // GROUPS
    i = pl.program_id(0)

    mod = mod_ref[pl.ds(i // steps_per_seq, 1), :]
    shift, scale, gate = mod[:, 0:d], mod[:, d:2 * d], mod[:, 2 * d:3 * d]
    bias = bias_ref[...]
    carried = jnp.where(i % steps_per_seq == 0, 0.0, z_scr[...])
    wconv = [wconv_ref[:, k * d:(k + 1) * d] for k in range(CONV_WIDTH)]
    starts = range(0, tm, sub)
    tails = []

    def modulated(r0):
        return (x_ref[r0:r0 + sub, :] * (1.0 + scale) + shift).astype(_BF16)

    def project(h, ks):
        return [_dot(h, _as_bf16(win_ref[:, k * d:(k + 1) * d])) for k in ks]

    def mixer_a(u, v, z_a):
        vb = _layer_norm(v, alg_ref[...], alb_ref[...]).astype(_BF16)
        s_cols = []
        for g in range(GROUPS):
            cols = slice(g * gd, (g + 1) * gd)
            pair_rows = []
            for c in range(0, sub // CHUNK, 2):
                rhs = jnp.concatenate([vb[c * CHUNK:(c + 1) * CHUNK, cols],
                                       vb[(c + 1) * CHUNK:(c + 2) * CHUNK, cols]], axis=1)
                res = _dot(_as_bf16(wsm_ref[g]), rhs)
                pair_rows.append(res[:, :gd] + bias[:, cols])
                pair_rows.append(res[:, gd:] + bias[:, cols])
            s_cols.append(jnp.concatenate(pair_rows, axis=0))
        s = jnp.concatenate(s_cols, axis=1)
        return (u * s * _silu(z_a)).astype(_BF16)

    def mixer_b(b_g, c_g, h_b, z_b):
        z = c_g * h_b
        hist = carried if not tails else tails[-1]
        tails.append(z[sub - 8:sub, :])
        top_rows = lax.broadcasted_iota(jnp.int32, (8, d), 0)

        def delayed(k):
            rolled = pltpu.roll(z, k, axis=0)
            top = jnp.where(top_rows < k, pltpu.roll(hist, k, axis=0), rolled[0:8, :])
            return jnp.concatenate([top, rolled[8:, :]], axis=0)

        conv = bconv_ref[...] + wconv[0] * delayed(2) + wconv[1] * delayed(1) + wconv[2] * z
        return (b_g * conv * _silu(z_b)).astype(_BF16)

    def merge(out_a, out_b, g_a, g_b):
        return (jax.nn.sigmoid(g_a) * _dot(out_a, _as_bf16(wpa_ref[...]))
                + jax.nn.sigmoid(g_b) * _dot(out_b, _as_bf16(wpb_ref[...]))).astype(_BF16)

    def finish(r0, mg):
        x = x_ref[r0:r0 + sub, :]
        y_ref[r0:r0 + sub, :] = _layer_norm(
            alpha * x + (1.0 + gate) * _dot(mg, _as_bf16(wo_ref[...])), lng_ref[...], lnb_ref[...])

    r_a, r_b = starts
    u, v, z_a, b_g, c_g, h_b, z_b, g_a, g_b = project(modulated(r_a), range(9))
    hb = modulated(r_b)
    u2, v2, z_a2 = project(hb, range(0, 3))
    out_a = mixer_a(u, v, z_a)
    b_g2, c_g2, h_b2, z_b2 = project(hb, range(3, 7))
    out_b = mixer_b(b_g, c_g, h_b, z_b)
    g_a2, g_b2 = project(hb, range(7, 9))
    out_a2 = mixer_a(u2, v2, z_a2)
    mg = merge(out_a, out_b, g_a, g_b)
    out_b2 = mixer_b(b_g2, c_g2, h_b2, z_b2)
    finish(r_a, mg)
    finish(r_b, merge(out_a2, out_b2, g_a2, g_b2))

    z_scr[...] = tails[-1]
    conv_ref[0] = tails[-1][8 - (CONV_WIDTH - 1):8, :]


def _resident(shape):
    nd = len(shape)
    return pl.BlockSpec(shape, lambda i: (0,) * nd, pipeline_mode=pl.Buffered(1))


def _prompt_call(x, mod, win, wpa, wpb, wo, wsm, bias, alg, alb, wconv, bconv, lng, lnb, alpha):
    nb, length, d = x.shape
    tm = ROW_TILE
    assert length % tm == 0 and tm == 2 * SUB_TILE and SUB_TILE % (2 * CHUNK) == 0
    assert d % GROUPS == 0 and d // GROUPS == CHUNK
    steps_per_seq = length // tm
    small = [alg, alb, wconv, bconv, lng, lnb]
    y, conv = pl.pallas_call(
        functools.partial(_prompt_kernel, alpha=alpha, steps_per_seq=steps_per_seq),
        grid=(nb * steps_per_seq,),
        in_specs=[
            pl.BlockSpec((tm, d), lambda i: (i, 0)),
            _resident(mod.shape),
            _resident(win.shape), _resident(wpa.shape), _resident(wpb.shape), _resident(wo.shape),
            _resident(wsm.shape), _resident(bias.shape),
        ] + [_resident(a.shape) for a in small],
        out_specs=[
            pl.BlockSpec((tm, d), lambda i: (i, 0)),
            pl.BlockSpec((1, CONV_WIDTH - 1, d), lambda i: (i // steps_per_seq, 0, 0)),
        ],
        out_shape=[
            jax.ShapeDtypeStruct((nb * length, d), _F32),
            jax.ShapeDtypeStruct((nb, CONV_WIDTH - 1, d), _F32),
        ],
        scratch_shapes=[pltpu.VMEM((8, d), _F32)],
        compiler_params=pltpu.CompilerParams(
            dimension_semantics=("arbitrary",),
            vmem_limit_bytes=VMEM_LIMIT_BYTES),
        name="prompt_layer",
    )(x.reshape(nb * length, d), mod, win, wpa, wpb, wo, wsm, bias, *small)
    return y.reshape(nb, length, d), conv


_PROJ_ORDER = (1, 0, 2, 4, 5, 3, 6, 7, 8)


def _sample_kernel(bs_ref, ws_ref, cs_ref, cp_ref, bc_ref,
                   alg_ref, alb_ref, wconv_ref, bconv_ref, lng_ref, lnb_ref,
                   x_hbm, st_hbm, ws_hbm, wc_hbm, win_hbm, wpa_hbm, wpb_hbm, wo_hbm,
                   y_hbm, conv_hbm, v_hbm, win_out, wpa_out, wpb_out, wo_out, modp_ref,
                   wsm_ref, bias_ref,
                   win_ref, wpa_ref, wpb_ref, wo_ref, stage, xbuf, stbuf, ybuf, cvbuf, vbuf, ws_smem,
                   sem, exp_sem, in_sem, res_sem, ws_sem, *, alpha):
    steps, n, d = xbuf.shape
    gd = d // GROUPS
    hist_rows = CONV_WIDTH - 1

    ws_copy = pltpu.make_async_copy(ws_hbm.at[:, 0:ws_smem.shape[1], :], ws_smem, ws_sem.at[0])
    ws_copy.start()

    in_copies = ([pltpu.make_async_copy(x_hbm.at[:, t, :], xbuf.at[t], in_sem.at[t])
                  for t in range(steps)]
                 + [pltpu.make_async_copy(st_hbm.at[:, k, :], stbuf.at[k], in_sem.at[steps + k])
                    for k in range(hist_rows)])
    for cp in in_copies:
        cp.start()

    blocks = ([(wc_hbm, None, None, j) for j in range(wc_hbm.shape[1] // d)]
              + [(win_hbm, win_ref, win_out, k) for k in _PROJ_ORDER]
              + [(wpa_hbm, wpa_ref, wpa_out, 0), (wpb_hbm, wpb_ref, wpb_out, 0),
                 (wo_hbm, wo_ref, wo_out, 0)])
    exports = []
    position = [0]

    def stage_copy(i):
        src, _, _, k = blocks[i]
        slot = i % STAGE_SLOTS
        return pltpu.make_async_copy(src.at[:, k * d:(k + 1) * d], stage.at[slot], sem.at[slot])

    def load_weight(i):
        stage_copy(i).wait()
        if i + STAGE_SLOTS - 1 < len(blocks):
            stage_copy(i + STAGE_SLOTS - 1).start()
        w = stage[i % STAGE_SLOTS].astype(_BF16)
        _, keep, out, k = blocks[i]
        if keep is not None:
            cols = slice(k * d, (k + 1) * d)
            keep[:, cols] = pltpu.bitcast(w, _PACKED)
            cp = pltpu.make_async_copy(keep.at[:, cols], out.at[:, cols], exp_sem.at[len(exports)])
            cp.start()
            exports.append(cp)
        return w

    def next_weight():
        i = position[0]
        position[0] += 1
        w = ahead.pop()
        if i + 1 < len(blocks):
            ahead.append(load_weight(i + 1))
        return w

    for i in range(STAGE_SLOTS - 1):
        stage_copy(i).start()
    ahead = [load_weight(0)]

    c_all = jnp.concatenate([cs_ref[...], cp_ref[...]], axis=0).astype(_BF16)
    bc = bc_ref[...]
    mods = []
    for j in range(3):
        m = _dot(c_all, next_weight()) + bc[:, j * d:(j + 1) * d]
        modp_ref[:, j * d:(j + 1) * d] = m[n:, :]
        mods.append(m[0:n, :])
    shift, scale, gate = mods

    row = lax.broadcasted_iota(jnp.int32, (CHUNK, CHUNK), 0)
    col = lax.broadcasted_iota(jnp.int32, (CHUNK, CHUNK), 1)
    for g in range(GROUPS):
        wsm_ref[g] = pltpu.bitcast(jnp.where(col <= row, ws_ref[g], 0.0).astype(_BF16), _PACKED)

    for t in range(CHUNK):
        for g in range(GROUPS):
            bias_ref[t:t + 1, g * gd:(g + 1) * gd] = jnp.full((1, gd), bs_ref[g, t], _F32)

    for cp in in_copies:
        cp.wait()
    ws_copy.wait()
    h = jnp.concatenate([(xbuf[t] * (1.0 + scale) + shift).astype(_BF16) for t in range(steps)],
                        axis=0)

    def rows(a, t):
        return a[t * n:(t + 1) * n, :]

    def proj():
        return _dot(h, next_weight())

    results = []

    def send(buf, t, dst):
        cp = pltpu.make_async_copy(buf.at[t], dst.at[:, t, :], res_sem.at[len(results)])
        cp.start()
        results.append(cp)

    vn = _layer_norm(proj(), alg_ref[...], alb_ref[...])
    s_rows = []
    for t in range(steps):
        vbuf[t] = rows(vn, t)
        send(vbuf, t, v_hbm)
        s_cols = []
        for g in range(GROUPS):
            cols = slice(g * gd, (g + 1) * gd)
            acc = jnp.full((n, gd), bs_ref[g, t], _F32)
            for jj in range(t + 1):
                acc = acc + ws_smem[g, t, jj] * rows(vn, jj)[:, cols]
            s_cols.append(acc)
        s_rows.append(jnp.concatenate(s_cols, axis=1))
    s = jnp.concatenate(s_rows, axis=0)
    u = proj()
    out_a = (u * s * _silu(proj())).astype(_BF16)

    z = proj()
    z = z * proj()
    hist = [stbuf[k] for k in range(hist_rows)] + [rows(z, t) for t in range(steps)]
    for k in range(hist_rows):
        cvbuf[k] = hist[steps + k]
        send(cvbuf, k, conv_hbm)
    wconv = [wconv_ref[:, k * d:(k + 1) * d] for k in range(CONV_WIDTH)]
    conv = jnp.concatenate(
        [bconv_ref[...] + sum(wconv[k] * hist[t + k] for k in range(CONV_WIDTH))
         for t in range(steps)], axis=0)
    b_g = proj()
    out_b = (b_g * conv * _silu(proj())).astype(_BF16)

    sg_a = jax.nn.sigmoid(proj())
    sg_b = jax.nn.sigmoid(proj())
    pa = _dot(out_a, next_weight())
    pb = _dot(out_b, next_weight())
    merged = (sg_a * pa + sg_b * pb).astype(_BF16)
    y = _dot(merged, next_weight())
    for t in range(steps):
        ybuf[t] = _layer_norm(alpha * xbuf[t] + (1.0 + gate) * rows(y, t), lng_ref[...], lnb_ref[...])
        send(ybuf, t, y_hbm)

    for cp in exports + results:
        cp.wait()


def _sample_call(x, state, c_s, c_p, w_c, b_c, win, wpa, wpb, wo, w_s, b_s,
                 alg, alb, wconv, bconv, lng, lnb, alpha):
    n, steps, d = x.shape
    nb = c_p.shape[0]
    hist_rows = CONV_WIDTH - 1
    assert hist_rows <= steps <= CHUNK and w_c.shape == (d, 3 * d)
    assert w_s.shape == (GROUPS, CHUNK, CHUNK) and d == GROUPS * CHUNK
    vmem = pl.BlockSpec(memory_space=pltpu.VMEM)
    smem = pl.BlockSpec(memory_space=pltpu.SMEM)
    hbm = pl.BlockSpec(memory_space=pl.ANY)
    weights = (win, wpa, wpb, wo)
    n_exports = sum(w.shape[1] // d for w in weights)
    n_results = 2 * steps + hist_rows
    ws_rows = -(-steps // 8) * 8
    y, conv, v, *weights_bf16, mod_p, wsm, bias = pl.pallas_call(
        functools.partial(_sample_kernel, alpha=alpha),
        in_specs=[smem] + [vmem] * 10 + [hbm] * 8,
        out_specs=[hbm] * (3 + len(weights)) + [vmem] * 3,
        out_shape=[
            jax.ShapeDtypeStruct((n, steps, d), _F32),
            jax.ShapeDtypeStruct((n, hist_rows, d), _F32),
            jax.ShapeDtypeStruct((n, steps, d), _F32),
        ] + [jax.ShapeDtypeStruct(_packed_shape(w), _PACKED) for w in weights] + [
            jax.ShapeDtypeStruct((nb, 3 * d), _F32),
            jax.ShapeDtypeStruct((GROUPS, CHUNK // 2, CHUNK), _PACKED),
            jax.ShapeDtypeStruct((CHUNK, d), _F32)],
        scratch_shapes=[pltpu.VMEM(_packed_shape(w), _PACKED) for w in weights] + [
            pltpu.VMEM((STAGE_SLOTS, d, d), _F32),
            pltpu.VMEM((steps, n, d), _F32), pltpu.VMEM((hist_rows, n, d), _F32),
            pltpu.VMEM((steps, n, d), _F32), pltpu.VMEM((hist_rows, n, d), _F32),
            pltpu.VMEM((steps, n, d), _F32),
            pltpu.SMEM((w_s.shape[0], ws_rows, w_s.shape[2]), _F32),
            pltpu.SemaphoreType.DMA((STAGE_SLOTS,)), pltpu.SemaphoreType.DMA((n_exports,)),
            pltpu.SemaphoreType.DMA((steps + hist_rows,)), pltpu.SemaphoreType.DMA((n_results,)),
            pltpu.SemaphoreType.DMA((1,))],
        compiler_params=pltpu.CompilerParams(vmem_limit_bytes=VMEM_LIMIT_BYTES),
        name="sample_layer",
    )(b_s, w_s, c_s, c_p, b_c.reshape(1, 3 * d), alg, alb, wconv, bconv, lng, lnb,
      x, state, w_s, w_c, win, wpa, wpb, wo)
    return y, conv, v, weights_bf16, mod_p, wsm, bias


def kernel(x_prompt, x_sample, state_conv, c_prompt, c_sample, w_c, b_c, w_in, a_ln_g, a_ln_b,
           w_s, b_s, w_conv, b_conv, w_pa, w_pb, w_o, ln_g, ln_b):
    depth = w_in.shape[0]
    d = x_prompt.shape[-1]
    alpha = (2.0 * depth) ** 0.25

    xp, xs = x_prompt, x_sample
    conv_p_rows, conv_s_rows, v_rows = [], [], []
    for l in range(depth):
        row = lambda a: a.reshape(1, d)
        small = (row(a_ln_g[l]), row(a_ln_b[l]), w_conv[l].reshape(1, CONV_WIDTH * d),
                 row(b_conv[l]), row(ln_g[l]), row(ln_b[l]))
        xs, conv_s, v_s, weights_bf16, mod_p, wsm, bias = _sample_call(
            xs, state_conv[l], c_sample, c_prompt, w_c[l], b_c[l],
            w_in[l], w_pa[l], w_pb[l], w_o[l], w_s[l], b_s[l], *small, alpha)
        xp, conv_p = _prompt_call(xp, mod_p, *weights_bf16, wsm, bias, *small, alpha)
        conv_p_rows.append(conv_p)
        conv_s_rows.append(conv_s)
        v_rows.append(v_s)
    stack = (lambda rows: rows[0][None]) if depth == 1 else jnp.stack
    return (xp, xs, stack(conv_p_rows), stack(conv_s_rows), stack(v_rows))
```

```python
import functools

import jax
import jax.numpy as jnp
from jax import lax
from jax.experimental import pallas as pl
from jax.experimental.pallas import tpu as pltpu

CHUNK = 128
GROUPS = 8
CONV_WIDTH = 3
LN_EPS = 1e-5
ROW_TILE = 512
SUB_TILE = 256
STAGE_SLOTS = 4
VMEM_LIMIT_BYTES = 62 * 1024 * 1024

_F32 = jnp.float32
_BF16 = jnp.bfloat16


def _dot(a, b):
    return jnp.dot(a, b, preferred_element_type=_F32)


def _layer_norm(x, gain, bias):
    mu = jnp.mean(x, axis=-1, keepdims=True)
    xc = x - mu
    var = jnp.mean(xc * xc, axis=-1, keepdims=True)
    return xc * lax.rsqrt(var + LN_EPS) * gain + bias


def _sigmoid(x):
    return 0.5 * jnp.tanh(0.5 * x) + 0.5


def _silu(x):
    half = 0.5 * x
    return half * jnp.tanh(half) + half


_PACKED = jnp.uint32


def _packed_shape(w):
    return (w.shape[0] // 2, w.shape[1])


def _as_bf16(words):
    return pltpu.bitcast(words, _BF16)


def _prompt_kernel(x_ref, mod_ref, win_ref, wpa_ref, wpb_ref, wo_ref, wsm_ref, bias_ref,
                   alg_ref, alb_ref, wconv_ref, bconv_ref, lng_ref, lnb_ref,
                   y_ref, conv_ref, z_scr, *, alpha, steps_per_seq):
    tm, d = x_ref.shape
    sub = SUB_TILE
    gd = d // GROUPS
    i = pl.program_id(0)

    mod = mod_ref[pl.ds(i // steps_per_seq, 1), :]
    shift, scale, gate = mod[:, 0:d], mod[:, d:2 * d], mod[:, 2 * d:3 * d]
    bias = bias_ref[...]
    carried = jnp.where(i % steps_per_seq == 0, 0.0, z_scr[...])
    wconv = [wconv_ref[:, k * d:(k + 1) * d] for k in range(CONV_WIDTH)]
    starts = range(0, tm, sub)
    tails = []

    def modulated(r0):
        return (x_ref[r0:r0 + sub, :] * (1.0 + scale) + shift).astype(_BF16)

    def project(h, ks):
        return [_dot(h, _as_bf16(win_ref[:, k * d:(k + 1) * d])) for k in ks]

    def mixer_a(u, v, z_a):
        vb = _layer_norm(v, alg_ref[...], alb_ref[...]).astype(_BF16)
        s_cols = []
        for g in range(GROUPS):
            cols = slice(g * gd, (g + 1) * gd)
            pair_rows = []
            for c in range(0, sub // CHUNK, 2):
                rhs = jnp.concatenate([vb[c * CHUNK:(c + 1) * CHUNK, cols],
                                       vb[(c + 1) * CHUNK:(c + 2) * CHUNK, cols]], axis=1)
                res = _dot(_as_bf16(wsm_ref[g]), rhs)
                pair_rows.append(res[:, :gd] + bias[:, cols])
                pair_rows.append(res[:, gd:] + bias[:, cols])
            s_cols.append(jnp.concatenate(pair_rows, axis=0))
        s = jnp.concatenate(s_cols, axis=1)
        return (u * s * _silu(z_a)).astype(_BF16)

    def mixer_b(b_g, c_g, h_b, z_b):
        z = c_g * h_b
        hist = carried if not tails else tails[-1]
        tails.append(z[sub - 8:sub, :])
        top_rows = lax.broadcasted_iota(jnp.int32, (8, d), 0)

        def delayed(k):
            rolled = pltpu.roll(z, k, axis=0)
            top = jnp.where(top_rows < k, pltpu.roll(hist, k, axis=0), rolled[0:8, :])
            return jnp.concatenate([top, rolled[8:, :]], axis=0)

        conv = bconv_ref[...] + wconv[0] * delayed(2) + wconv[1] * delayed(1) + wconv[2] * z
        return (b_g * conv * _silu(z_b)).astype(_BF16)

    def merge(out_a, out_b, g_a, g_b):
        return (_sigmoid(g_a) * _dot(out_a, _as_bf16(wpa_ref[...]))
                + _sigmoid(g_b) * _dot(out_b, _as_bf16(wpb_ref[...]))).astype(_BF16)

    def finish(r0, mg):
        x = x_ref[r0:r0 + sub, :]
        y_ref[r0:r0 + sub, :] = _layer_norm(
            alpha * x + (1.0 + gate) * _dot(mg, _as_bf16(wo_ref[...])), lng_ref[...], lnb_ref[...])

    r_a, r_b = starts
    u, v, z_a, b_g, c_g, h_b, z_b, g_a, g_b = project(modulated(r_a), range(9))
    hb = modulated(r_b)
    u2, v2, z_a2 = project(hb, range(0, 3))
    out_a = mixer_a(u, v, z_a)
    b_g2, c_g2, h_b2, z_b2 = project(hb, range(3, 7))
    out_b = mixer_b(b_g, c_g, h_b, z_b)
    g_a2, g_b2 = project(hb, range(7, 9))
    out_a2 = mixer_a(u2, v2, z_a2)
    mg = merge(out_a, out_b, g_a, g_b)
    out_b2 = mixer_b(b_g2, c_g2, h_b2, z_b2)
    finish(r_a, mg)
    finish(r_b, merge(out_a2, out_b2, g_a2, g_b2))

    z_scr[...] = tails[-1]
    conv_ref[0] = tails[-1][8 - (CONV_WIDTH - 1):8, :]


def _resident(shape):
    nd = len(shape)
    return pl.BlockSpec(shape, lambda i: (0,) * nd, pipeline_mode=pl.Buffered(1))


def _prompt_call(x, mod, win, wpa, wpb, wo, wsm, bias, alg, alb, wconv, bconv, lng, lnb, alpha):
    nb, length, d = x.shape
    tm = ROW_TILE
    assert length % tm == 0 and tm == 2 * SUB_TILE and SUB_TILE % (2 * CHUNK) == 0
    assert d % GROUPS == 0 and d // GROUPS == CHUNK
    steps_per_seq = length // tm
    small = [alg, alb, wconv, bconv, lng, lnb]
    y, conv = pl.pallas_call(
        functools.partial(_prompt_kernel, alpha=alpha, steps_per_seq=steps_per_seq),
        grid=(nb * steps_per_seq,),
        in_specs=[
            pl.BlockSpec((tm, d), lambda i: (i, 0)),
            _resident(mod.shape),
            _resident(win.shape), _resident(wpa.shape), _resident(wpb.shape), _resident(wo.shape),
            _resident(wsm.shape), _resident(bias.shape),
        ] + [_resident(a.shape) for a in small],
        out_specs=[
            pl.BlockSpec((tm, d), lambda i: (i, 0)),
            pl.BlockSpec((1, CONV_WIDTH - 1, d), lambda i: (i // steps_per_seq, 0, 0)),
        ],
        out_shape=[
            jax.ShapeDtypeStruct((nb * length, d), _F32),
            jax.ShapeDtypeStruct((nb, CONV_WIDTH - 1, d), _F32),
        ],
        scratch_shapes=[pltpu.VMEM((8, d), _F32)],
        compiler_params=pltpu.CompilerParams(
            dimension_semantics=("arbitrary",),
            vmem_limit_bytes=VMEM_LIMIT_BYTES),
        name="prompt_layer",
    )(x.reshape(nb * length, d), mod, win, wpa, wpb, wo, wsm, bias, *small)
    return y.reshape(nb, length, d), conv


_PROJ_ORDER = (1, 0, 2, 4, 5, 3, 6, 7, 8)


def _sample_kernel(bs_ref, ws_ref, cs_ref, cp_ref, bc_ref,
                   alg_ref, alb_ref, wconv_ref, bconv_ref, lng_ref, lnb_ref,
                   x_hbm, st_hbm, ws_hbm, wc_hbm, win_hbm, wpa_hbm, wpb_hbm, wo_hbm,
                   y_hbm, conv_hbm, v_hbm, win_out, wpa_out, wpb_out, wo_out, modp_ref,
                   wsm_ref, bias_ref,
                   win_ref, wpa_ref, wpb_ref, wo_ref, stage, xbuf, stbuf, ybuf, cvbuf, vbuf, ws_smem,
                   sem, exp_sem, in_sem, res_sem, ws_sem, *, alpha):
    steps, n, d = xbuf.shape
    gd = d // GROUPS
    hist_rows = CONV_WIDTH - 1

    ws_copy = pltpu.make_async_copy(ws_hbm.at[:, 0:ws_smem.shape[1], :], ws_smem, ws_sem.at[0])
    ws_copy.start()

    in_copies = ([pltpu.make_async_copy(x_hbm.at[:, t, :], xbuf.at[t], in_sem.at[t])
                  for t in range(steps)]
                 + [pltpu.make_async_copy(st_hbm.at[:, k, :], stbuf.at[k], in_sem.at[steps + k])
                    for k in range(hist_rows)])
    for cp in in_copies:
        cp.start()

    blocks = ([(wc_hbm, None, None, j) for j in range(wc_hbm.shape[1] // d)]
              + [(win_hbm, win_ref, win_out, k) for k in _PROJ_ORDER]
              + [(wpa_hbm, wpa_ref, wpa_out, 0), (wpb_hbm, wpb_ref, wpb_out, 0),
                 (wo_hbm, wo_ref, wo_out, 0)])
    exports = []
    position = [0]

    def stage_copy(i):
        src, _, _, k = blocks[i]
        slot = i % STAGE_SLOTS
        return pltpu.make_async_copy(src.at[:, k * d:(k + 1) * d], stage.at[slot], sem.at[slot])

    def next_weight():
        i = position[0]
        position[0] += 1
        stage_copy(i).wait()
        if i + STAGE_SLOTS - 1 < len(blocks):
            stage_copy(i + STAGE_SLOTS - 1).start()
        w = stage[i % STAGE_SLOTS].astype(_BF16)
        _, keep, out, k = blocks[i]
        if keep is not None:
            cols = slice(k * d, (k + 1) * d)
            keep[:, cols] = pltpu.bitcast(w, _PACKED)
            cp = pltpu.make_async_copy(keep.at[:, cols], out.at[:, cols], exp_sem.at[len(exports)])
            cp.start()
            exports.append(cp)
        return w

    for i in range(STAGE_SLOTS - 1):
        stage_copy(i).start()

    c_all = jnp.concatenate([cs_ref[...], cp_ref[...]], axis=0).astype(_BF16)
    bc = bc_ref[...]
    mods = []
    for j in range(3):
        m = _dot(c_all, next_weight()) + bc[:, j * d:(j + 1) * d]
        modp_ref[:, j * d:(j + 1) * d] = m[n:, :]
        mods.append(m[0:n, :])
    shift, scale, gate = mods

    row = lax.broadcasted_iota(jnp.int32, (CHUNK, CHUNK), 0)
    col = lax.broadcasted_iota(jnp.int32, (CHUNK, CHUNK), 1)
    for g in range(GROUPS):
        wsm_ref[g] = pltpu.bitcast(jnp.where(col <= row, ws_ref[g], 0.0).astype(_BF16), _PACKED)

    for t in range(CHUNK):
        for g in range(GROUPS):
            bias_ref[t:t + 1, g * gd:(g + 1) * gd] = jnp.full((1, gd), bs_ref[g, t], _F32)

    for cp in in_copies:
        cp.wait()
    ws_copy.wait()
    h = jnp.concatenate([(xbuf[t] * (1.0 + scale) + shift).astype(_BF16) for t in range(steps)],
                        axis=0)

    def rows(a, t):
        return a[t * n:(t + 1) * n, :]

    def proj():
        return _dot(h, next_weight())

    results = []

    def send(buf, t, dst):
        cp = pltpu.make_async_copy(buf.at[t], dst.at[:, t, :], res_sem.at[len(results)])
        cp.start()
        results.append(cp)

    vn = _layer_norm(proj(), alg_ref[...], alb_ref[...])
    s_rows = []
    for t in range(steps):
        vbuf[t] = rows(vn, t)
        send(vbuf, t, v_hbm)
        s_cols = []
        for g in range(GROUPS):
            cols = slice(g * gd, (g + 1) * gd)
            acc = jnp.full((n, gd), bs_ref[g, t], _F32)
            for jj in range(t + 1):
                acc = acc + ws_smem[g, t, jj] * rows(vn, jj)[:, cols]
            s_cols.append(acc)
        s_rows.append(jnp.concatenate(s_cols, axis=1))
    s = jnp.concatenate(s_rows, axis=0)
    u = proj()
    out_a = (u * s * _silu(proj())).astype(_BF16)

    z = proj()
    z = z * proj()
    hist = [stbuf[k] for k in range(hist_rows)] + [rows(z, t) for t in range(steps)]
    for k in range(hist_rows):
        cvbuf[k] = hist[steps + k]
        send(cvbuf, k, conv_hbm)
    wconv = [wconv_ref[:, k * d:(k + 1) * d] for k in range(CONV_WIDTH)]
    conv = jnp.concatenate(
        [bconv_ref[...] + sum(wconv[k] * hist[t + k] for k in range(CONV_WIDTH))
         for t in range(steps)], axis=0)
    b_g = proj()
    out_b = (b_g * conv * _silu(proj())).astype(_BF16)

    sg_a = _sigmoid(proj())
    sg_b = _sigmoid(proj())
    pa = _dot(out_a, next_weight())
    pb = _dot(out_b, next_weight())
    merged = (sg_a * pa + sg_b * pb).astype(_BF16)
    y = _dot(merged, next_weight())
    for t in range(steps):
        ybuf[t] = _layer_norm(alpha * xbuf[t] + (1.0 + gate) * rows(y, t), lng_ref[...], lnb_ref[...])
        send(ybuf, t, y_hbm)

    for cp in exports + results:
        cp.wait()


def _sample_call(x, state, c_s, c_p, w_c, b_c, win, wpa, wpb, wo, w_s, b_s,
                 alg, alb, wconv, bconv, lng, lnb, alpha):
    n, steps, d = x.shape
    nb = c_p.shape[0]
    hist_rows = CONV_WIDTH - 1
    assert hist_rows <= steps <= CHUNK and w_c.shape == (d, 3 * d)
    assert w_s.shape == (GROUPS, CHUNK, CHUNK) and d == GROUPS * CHUNK
    vmem = pl.BlockSpec(memory_space=pltpu.VMEM)
    smem = pl.BlockSpec(memory_space=pltpu.SMEM)
    hbm = pl.BlockSpec(memory_space=pl.ANY)
    weights = (win, wpa, wpb, wo)
    n_exports = sum(w.shape[1] // d for w in weights)
    n_results = 2 * steps + hist_rows
    ws_rows = -(-steps // 8) * 8
    y, conv, v, *weights_bf16, mod_p, wsm, bias = pl.pallas_call(
        functools.partial(_sample_kernel, alpha=alpha),
        in_specs=[smem] + [vmem] * 10 + [hbm] * 8,
        out_specs=[hbm] * (3 + len(weights)) + [vmem] * 3,
        out_shape=[
            jax.ShapeDtypeStruct((n, steps, d), _F32),
            jax.ShapeDtypeStruct((n, hist_rows, d), _F32),
            jax.ShapeDtypeStruct((n, steps, d), _F32),
        ] + [jax.ShapeDtypeStruct(_packed_shape(w), _PACKED) for w in weights] + [
            jax.ShapeDtypeStruct((nb, 3 * d), _F32),
            jax.ShapeDtypeStruct((GROUPS, CHUNK // 2, CHUNK), _PACKED),
            jax.ShapeDtypeStruct((CHUNK, d), _F32)],
        scratch_shapes=[pltpu.VMEM(_packed_shape(w), _PACKED) for w in weights] + [
            pltpu.VMEM((STAGE_SLOTS, d, d), _F32),
            pltpu.VMEM((steps, n, d), _F32), pltpu.VMEM((hist_rows, n, d), _F32),
            pltpu.VMEM((steps, n, d), _F32), pltpu.VMEM((hist_rows, n, d), _F32),
            pltpu.VMEM((steps, n, d), _F32),
            pltpu.SMEM((w_s.shape[0], ws_rows, w_s.shape[2]), _F32),
            pltpu.SemaphoreType.DMA((STAGE_SLOTS,)), pltpu.SemaphoreType.DMA((n_exports,)),
            pltpu.SemaphoreType.DMA((steps + hist_rows,)), pltpu.SemaphoreType.DMA((n_results,)),
            pltpu.SemaphoreType.DMA((1,))],
        compiler_params=pltpu.CompilerParams(vmem_limit_bytes=VMEM_LIMIT_BYTES),
        name="sample_layer",
    )(b_s, w_s, c_s, c_p, b_c.reshape(1, 3 * d), alg, alb, wconv, bconv, lng, lnb,
      x, state, w_s, w_c, win, wpa, wpb, wo)
    return y, conv, v, weights_bf16, mod_p, wsm, bias


def kernel(x_prompt, x_sample, state_conv, c_prompt, c_sample, w_c, b_c, w_in, a_ln_g, a_ln_b,
           w_s, b_s, w_conv, b_conv, w_pa, w_pb, w_o, ln_g, ln_b):
    depth = w_in.shape[0]
    d = x_prompt.shape[-1]
    alpha = (2.0 * depth) ** 0.25

    xp, xs = x_prompt, x_sample
    conv_p_rows, conv_s_rows, v_rows = [], [], []
    for l in range(depth):
        row = lambda a: a.reshape(1, d)
        small = (row(a_ln_g[l]), row(a_ln_b[l]), w_conv[l].reshape(1, CONV_WIDTH * d),
                 row(b_conv[l]), row(ln_g[l]), row(ln_b[l]))
        xs, conv_s, v_s, weights_bf16, mod_p, wsm, bias = _sample_call(
            xs, state_conv[l], c_sample, c_prompt, w_c[l], b_c[l],
            w_in[l], w_pa[l], w_pb[l], w_o[l], w_s[l], b_s[l], *small, alpha)
        xp, conv_p = _prompt_call(xp, mod_p, *weights_bf16, wsm, bias, *small, alpha)
        conv_p_rows.append(conv_p)
        conv_s_rows.append(conv_s)
        v_rows.append(v_s)
    stack = (lambda rows: rows[0][None]) if depth == 1 else jnp.stack
    return (xp, xs, stack(conv_p_rows), stack(conv_s_rows), stack(v_rows))
```

```python
import functools

import jax
import jax.numpy as jnp
from jax import lax
from jax.experimental import pallas as pl
from jax.experimental.pallas import tpu as pltpu

CHUNK = 128
GROUPS = 8
CONV_WIDTH = 3
LN_EPS = 1e-5
ROW_TILE = 512
TILES_PER_STEP = 2
SUB_TILE = 256
STAGE_SLOTS = 4
VMEM_LIMIT_BYTES = 62 * 1024 * 1024

_F32 = jnp.float32
_BF16 = jnp.bfloat16


def _dot(a, b):
    return jnp.dot(a, b, preferred_element_type=_F32)


def _layer_norm(x, gain, bias):
    mu = jnp.mean(x, axis=-1, keepdims=True)
    xc = x - mu
    var = jnp.mean(xc * xc, axis=-1, keepdims=True)
    return xc * lax.rsqrt(var + LN_EPS) * gain + bias


def _silu(x):
    return x * jax.nn.sigmoid(x)


_PACKED = jnp.uint32


def _packed_shape(w):
    return (w.shape[0] // 2, w.shape[1])


def _as_bf16(words):
    return pltpu.bitcast(words, _BF16)


def _prompt_kernel(x_ref, mod_ref, win_ref, wpa_ref, wpb_ref, wo_ref, wsm_ref, bias_ref,
                   alg_ref, alb_ref, wconv_ref, bconv_ref, lng_ref, lnb_ref,
                   y_ref, conv_ref, z_scr, *, alpha, tiles_per_seq):
    d = x_ref.shape[1]
    lax.fori_loop(0, TILES_PER_STEP, functools.partial(
        _prompt_tile, x_ref, mod_ref, win_ref, wpa_ref, wpb_ref, wo_ref, wsm_ref, bias_ref,
        alg_ref, alb_ref, wconv_ref, bconv_ref, lng_ref, lnb_ref, y_ref, conv_ref, z_scr,
        alpha, tiles_per_seq, d), 0)


def _prompt_tile(x_ref, mod_ref, win_ref, wpa_ref, wpb_ref, wo_ref, wsm_ref, bias_ref,
                 alg_ref, alb_ref, wconv_ref, bconv_ref, lng_ref, lnb_ref, y_ref, conv_ref, z_scr,
                 alpha, tiles_per_seq, d, t, carry):
    tm = ROW_TILE
    sub = SUB_TILE
    gd = d // GROUPS
    j = pl.program_id(0) * TILES_PER_STEP + t
    base = t * tm

    def rows(r0):
        return pl.ds(pl.multiple_of(base + r0, sub), sub)

    mod = mod_ref[pl.ds(j // tiles_per_seq, 1), :]
    shift, scale, gate = mod[:, 0:d], mod[:, d:2 * d], mod[:, 2 * d:3 * d]
    bias = bias_ref[...]
    carried = jnp.where(j % tiles_per_seq == 0, 0.0, z_scr[...])
    wconv = [wconv_ref[:, k * d:(k + 1) * d] for k in range(CONV_WIDTH)]
    starts = range(0, tm, sub)
    tails = []

    def modulated(r0):
        return (x_ref[rows(r0), :] * (1.0 + scale) + shift).astype(_BF16)

    def project(h, ks):
        return [_dot(h, _as_bf16(win_ref[:, k * d:(k + 1) * d])) for k in ks]

    def mixer_a(u, v, z_a):
        vb = _layer_norm(v, alg_ref[...], alb_ref[...]).astype(_BF16)
        s_cols = []
        for g in range(GROUPS):
            cols = slice(g * gd, (g + 1) * gd)
            pair_rows = []
            for c in range(0, sub // CHUNK, 2):
                rhs = jnp.concatenate([vb[c * CHUNK:(c + 1) * CHUNK, cols],
                                       vb[(c + 1) * CHUNK:(c + 2) * CHUNK, cols]], axis=1)
                res = _dot(_as_bf16(wsm_ref[g]), rhs)
                pair_rows.append(res[:, :gd] + bias[:, cols])
                pair_rows.append(res[:, gd:] + bias[:, cols])
            s_cols.append(jnp.concatenate(pair_rows, axis=0))
        s = jnp.concatenate(s_cols, axis=1)
        return (u * s * _silu(z_a)).astype(_BF16)

    def mixer_b(b_g, c_g, h_b, z_b):
        z = c_g * h_b
        hist = carried if not tails else tails[-1]
        tails.append(z[sub - 8:sub, :])
        top_rows = lax.broadcasted_iota(jnp.int32, (8, d), 0)

        def delayed(k):
            rolled = pltpu.roll(z, k, axis=0)
            top = jnp.where(top_rows < k, pltpu.roll(hist, k, axis=0), rolled[0:8, :])
            return jnp.concatenate([top, rolled[8:, :]], axis=0)

        conv = bconv_ref[...] + wconv[0] * delayed(2) + wconv[1] * delayed(1) + wconv[2] * z
        return (b_g * conv * _silu(z_b)).astype(_BF16)

    def merge(out_a, out_b, g_a, g_b):
        return (jax.nn.sigmoid(g_a) * _dot(out_a, _as_bf16(wpa_ref[...]))
                + jax.nn.sigmoid(g_b) * _dot(out_b, _as_bf16(wpb_ref[...]))).astype(_BF16)

    def finish(r0, mg):
        x = x_ref[rows(r0), :]
        y_ref[rows(r0), :] = _layer_norm(
            alpha * x + (1.0 + gate) * _dot(mg, _as_bf16(wo_ref[...])), lng_ref[...], lnb_ref[...])

    r_a, r_b = starts
    u, v, z_a, b_g, c_g, h_b, z_b, g_a, g_b = project(modulated(r_a), range(9))
    hb = modulated(r_b)
    u2, v2, z_a2 = project(hb, range(0, 3))
    out_a = mixer_a(u, v, z_a)
    b_g2, c_g2, h_b2, z_b2 = project(hb, range(3, 7))
    out_b = mixer_b(b_g, c_g, h_b, z_b)
    g_a2, g_b2 = project(hb, range(7, 9))
    out_a2 = mixer_a(u2, v2, z_a2)
    mg = merge(out_a, out_b, g_a, g_b)
    out_b2 = mixer_b(b_g2, c_g2, h_b2, z_b2)
    finish(r_a, mg)
    finish(r_b, merge(out_a2, out_b2, g_a2, g_b2))

    z_scr[...] = tails[-1]
    conv_ref[0] = tails[-1][8 - (CONV_WIDTH - 1):8, :]
    return carry


def _resident(shape):
    nd = len(shape)
    return pl.BlockSpec(shape, lambda i: (0,) * nd, pipeline_mode=pl.Buffered(1))


def _prompt_call(x, mod, win, wpa, wpb, wo, wsm, bias, alg, alb, wconv, bconv, lng, lnb, alpha):
    nb, length, d = x.shape
    tm = ROW_TILE
    assert length % tm == 0 and tm == 2 * SUB_TILE and SUB_TILE % (2 * CHUNK) == 0
    assert d % GROUPS == 0 and d // GROUPS == CHUNK
    tiles_per_seq = length // tm
    assert tiles_per_seq % TILES_PER_STEP == 0
    steps_per_seq = tiles_per_seq // TILES_PER_STEP
    block = TILES_PER_STEP * tm
    small = [alg, alb, wconv, bconv, lng, lnb]
    y, conv = pl.pallas_call(
        functools.partial(_prompt_kernel, alpha=alpha, tiles_per_seq=tiles_per_seq),
        grid=(nb * steps_per_seq,),
        in_specs=[
            pl.BlockSpec((block, d), lambda i: (i, 0)),
            _resident(mod.shape),
            _resident(win.shape), _resident(wpa.shape), _resident(wpb.shape), _resident(wo.shape),
            _resident(wsm.shape), _resident(bias.shape),
        ] + [_resident(a.shape) for a in small],
        out_specs=[
            pl.BlockSpec((block, d), lambda i: (i, 0)),
            pl.BlockSpec((1, CONV_WIDTH - 1, d), lambda i: (i // steps_per_seq, 0, 0)),
        ],
        out_shape=[
            jax.ShapeDtypeStruct((nb * length, d), _F32),
            jax.ShapeDtypeStruct((nb, CONV_WIDTH - 1, d), _F32),
        ],
        scratch_shapes=[pltpu.VMEM((8, d), _F32)],
        compiler_params=pltpu.CompilerParams(
            dimension_semantics=("arbitrary",),
            vmem_limit_bytes=VMEM_LIMIT_BYTES),
        name="prompt_layer",
    )(x.reshape(nb * length, d), mod, win, wpa, wpb, wo, wsm, bias, *small)
    return y.reshape(nb, length, d), conv


_PROJ_ORDER = (1, 0, 2, 4, 5, 3, 6, 7, 8)


def _sample_kernel(bs_ref, ws_ref, cs_ref, cp_ref, bc_ref,
                   alg_ref, alb_ref, wconv_ref, bconv_ref, lng_ref, lnb_ref,
                   x_hbm, st_hbm, ws_hbm, wc_hbm, win_hbm, wpa_hbm, wpb_hbm, wo_hbm,
                   y_hbm, conv_hbm, v_hbm, win_out, wpa_out, wpb_out, wo_out, modp_ref,
                   wsm_ref, bias_ref,
                   win_ref, wpa_ref, wpb_ref, wo_ref, stage, xbuf, stbuf, ybuf, cvbuf, vbuf, ws_smem,
                   sem, exp_sem, in_sem, res_sem, ws_sem, *, alpha):
    steps, n, d = xbuf.shape
    gd = d // GROUPS
    hist_rows = CONV_WIDTH - 1

    ws_copy = pltpu.make_async_copy(ws_hbm.at[:, 0:ws_smem.shape[1], :], ws_smem, ws_sem.at[0])
    ws_copy.start()

    in_copies = ([pltpu.make_async_copy(x_hbm.at[:, t, :], xbuf.at[t], in_sem.at[t])
                  for t in range(steps)]
                 + [pltpu.make_async_copy(st_hbm.at[:, k, :], stbuf.at[k], in_sem.at[steps + k])
                    for k in range(hist_rows)])
    for cp in in_copies:
        cp.start()

    blocks = ([(wc_hbm, None, None, j) for j in range(wc_hbm.shape[1] // d)]
              + [(win_hbm, win_ref, win_out, k) for k in _PROJ_ORDER]
              + [(wpa_hbm, wpa_ref, wpa_out, 0), (wpb_hbm, wpb_ref, wpb_out, 0),
                 (wo_hbm, wo_ref, wo_out, 0)])
    exports = []
    position = [0]

    def stage_copy(i):
        src, _, _, k = blocks[i]
        slot = i % STAGE_SLOTS
        return pltpu.make_async_copy(src.at[:, k * d:(k + 1) * d], stage.at[slot], sem.at[slot])

    def next_weight():
        i = position[0]
        position[0] += 1
        stage_copy(i).wait()
        if i + STAGE_SLOTS - 1 < len(blocks):
            stage_copy(i + STAGE_SLOTS - 1).start()
        w = stage[i % STAGE_SLOTS].astype(_BF16)
        _, keep, out, k = blocks[i]
        if keep is not None:
            cols = slice(k * d, (k + 1) * d)
            keep[:, cols] = pltpu.bitcast(w, _PACKED)
            cp = pltpu.make_async_copy(keep.at[:, cols], out.at[:, cols], exp_sem.at[len(exports)])
            cp.start()
            exports.append(cp)
        return w

    for i in range(STAGE_SLOTS - 1):
        stage_copy(i).start()

    c_all = jnp.concatenate([cs_ref[...], cp_ref[...]], axis=0).astype(_BF16)
    bc = bc_ref[...]
    mods = []
    for j in range(3):
        m = _dot(c_all, next_weight()) + bc[:, j * d:(j + 1) * d]
        modp_ref[:, j * d:(j + 1) * d] = m[n:, :]
        mods.append(m[0:n, :])
    shift, scale, gate = mods

    row = lax.broadcasted_iota(jnp.int32, (CHUNK, CHUNK), 0)
    col = lax.broadcasted_iota(jnp.int32, (CHUNK, CHUNK), 1)
    for g in range(GROUPS):
        wsm_ref[g] = pltpu.bitcast(jnp.where(col <= row, ws_ref[g], 0.0).astype(_BF16), _PACKED)

    for t in range(CHUNK):
        for g in range(GROUPS):
            bias_ref[t:t + 1, g * gd:(g + 1) * gd] = jnp.full((1, gd), bs_ref[g, t], _F32)

    for cp in in_copies:
        cp.wait()
    ws_copy.wait()
    h = jnp.concatenate([(xbuf[t] * (1.0 + scale) + shift).astype(_BF16) for t in range(steps)],
                        axis=0)

    def rows(a, t):
        return a[t * n:(t + 1) * n, :]

    def proj():
        return _dot(h, next_weight())

    results = []

    def send(buf, t, dst):
        cp = pltpu.make_async_copy(buf.at[t], dst.at[:, t, :], res_sem.at[len(results)])
        cp.start()
        results.append(cp)

    vn = _layer_norm(proj(), alg_ref[...], alb_ref[...])
    s_rows = []
    for t in range(steps):
        vbuf[t] = rows(vn, t)
        send(vbuf, t, v_hbm)
        s_cols = []
        for g in range(GROUPS):
            cols = slice(g * gd, (g + 1) * gd)
            acc = jnp.full((n, gd), bs_ref[g, t], _F32)
            for jj in range(t + 1):
                acc = acc + ws_smem[g, t, jj] * rows(vn, jj)[:, cols]
            s_cols.append(acc)
        s_rows.append(jnp.concatenate(s_cols, axis=1))
    s = jnp.concatenate(s_rows, axis=0)
    u = proj()
    out_a = (u * s * _silu(proj())).astype(_BF16)

    z = proj()
    z = z * proj()
    hist = [stbuf[k] for k in range(hist_rows)] + [rows(z, t) for t in range(steps)]
    for k in range(hist_rows):
        cvbuf[k] = hist[steps + k]
        send(cvbuf, k, conv_hbm)
    wconv = [wconv_ref[:, k * d:(k + 1) * d] for k in range(CONV_WIDTH)]
    conv = jnp.concatenate(
        [bconv_ref[...] + sum(wconv[k] * hist[t + k] for k in range(CONV_WIDTH))
         for t in range(steps)], axis=0)
    b_g = proj()
    out_b = (b_g * conv * _silu(proj())).astype(_BF16)

    sg_a = jax.nn.sigmoid(proj())
    sg_b = jax.nn.sigmoid(proj())
    pa = _dot(out_a, next_weight())
    pb = _dot(out_b, next_weight())
    merged = (sg_a * pa + sg_b * pb).astype(_BF16)
    y = _dot(merged, next_weight())
    for t in range(steps):
        ybuf[t] = _layer_norm(alpha * xbuf[t] + (1.0 + gate) * rows(y, t), lng_ref[...], lnb_ref[...])
        send(ybuf, t, y_hbm)

    for cp in exports + results:
        cp.wait()


def _sample_call(x, state, c_s, c_p, w_c, b_c, win, wpa, wpb, wo, w_s, b_s,
                 alg, alb, wconv, bconv, lng, lnb, alpha):
    n, steps, d = x.shape
    nb = c_p.shape[0]
    hist_rows = CONV_WIDTH - 1
    assert hist_rows <= steps <= CHUNK and w_c.shape == (d, 3 * d)
    assert w_s.shape == (GROUPS, CHUNK, CHUNK) and d == GROUPS * CHUNK
    vmem = pl.BlockSpec(memory_space=pltpu.VMEM)
    smem = pl.BlockSpec(memory_space=pltpu.SMEM)
    hbm = pl.BlockSpec(memory_space=pl.ANY)
    weights = (win, wpa, wpb, wo)
    n_exports = sum(w.shape[1] // d for w in weights)
    n_results = 2 * steps + hist_rows
    ws_rows = -(-steps // 8) * 8
    y, conv, v, *weights_bf16, mod_p, wsm, bias = pl.pallas_call(
        functools.partial(_sample_kernel, alpha=alpha),
        in_specs=[smem] + [vmem] * 10 + [hbm] * 8,
        out_specs=[hbm] * (3 + len(weights)) + [vmem] * 3,
        out_shape=[
            jax.ShapeDtypeStruct((n, steps, d), _F32),
            jax.ShapeDtypeStruct((n, hist_rows, d), _F32),
            jax.ShapeDtypeStruct((n, steps, d), _F32),
        ] + [jax.ShapeDtypeStruct(_packed_shape(w), _PACKED) for w in weights] + [
            jax.ShapeDtypeStruct((nb, 3 * d), _F32),
            jax.ShapeDtypeStruct((GROUPS, CHUNK // 2, CHUNK), _PACKED),
            jax.ShapeDtypeStruct((CHUNK, d), _F32)],
        scratch_shapes=[pltpu.VMEM(_packed_shape(w), _PACKED) for w in weights] + [
            pltpu.VMEM((STAGE_SLOTS, d, d), _F32),
            pltpu.VMEM((steps, n, d), _F32), pltpu.VMEM((hist_rows, n, d), _F32),
            pltpu.VMEM((steps, n, d), _F32), pltpu.VMEM((hist_rows, n, d), _F32),
            pltpu.VMEM((steps, n, d), _F32),
            pltpu.SMEM((w_s.shape[0], ws_rows, w_s.shape[2]), _F32),
            pltpu.SemaphoreType.DMA((STAGE_SLOTS,)), pltpu.SemaphoreType.DMA((n_exports,)),
            pltpu.SemaphoreType.DMA((steps + hist_rows,)), pltpu.SemaphoreType.DMA((n_results,)),
            pltpu.SemaphoreType.DMA((1,))],
        compiler_params=pltpu.CompilerParams(vmem_limit_bytes=VMEM_LIMIT_BYTES),
        name="sample_layer",
    )(b_s, w_s, c_s, c_p, b_c.reshape(1, 3 * d), alg, alb, wconv, bconv, lng, lnb,
      x, state, w_s, w_c, win, wpa, wpb, wo)
    return y, conv, v, weights_bf16, mod_p, wsm, bias


def kernel(x_prompt, x_sample, state_conv, c_prompt, c_sample, w_c, b_c, w_in, a_ln_g, a_ln_b,
           w_s, b_s, w_conv, b_conv, w_pa, w_pb, w_o, ln_g, ln_b):
    depth = w_in.shape[0]
    d = x_prompt.shape[-1]
    alpha = (2.0 * depth) ** 0.25

    xp, xs = x_prompt, x_sample
    conv_p_rows, conv_s_rows, v_rows = [], [], []
    for l in range(depth):
        row = lambda a: a.reshape(1, d)
        small = (row(a_ln_g[l]), row(a_ln_b[l]), w_conv[l].reshape(1, CONV_WIDTH * d),
                 row(b_conv[l]), row(ln_g[l]), row(ln_b[l]))
        xs, conv_s, v_s, weights_bf16, mod_p, wsm, bias = _sample_call(
            xs, state_conv[l], c_sample, c_prompt, w_c[l], b_c[l],
            w_in[l], w_pa[l], w_pb[l], w_o[l], w_s[l], b_s[l], *small, alpha)
        xp, conv_p = _prompt_call(xp, mod_p, *weights_bf16, wsm, bias, *small, alpha)
        conv_p_rows.append(conv_p)
        conv_s_rows.append(conv_s)
        v_rows.append(v_s)
    stack = (lambda rows: rows[0][None]) if depth == 1 else jnp.stack
    return (xp, xs, stack(conv_p_rows), stack(conv_s_rows), stack(v_rows))
```

```python
import functools

import jax
import jax.numpy as jnp
from jax import lax
from jax.experimental import pallas as pl
from jax.experimental.pallas import tpu as pltpu

CHUNK = 128
GROUPS = 8
CONV_WIDTH = 3
LN_EPS = 1e-5
ROW_TILE = 512
TILES_PER_STEP = 2
SUB_TILE = 256
STAGE_SLOTS = 4
VMEM_LIMIT_BYTES = 62 * 1024 * 1024

_F32 = jnp.float32
_BF16 = jnp.bfloat16


def _dot(a, b):
    return jnp.dot(a, b, preferred_element_type=_F32)


def _layer_norm(x, gain, bias):
    mu = jnp.mean(x, axis=-1, keepdims=True)
    xc = x - mu
    var = jnp.mean(xc * xc, axis=-1, keepdims=True)
    return xc * lax.rsqrt(var + LN_EPS) * gain + bias


def _silu(x):
    return x * jax.nn.sigmoid(x)


_PACKED = jnp.uint32


def _packed_shape(w):
    return (w.shape[0] // 2, w.shape[1])


def _as_bf16(words):
    return pltpu.bitcast(words, _BF16)


def _prompt_kernel(x_ref, mod_ref, win_ref, wpa_ref, wpb_ref, wo_ref, wsm_ref, bias_ref,
                   alg_ref, alb_ref, wconv_ref, bconv_ref, lng_ref, lnb_ref,
                   y_ref, conv_ref, z_scr, *, alpha, tiles_per_seq):
    d = x_ref.shape[1]
    for t in range(TILES_PER_STEP):
        _prompt_tile(
            x_ref, mod_ref, win_ref, wpa_ref, wpb_ref, wo_ref, wsm_ref, bias_ref,
            alg_ref, alb_ref, wconv_ref, bconv_ref, lng_ref, lnb_ref, y_ref, conv_ref, z_scr,
            alpha, tiles_per_seq, d, t, 0)


def _prompt_tile(x_ref, mod_ref, win_ref, wpa_ref, wpb_ref, wo_ref, wsm_ref, bias_ref,
                 alg_ref, alb_ref, wconv_ref, bconv_ref, lng_ref, lnb_ref, y_ref, conv_ref, z_scr,
                 alpha, tiles_per_seq, d, t, carry):
    tm = ROW_TILE
    sub = SUB_TILE
    gd = d // GROUPS
    j = pl.program_id(0) * TILES_PER_STEP + t
    base = t * tm

    def rows(r0):
        return pl.ds(base + r0, sub)

    mod = mod_ref[pl.ds(j // tiles_per_seq, 1), :]
    shift, scale, gate = mod[:, 0:d], mod[:, d:2 * d], mod[:, 2 * d:3 * d]
    bias = bias_ref[...]
    carried = jnp.where(j % tiles_per_seq == 0, 0.0, z_scr[...])
    wconv = [wconv_ref[:, k * d:(k + 1) * d] for k in range(CONV_WIDTH)]
    starts = range(0, tm, sub)
    tails = []

    def modulated(r0):
        return (x_ref[rows(r0), :] * (1.0 + scale) + shift).astype(_BF16)

    def project(h, ks):
        return [_dot(h, _as_bf16(win_ref[:, k * d:(k + 1) * d])) for k in ks]

    def mixer_a(u, v, z_a):
        vb = _layer_norm(v, alg_ref[...], alb_ref[...]).astype(_BF16)
        s_cols = []
        for g in range(GROUPS):
            cols = slice(g * gd, (g + 1) * gd)
            pair_rows = []
            for c in range(0, sub // CHUNK, 2):
                rhs = jnp.concatenate([vb[c * CHUNK:(c + 1) * CHUNK, cols],
                                       vb[(c + 1) * CHUNK:(c + 2) * CHUNK, cols]], axis=1)
                res = _dot(_as_bf16(wsm_ref[g]), rhs)
                pair_rows.append(res[:, :gd] + bias[:, cols])
                pair_rows.append(res[:, gd:] + bias[:, cols])
            s_cols.append(jnp.concatenate(pair_rows, axis=0))
        s = jnp.concatenate(s_cols, axis=1)
        return (u * s * _silu(z_a)).astype(_BF16)

    def mixer_b(b_g, c_g, h_b, z_b):
        z = c_g * h_b
        hist = carried if not tails else tails[-1]
        tails.append(z[sub - 8:sub, :])
        top_rows = lax.broadcasted_iota(jnp.int32, (8, d), 0)

        def delayed(k):
            rolled = pltpu.roll(z, k, axis=0)
            top = jnp.where(top_rows < k, pltpu.roll(hist, k, axis=0), rolled[0:8, :])
            return jnp.concatenate([top, rolled[8:, :]], axis=0)

        conv = bconv_ref[...] + wconv[0] * delayed(2) + wconv[1] * delayed(1) + wconv[2] * z
        return (b_g * conv * _silu(z_b)).astype(_BF16)

    def merge(out_a, out_b, g_a, g_b):
        return (jax.nn.sigmoid(g_a) * _dot(out_a, _as_bf16(wpa_ref[...]))
                + jax.nn.sigmoid(g_b) * _dot(out_b, _as_bf16(wpb_ref[...]))).astype(_BF16)

    def finish(r0, mg):
        x = x_ref[rows(r0), :]
        y_ref[rows(r0), :] = _layer_norm(
            alpha * x + (1.0 + gate) * _dot(mg, _as_bf16(wo_ref[...])), lng_ref[...], lnb_ref[...])

    r_a, r_b = starts
    u, v, z_a, b_g, c_g, h_b, z_b, g_a, g_b = project(modulated(r_a), range(9))
    hb = modulated(r_b)
    u2, v2, z_a2 = project(hb, range(0, 3))
    out_a = mixer_a(u, v, z_a)
    b_g2, c_g2, h_b2, z_b2 = project(hb, range(3, 7))
    out_b = mixer_b(b_g, c_g, h_b, z_b)
    g_a2, g_b2 = project(hb, range(7, 9))
    out_a2 = mixer_a(u2, v2, z_a2)
    mg = merge(out_a, out_b, g_a, g_b)
    out_b2 = mixer_b(b_g2, c_g2, h_b2, z_b2)
    finish(r_a, mg)
    finish(r_b, merge(out_a2, out_b2, g_a2, g_b2))

    z_scr[...] = tails[-1]
    conv_ref[0] = tails[-1][8 - (CONV_WIDTH - 1):8, :]
    return carry


def _resident(shape):
    nd = len(shape)
    return pl.BlockSpec(shape, lambda i: (0,) * nd, pipeline_mode=pl.Buffered(1))


def _prompt_call(x, mod, win, wpa, wpb, wo, wsm, bias, alg, alb, wconv, bconv, lng, lnb, alpha):
    nb, length, d = x.shape
    tm = ROW_TILE
    assert length % tm == 0 and tm == 2 * SUB_TILE and SUB_TILE % (2 * CHUNK) == 0
    assert d % GROUPS == 0 and d // GROUPS == CHUNK
    tiles_per_seq = length // tm
    assert tiles_per_seq % TILES_PER_STEP == 0
    steps_per_seq = tiles_per_seq // TILES_PER_STEP
    block = TILES_PER_STEP * tm
    small = [alg, alb, wconv, bconv, lng, lnb]
    y, conv = pl.pallas_call(
        functools.partial(_prompt_kernel, alpha=alpha, tiles_per_seq=tiles_per_seq),
        grid=(nb * steps_per_seq,),
        in_specs=[
            pl.BlockSpec((block, d), lambda i: (i, 0)),
            _resident(mod.shape),
            _resident(win.shape), _resident(wpa.shape), _resident(wpb.shape), _resident(wo.shape),
            _resident(wsm.shape), _resident(bias.shape),
        ] + [_resident(a.shape) for a in small],
        out_specs=[
            pl.BlockSpec((block, d), lambda i: (i, 0)),
            pl.BlockSpec((1, CONV_WIDTH - 1, d), lambda i: (i // steps_per_seq, 0, 0)),
        ],
        out_shape=[
            jax.ShapeDtypeStruct((nb * length, d), _F32),
            jax.ShapeDtypeStruct((nb, CONV_WIDTH - 1, d), _F32),
        ],
        scratch_shapes=[pltpu.VMEM((8, d), _F32)],
        compiler_params=pltpu.CompilerParams(
            dimension_semantics=("arbitrary",),
            vmem_limit_bytes=VMEM_LIMIT_BYTES),
        name="prompt_layer",
    )(x.reshape(nb * length, d), mod, win, wpa, wpb, wo, wsm, bias, *small)
    return y.reshape(nb, length, d), conv


_PROJ_ORDER = (1, 0, 2, 4, 5, 3, 6, 7, 8)


def _sample_kernel(bs_ref, ws_ref, cs_ref, cp_ref, bc_ref,
                   alg_ref, alb_ref, wconv_ref, bconv_ref, lng_ref, lnb_ref,
                   x_hbm, st_hbm, ws_hbm, wc_hbm, win_hbm, wpa_hbm, wpb_hbm, wo_hbm,
                   y_hbm, conv_hbm, v_hbm, win_out, wpa_out, wpb_out, wo_out, modp_ref,
                   wsm_ref, bias_ref,
                   win_ref, wpa_ref, wpb_ref, wo_ref, stage, xbuf, stbuf, ybuf, cvbuf, vbuf, ws_smem,
                   sem, exp_sem, in_sem, res_sem, ws_sem, *, alpha):
    steps, n, d = xbuf.shape
    gd = d // GROUPS
    hist_rows = CONV_WIDTH - 1

    ws_copy = pltpu.make_async_copy(ws_hbm.at[:, 0:ws_smem.shape[1], :], ws_smem, ws_sem.at[0])
    ws_copy.start()

    in_copies = ([pltpu.make_async_copy(x_hbm.at[:, t, :], xbuf.at[t], in_sem.at[t])
                  for t in range(steps)]
                 + [pltpu.make_async_copy(st_hbm.at[:, k, :], stbuf.at[k], in_sem.at[steps + k])
                    for k in range(hist_rows)])
    for cp in in_copies:
        cp.start()

    blocks = ([(wc_hbm, None, None, j) for j in range(wc_hbm.shape[1] // d)]
              + [(win_hbm, win_ref, win_out, k) for k in _PROJ_ORDER]
              + [(wpa_hbm, wpa_ref, wpa_out, 0), (wpb_hbm, wpb_ref, wpb_out, 0),
                 (wo_hbm, wo_ref, wo_out, 0)])
    exports = []
    position = [0]

    def stage_copy(i):
        src, _, _, k = blocks[i]
        slot = i % STAGE_SLOTS
        return pltpu.make_async_copy(src.at[:, k * d:(k + 1) * d], stage.at[slot], sem.at[slot])

    def next_weight():
        i = position[0]
        position[0] += 1
        stage_copy(i).wait()
        if i + STAGE_SLOTS - 1 < len(blocks):
            stage_copy(i + STAGE_SLOTS - 1).start()
        w = stage[i % STAGE_SLOTS].astype(_BF16)
        _, keep, out, k = blocks[i]
        if keep is not None:
            cols = slice(k * d, (k + 1) * d)
            keep[:, cols] = pltpu.bitcast(w, _PACKED)
            cp = pltpu.make_async_copy(keep.at[:, cols], out.at[:, cols], exp_sem.at[len(exports)])
            cp.start()
            exports.append(cp)
        return w

    for i in range(STAGE_SLOTS - 1):
        stage_copy(i).start()

    c_all = jnp.concatenate([cs_ref[...], cp_ref[...]], axis=0).astype(_BF16)
    bc = bc_ref[...]
    mods = []
    for j in range(3):
        m = _dot(c_all, next_weight()) + bc[:, j * d:(j + 1) * d]
        modp_ref[:, j * d:(j + 1) * d] = m[n:, :]
        mods.append(m[0:n, :])
    shift, scale, gate = mods

    row = lax.broadcasted_iota(jnp.int32, (CHUNK, CHUNK), 0)
    col = lax.broadcasted_iota(jnp.int32, (CHUNK, CHUNK), 1)
    for g in range(GROUPS):
        wsm_ref[g] = pltpu.bitcast(jnp.where(col <= row, ws_ref[g], 0.0).astype(_BF16), _PACKED)

    for t in range(CHUNK):
        for g in range(GROUPS):
            bias_ref[t:t + 1, g * gd:(g + 1) * gd] = jnp.full((1, gd), bs_ref[g, t], _F32)

    for cp in in_copies:
        cp.wait()
    ws_copy.wait()
    h = jnp.concatenate([(xbuf[t] * (1.0 + scale) + shift).astype(_BF16) for t in range(steps)],
                        axis=0)

    def rows(a, t):
        return a[t * n:(t + 1) * n, :]

    def proj():
        return _dot(h, next_weight())

    results = []

    def send(buf, t, dst):
        cp = pltpu.make_async_copy(buf.at[t], dst.at[:, t, :], res_sem.at[len(results)])
        cp.start()
        results.append(cp)

    vn = _layer_norm(proj(), alg_ref[...], alb_ref[...])
    s_rows = []
    for t in range(steps):
        vbuf[t] = rows(vn, t)
        send(vbuf, t, v_hbm)
        s_cols = []
        for g in range(GROUPS):
            cols = slice(g * gd, (g + 1) * gd)
            acc = jnp.full((n, gd), bs_ref[g, t], _F32)
            for jj in range(t + 1):
                acc = acc + ws_smem[g, t, jj] * rows(vn, jj)[:, cols]
            s_cols.append(acc)
        s_rows.append(jnp.concatenate(s_cols, axis=1))
    s = jnp.concatenate(s_rows, axis=0)
    u = proj()
    out_a = (u * s * _silu(proj())).astype(_BF16)

    z = proj()
    z = z * proj()
    hist = [stbuf[k] for k in range(hist_rows)] + [rows(z, t) for t in range(steps)]
    for k in range(hist_rows):
        cvbuf[k] = hist[steps + k]
        send(cvbuf, k, conv_hbm)
    wconv = [wconv_ref[:, k * d:(k + 1) * d] for k in range(CONV_WIDTH)]
    conv = jnp.concatenate(
        [bconv_ref[...] + sum(wconv[k] * hist[t + k] for k in range(CONV_WIDTH))
         for t in range(steps)], axis=0)
    b_g = proj()
    out_b = (b_g * conv * _silu(proj())).astype(_BF16)

    sg_a = jax.nn.sigmoid(proj())
    sg_b = jax.nn.sigmoid(proj())
    pa = _dot(out_a, next_weight())
    pb = _dot(out_b, next_weight())
    merged = (sg_a * pa + sg_b * pb).astype(_BF16)
    y = _dot(merged, next_weight())
    for t in range(steps):
        ybuf[t] = _layer_norm(alpha * xbuf[t] + (1.0 + gate) * rows(y, t), lng_ref[...], lnb_ref[...])
        send(ybuf, t, y_hbm)

    for cp in exports + results:
        cp.wait()


def _sample_call(x, state, c_s, c_p, w_c, b_c, win, wpa, wpb, wo, w_s, b_s,
                 alg, alb, wconv, bconv, lng, lnb, alpha):
    n, steps, d = x.shape
    nb = c_p.shape[0]
    hist_rows = CONV_WIDTH - 1
    assert hist_rows <= steps <= CHUNK and w_c.shape == (d, 3 * d)
    assert w_s.shape == (GROUPS, CHUNK, CHUNK) and d == GROUPS * CHUNK
    vmem = pl.BlockSpec(memory_space=pltpu.VMEM)
    smem = pl.BlockSpec(memory_space=pltpu.SMEM)
    hbm = pl.BlockSpec(memory_space=pl.ANY)
    weights = (win, wpa, wpb, wo)
    n_exports = sum(w.shape[1] // d for w in weights)
    n_results = 2 * steps + hist_rows
    ws_rows = -(-steps // 8) * 8
    y, conv, v, *weights_bf16, mod_p, wsm, bias = pl.pallas_call(
        functools.partial(_sample_kernel, alpha=alpha),
        in_specs=[smem] + [vmem] * 10 + [hbm] * 8,
        out_specs=[hbm] * (3 + len(weights)) + [vmem] * 3,
        out_shape=[
            jax.ShapeDtypeStruct((n, steps, d), _F32),
            jax.ShapeDtypeStruct((n, hist_rows, d), _F32),
            jax.ShapeDtypeStruct((n, steps, d), _F32),
        ] + [jax.ShapeDtypeStruct(_packed_shape(w), _PACKED) for w in weights] + [
            jax.ShapeDtypeStruct((nb, 3 * d), _F32),
            jax.ShapeDtypeStruct((GROUPS, CHUNK // 2, CHUNK), _PACKED),
            jax.ShapeDtypeStruct((CHUNK, d), _F32)],
        scratch_shapes=[pltpu.VMEM(_packed_shape(w), _PACKED) for w in weights] + [
            pltpu.VMEM((STAGE_SLOTS, d, d), _F32),
            pltpu.VMEM((steps, n, d), _F32), pltpu.VMEM((hist_rows, n, d), _F32),
            pltpu.VMEM((steps, n, d), _F32), pltpu.VMEM((hist_rows, n, d), _F32),
            pltpu.VMEM((steps, n, d), _F32),
            pltpu.SMEM((w_s.shape[0], ws_rows, w_s.shape[2]), _F32),
            pltpu.SemaphoreType.DMA((STAGE_SLOTS,)), pltpu.SemaphoreType.DMA((n_exports,)),
            pltpu.SemaphoreType.DMA((steps + hist_rows,)), pltpu.SemaphoreType.DMA((n_results,)),
            pltpu.SemaphoreType.DMA((1,))],
        compiler_params=pltpu.CompilerParams(vmem_limit_bytes=VMEM_LIMIT_BYTES),
        name="sample_layer",
    )(b_s, w_s, c_s, c_p, b_c.reshape(1, 3 * d), alg, alb, wconv, bconv, lng, lnb,
      x, state, w_s, w_c, win, wpa, wpb, wo)
    return y, conv, v, weights_bf16, mod_p, wsm, bias


def kernel(x_prompt, x_sample, state_conv, c_prompt, c_sample, w_c, b_c, w_in, a_ln_g, a_ln_b,
           w_s, b_s, w_conv, b_conv, w_pa, w_pb, w_o, ln_g, ln_b):
    depth = w_in.shape[0]
    d = x_prompt.shape[-1]
    alpha = (2.0 * depth) ** 0.25

    xp, xs = x_prompt, x_sample
    conv_p_rows, conv_s_rows, v_rows = [], [], []
    for l in range(depth):
        row = lambda a: a.reshape(1, d)
        small = (row(a_ln_g[l]), row(a_ln_b[l]), w_conv[l].reshape(1, CONV_WIDTH * d),
                 row(b_conv[l]), row(ln_g[l]), row(ln_b[l]))
        xs, conv_s, v_s, weights_bf16, mod_p, wsm, bias = _sample_call(
            xs, state_conv[l], c_sample, c_prompt, w_c[l], b_c[l],
            w_in[l], w_pa[l], w_pb[l], w_o[l], w_s[l], b_s[l], *small, alpha)
        xp, conv_p = _prompt_call(xp, mod_p, *weights_bf16, wsm, bias, *small, alpha)
        conv_p_rows.append(conv_p)
        conv_s_rows.append(conv_s)
        v_rows.append(v_s)
    stack = (lambda rows: rows[0][None]) if depth == 1 else jnp.stack
    return (xp, xs, stack(conv_p_rows), stack(conv_s_rows), stack(v_rows))
```

```python
import functools

import jax
import jax.numpy as jnp
from jax import lax
from jax.experimental import pallas as pl
from jax.experimental.pallas import tpu as pltpu

CHUNK = 128
GROUPS = 8
CONV_WIDTH = 3
LN_EPS = 1e-5
ROW_TILE = 512
SUB_TILE = 256
STAGE_SLOTS = 5
VMEM_LIMIT_BYTES = 62 * 1024 * 1024

_F32 = jnp.float32
_BF16 = jnp.bfloat16


def _dot(a, b):
    return jnp.dot(a, b, preferred_element_type=_F32)


def _layer_norm(x, gain, bias):
    mu = jnp.mean(x, axis=-1, keepdims=True)
    xc = x - mu
    var = jnp.mean(xc * xc, axis=-1, keepdims=True)
    return xc * lax.rsqrt(var + LN_EPS) * gain + bias


def _silu(x):
    return x * jax.nn.sigmoid(x)


_PACKED = jnp.uint32


def _packed_shape(w):
    return (w.shape[0] // 2, w.shape[1])


def _as_bf16(words):
    return pltpu.bitcast(words, _BF16)


def _prompt_kernel(x_ref, mod_ref, win_ref, wpa_ref, wpb_ref, wo_ref, wsm_ref, bias_ref,
                   alg_ref, alb_ref, wconv_ref, bconv_ref, lng_ref, lnb_ref,
                   y_ref, conv_ref, z_scr, *, alpha, steps_per_seq):
    tm, d = x_ref.shape
    sub = SUB_TILE
    gd = d // GROUPS
    i = pl.program_id(0)

    mod = mod_ref[pl.ds(i // steps_per_seq, 1), :]
    shift, scale, gate = mod[:, 0:d], mod[:, d:2 * d], mod[:, 2 * d:3 * d]
    bias = bias_ref[...]
    carried = jnp.where(i % steps_per_seq == 0, 0.0, z_scr[...])
    wconv = [wconv_ref[:, k * d:(k + 1) * d] for k in range(CONV_WIDTH)]
    starts = range(0, tm, sub)
    tails = []

    def modulated(r0):
        return (x_ref[r0:r0 + sub, :] * (1.0 + scale) + shift).astype(_BF16)

    def project(h, ks):
        return [_dot(h, _as_bf16(win_ref[:, k * d:(k + 1) * d])) for k in ks]

    def mixer_a(u, v, z_a):
        vb = _layer_norm(v, alg_ref[...], alb_ref[...]).astype(_BF16)
        s_cols = []
        for g in range(GROUPS):
            cols = slice(g * gd, (g + 1) * gd)
            pair_rows = []
            for c in range(0, sub // CHUNK, 2):
                rhs = jnp.concatenate([vb[c * CHUNK:(c + 1) * CHUNK, cols],
                                       vb[(c + 1) * CHUNK:(c + 2) * CHUNK, cols]], axis=1)
                res = _dot(_as_bf16(wsm_ref[g]), rhs)
                pair_rows.append(res[:, :gd] + bias[:, cols])
                pair_rows.append(res[:, gd:] + bias[:, cols])
            s_cols.append(jnp.concatenate(pair_rows, axis=0))
        s = jnp.concatenate(s_cols, axis=1)
        return (u * s * _silu(z_a)).astype(_BF16)

    def mixer_b(b_g, c_g, h_b, z_b):
        z = c_g * h_b
        hist = carried if not tails else tails[-1]
        tails.append(z[sub - 8:sub, :])
        top_rows = lax.broadcasted_iota(jnp.int32, (8, d), 0)

        def delayed(k):
            rolled = pltpu.roll(z, k, axis=0)
            top = jnp.where(top_rows < k, pltpu.roll(hist, k, axis=0), rolled[0:8, :])
            return jnp.concatenate([top, rolled[8:, :]], axis=0)

        conv = bconv_ref[...] + wconv[0] * delayed(2) + wconv[1] * delayed(1) + wconv[2] * z
        return (b_g * conv * _silu(z_b)).astype(_BF16)

    def merge(out_a, out_b, g_a, g_b):
        return (jax.nn.sigmoid(g_a) * _dot(out_a, _as_bf16(wpa_ref[...]))
                + jax.nn.sigmoid(g_b) * _dot(out_b, _as_bf16(wpb_ref[...]))).astype(_BF16)

    def finish(r0, mg):
        x = x_ref[r0:r0 + sub, :]
        y_ref[r0:r0 + sub, :] = _layer_norm(
            alpha * x + (1.0 + gate) * _dot(mg, _as_bf16(wo_ref[...])), lng_ref[...], lnb_ref[...])

    r_a, r_b = starts
    u, v, z_a, b_g, c_g, h_b, z_b, g_a, g_b = project(modulated(r_a), range(9))
    hb = modulated(r_b)
    u2, v2, z_a2 = project(hb, range(0, 3))
    out_a = mixer_a(u, v, z_a)
    b_g2, c_g2, h_b2, z_b2 = project(hb, range(3, 7))
    out_b = mixer_b(b_g, c_g, h_b, z_b)
    g_a2, g_b2 = project(hb, range(7, 9))
    out_a2 = mixer_a(u2, v2, z_a2)
    mg = merge(out_a, out_b, g_a, g_b)
    out_b2 = mixer_b(b_g2, c_g2, h_b2, z_b2)
    finish(r_a, mg)
    finish(r_b, merge(out_a2, out_b2, g_a2, g_b2))

    z_scr[...] = tails[-1]
    conv_ref[0] = tails[-1][8 - (CONV_WIDTH - 1):8, :]


def _resident(shape):
    nd = len(shape)
    return pl.BlockSpec(shape, lambda i: (0,) * nd, pipeline_mode=pl.Buffered(1))


def _prompt_call(x, mod, win, wpa, wpb, wo, wsm, bias, alg, alb, wconv, bconv, lng, lnb, alpha):
    nb, length, d = x.shape
    tm = ROW_TILE
    assert length % tm == 0 and tm == 2 * SUB_TILE and SUB_TILE % (2 * CHUNK) == 0
    assert d % GROUPS == 0 and d // GROUPS == CHUNK
    steps_per_seq = length // tm
    small = [alg, alb, wconv, bconv, lng, lnb]
    y, conv = pl.pallas_call(
        functools.partial(_prompt_kernel, alpha=alpha, steps_per_seq=steps_per_seq),
        grid=(nb * steps_per_seq,),
        in_specs=[
            pl.BlockSpec((tm, d), lambda i: (i, 0)),
            _resident(mod.shape),
            _resident(win.shape), _resident(wpa.shape), _resident(wpb.shape), _resident(wo.shape),
            _resident(wsm.shape), _resident(bias.shape),
        ] + [_resident(a.shape) for a in small],
        out_specs=[
            pl.BlockSpec((tm, d), lambda i: (i, 0)),
            pl.BlockSpec((1, CONV_WIDTH - 1, d), lambda i: (i // steps_per_seq, 0, 0)),
        ],
        out_shape=[
            jax.ShapeDtypeStruct((nb * length, d), _F32),
            jax.ShapeDtypeStruct((nb, CONV_WIDTH - 1, d), _F32),
        ],
        scratch_shapes=[pltpu.VMEM((8, d), _F32)],
        compiler_params=pltpu.CompilerParams(
            dimension_semantics=("arbitrary",),
            vmem_limit_bytes=VMEM_LIMIT_BYTES),
        name="prompt_layer",
    )(x.reshape(nb * length, d), mod, win, wpa, wpb, wo, wsm, bias, *small)
    return y.reshape(nb, length, d), conv


_PROJ_ORDER = (1, 0, 2, 4, 5, 3, 6, 7, 8)


def _sample_kernel(bs_ref, ws_ref, cs_ref, cp_ref, bc_ref,
                   alg_ref, alb_ref, wconv_ref, bconv_ref, lng_ref, lnb_ref,
                   x_hbm, st_hbm, ws_hbm, wc_hbm, win_hbm, wpa_hbm, wpb_hbm, wo_hbm,
                   y_hbm, conv_hbm, v_hbm, win_out, wpa_out, wpb_out, wo_out, modp_ref,
                   wsm_ref, bias_ref,
                   win_ref, wpa_ref, wpb_ref, wo_ref, stage, xbuf, stbuf, ybuf, cvbuf, vbuf, ws_smem,
                   sem, exp_sem, in_sem, res_sem, ws_sem, *, alpha):
    steps, n, d = xbuf.shape
    gd = d // GROUPS
    hist_rows = CONV_WIDTH - 1

    ws_copy = pltpu.make_async_copy(ws_hbm.at[:, 0:ws_smem.shape[1], :], ws_smem, ws_sem.at[0])
    ws_copy.start()

    in_copies = ([pltpu.make_async_copy(x_hbm.at[:, t, :], xbuf.at[t], in_sem.at[t])
                  for t in range(steps)]
                 + [pltpu.make_async_copy(st_hbm.at[:, k, :], stbuf.at[k], in_sem.at[steps + k])
                    for k in range(hist_rows)])
    for cp in in_copies:
        cp.start()

    blocks = ([(wc_hbm, None, None, j) for j in range(wc_hbm.shape[1] // d)]
              + [(win_hbm, win_ref, win_out, k) for k in _PROJ_ORDER]
              + [(wpa_hbm, wpa_ref, wpa_out, 0), (wpb_hbm, wpb_ref, wpb_out, 0),
                 (wo_hbm, wo_ref, wo_out, 0)])
    exports = []
    position = [0]

    def stage_copy(i):
        src, _, _, k = blocks[i]
        slot = i % STAGE_SLOTS
        return pltpu.make_async_copy(src.at[:, k * d:(k + 1) * d], stage.at[slot], sem.at[slot])

    def next_weight():
        i = position[0]
        position[0] += 1
        stage_copy(i).wait()
        if i + STAGE_SLOTS - 1 < len(blocks):
            stage_copy(i + STAGE_SLOTS - 1).start()
        w = stage[i % STAGE_SLOTS].astype(_BF16)
        _, keep, out, k = blocks[i]
        if keep is not None:
            cols = slice(k * d, (k + 1) * d)
            keep[:, cols] = pltpu.bitcast(w, _PACKED)
            cp = pltpu.make_async_copy(keep.at[:, cols], out.at[:, cols], exp_sem.at[len(exports)])
            cp.start()
            exports.append(cp)
        return w

    for i in range(STAGE_SLOTS - 1):
        stage_copy(i).start()

    c_all = jnp.concatenate([cs_ref[...], cp_ref[...]], axis=0).astype(_BF16)
    bc = bc_ref[...]
    mods = []
    for j in range(3):
        m = _dot(c_all, next_weight()) + bc[:, j * d:(j + 1) * d]
        modp_ref[:, j * d:(j + 1) * d] = m[n:, :]
        mods.append(m[0:n, :])
    shift, scale, gate = mods

    row = lax.broadcasted_iota(jnp.int32, (CHUNK, CHUNK), 0)
    col = lax.broadcasted_iota(jnp.int32, (CHUNK, CHUNK), 1)
    for g in range(GROUPS):
        wsm_ref[g] = pltpu.bitcast(jnp.where(col <= row, ws_ref[g], 0.0).astype(_BF16), _PACKED)

    for t in range(CHUNK):
        for g in range(GROUPS):
            bias_ref[t:t + 1, g * gd:(g + 1) * gd] = jnp.full((1, gd), bs_ref[g, t], _F32)

    for cp in in_copies:
        cp.wait()
    ws_copy.wait()
    h = jnp.concatenate([(xbuf[t] * (1.0 + scale) + shift).astype(_BF16) for t in range(steps)],
                        axis=0)

    def rows(a, t):
        return a[t * n:(t + 1) * n, :]

    def proj():
        return _dot(h, next_weight())

    results = []

    def send(buf, t, dst):
        cp = pltpu.make_async_copy(buf.at[t], dst.at[:, t, :], res_sem.at[len(results)])
        cp.start()
        results.append(cp)

    vn = _layer_norm(proj(), alg_ref[...], alb_ref[...])
    s_rows = []
    for t in range(steps):
        vbuf[t] = rows(vn, t)
        send(vbuf, t, v_hbm)
        s_cols = []
        for g in range(GROUPS):
            cols = slice(g * gd, (g + 1) * gd)
            acc = jnp.full((n, gd), bs_ref[g, t], _F32)
            for jj in range(t + 1):
                acc = acc + ws_smem[g, t, jj] * rows(vn, jj)[:, cols]
            s_cols.append(acc)
        s_rows.append(jnp.concatenate(s_cols, axis=1))
    s = jnp.concatenate(s_rows, axis=0)
    u = proj()
    out_a = (u * s * _silu(proj())).astype(_BF16)

    z = proj()
    z = z * proj()
    hist = [stbuf[k] for k in range(hist_rows)] + [rows(z, t) for t in range(steps)]
    for k in range(hist_rows):
        cvbuf[k] = hist[steps + k]
        send(cvbuf, k, conv_hbm)
    wconv = [wconv_ref[:, k * d:(k + 1) * d] for k in range(CONV_WIDTH)]
    conv = jnp.concatenate(
        [bconv_ref[...] + sum(wconv[k] * hist[t + k] for k in range(CONV_WIDTH))
         for t in range(steps)], axis=0)
    b_g = proj()
    out_b = (b_g * conv * _silu(proj())).astype(_BF16)

    sg_a = jax.nn.sigmoid(proj())
    sg_b = jax.nn.sigmoid(proj())
    pa = _dot(out_a, next_weight())
    pb = _dot(out_b, next_weight())
    merged = (sg_a * pa + sg_b * pb).astype(_BF16)
    y = _dot(merged, next_weight())
    for t in range(steps):
        ybuf[t] = _layer_norm(alpha * xbuf[t] + (1.0 + gate) * rows(y, t), lng_ref[...], lnb_ref[...])
        send(ybuf, t, y_hbm)

    for cp in exports + results:
        cp.wait()


def _sample_call(x, state, c_s, c_p, w_c, b_c, win, wpa, wpb, wo, w_s, b_s,
                 alg, alb, wconv, bconv, lng, lnb, alpha):
    n, steps, d = x.shape
    nb = c_p.shape[0]
    hist_rows = CONV_WIDTH - 1
    assert hist_rows <= steps <= CHUNK and w_c.shape == (d, 3 * d)
    assert w_s.shape == (GROUPS, CHUNK, CHUNK) and d == GROUPS * CHUNK
    vmem = pl.BlockSpec(memory_space=pltpu.VMEM)
    smem = pl.BlockSpec(memory_space=pltpu.SMEM)
    hbm = pl.BlockSpec(memory_space=pl.ANY)
    weights = (win, wpa, wpb, wo)
    n_exports = sum(w.shape[1] // d for w in weights)
    n_results = 2 * steps + hist_rows
    ws_rows = -(-steps // 8) * 8
    y, conv, v, *weights_bf16, mod_p, wsm, bias = pl.pallas_call(
        functools.partial(_sample_kernel, alpha=alpha),
        in_specs=[smem] + [vmem] * 10 + [hbm] * 8,
        out_specs=[hbm] * (3 + len(weights)) + [vmem] * 3,
        out_shape=[
            jax.ShapeDtypeStruct((n, steps, d), _F32),
            jax.ShapeDtypeStruct((n, hist_rows, d), _F32),
            jax.ShapeDtypeStruct((n, steps, d), _F32),
        ] + [jax.ShapeDtypeStruct(_packed_shape(w), _PACKED) for w in weights] + [
            jax.ShapeDtypeStruct((nb, 3 * d), _F32),
            jax.ShapeDtypeStruct((GROUPS, CHUNK // 2, CHUNK), _PACKED),
            jax.ShapeDtypeStruct((CHUNK, d), _F32)],
        scratch_shapes=[pltpu.VMEM(_packed_shape(w), _PACKED) for w in weights] + [
            pltpu.VMEM((STAGE_SLOTS, d, d), _F32),
            pltpu.VMEM((steps, n, d), _F32), pltpu.VMEM((hist_rows, n, d), _F32),
            pltpu.VMEM((steps, n, d), _F32), pltpu.VMEM((hist_rows, n, d), _F32),
            pltpu.VMEM((steps, n, d), _F32),
            pltpu.SMEM((w_s.shape[0], ws_rows, w_s.shape[2]), _F32),
            pltpu.SemaphoreType.DMA((STAGE_SLOTS,)), pltpu.SemaphoreType.DMA((n_exports,)),
            pltpu.SemaphoreType.DMA((steps + hist_rows,)), pltpu.SemaphoreType.DMA((n_results,)),
            pltpu.SemaphoreType.DMA((1,))],
        compiler_params=pltpu.CompilerParams(vmem_limit_bytes=VMEM_LIMIT_BYTES),
        name="sample_layer",
    )(b_s, w_s, c_s, c_p, b_c.reshape(1, 3 * d), alg, alb, wconv, bconv, lng, lnb,
      x, state, w_s, w_c, win, wpa, wpb, wo)
    return y, conv, v, weights_bf16, mod_p, wsm, bias


def kernel(x_prompt, x_sample, state_conv, c_prompt, c_sample, w_c, b_c, w_in, a_ln_g, a_ln_b,
           w_s, b_s, w_conv, b_conv, w_pa, w_pb, w_o, ln_g, ln_b):
    depth = w_in.shape[0]
    d = x_prompt.shape[-1]
    alpha = (2.0 * depth) ** 0.25

    xp, xs = x_prompt, x_sample
    conv_p_rows, conv_s_rows, v_rows = [], [], []
    for l in range(depth):
        row = lambda a: a.reshape(1, d)
        small = (row(a_ln_g[l]), row(a_ln_b[l]), w_conv[l].reshape(1, CONV_WIDTH * d),
                 row(b_conv[l]), row(ln_g[l]), row(ln_b[l]))
        xs, conv_s, v_s, weights_bf16, mod_p, wsm, bias = _sample_call(
            xs, state_conv[l], c_sample, c_prompt, w_c[l], b_c[l],
            w_in[l], w_pa[l], w_pb[l], w_o[l], w_s[l], b_s[l], *small, alpha)
        xp, conv_p = _prompt_call(xp, mod_p, *weights_bf16, wsm, bias, *small, alpha)
        conv_p_rows.append(conv_p)
        conv_s_rows.append(conv_s)
        v_rows.append(v_s)
    stack = (lambda rows: rows[0][None]) if depth == 1 else jnp.stack
    return (xp, xs, stack(conv_p_rows), stack(conv_s_rows), stack(v_rows))
```

```python
import functools

import jax
import jax.numpy as jnp
from jax import lax
from jax.experimental import pallas as pl
from jax.experimental.pallas import tpu as pltpu

CHUNK = 128
GROUPS = 8
CONV_WIDTH = 3
LN_EPS = 1e-5
ROW_TILE = 512
SUB_TILE = 256
STAGE_SLOTS = 4
VMEM_LIMIT_BYTES = 62 * 1024 * 1024

_F32 = jnp.float32
_BF16 = jnp.bfloat16


def _dot(a, b):
    return jnp.dot(a, b, preferred_element_type=_F32)


def _layer_norm(x, gain, bias):
    mu = jnp.mean(x, axis=-1, keepdims=True)
    xc = x - mu
    var = jnp.mean(xc * xc, axis=-1, keepdims=True)
    return xc * lax.rsqrt(var + LN_EPS) * gain + bias


def _silu(x):
    return x * jax.nn.sigmoid(x)


_PACKED = jnp.uint32


def _packed_shape(w):
    return (w.shape[0] // 2, w.shape[1])


def _as_bf16(words):
    return pltpu.bitcast(words, _BF16)


def _prompt_kernel(x_ref, mod_ref, win_ref, wpa_ref, wpb_ref, wo_ref, wsm_ref, bias_ref,
                   alg_ref, alb_ref, wconv_ref, bconv_ref, lng_ref, lnb_ref,
                   y_ref, conv_ref, z_scr, *, alpha, steps_per_seq):
    tm, d = x_ref.shape
    sub = SUB_TILE
    gd = d // GROUPS
    i = pl.program_id(0)

    mod = mod_ref[pl.ds(i // steps_per_seq, 1), :]
    shift, scale, gate = mod[:, 0:d], mod[:, d:2 * d], mod[:, 2 * d:3 * d]
    bias = bias_ref[...]
    carried = jnp.where(i % steps_per_seq == 0, 0.0, z_scr[...])
    wconv = [wconv_ref[:, k * d:(k + 1) * d] for k in range(CONV_WIDTH)]
    starts = range(0, tm, sub)
    tails = []

    def modulated(r0):
        return (x_ref[r0:r0 + sub, :] * (1.0 + scale) + shift).astype(_BF16)

    def project(h, ks):
        return [_dot(h, _as_bf16(win_ref[:, k * d:(k + 1) * d])) for k in ks]

    def mixer_a(u, v, z_a):
        vb = _layer_norm(v, alg_ref[...], alb_ref[...]).astype(_BF16)
        s_cols = []
        for g in range(GROUPS):
            cols = slice(g * gd, (g + 1) * gd)
            pair_rows = []
            for c in range(0, sub // CHUNK, 2):
                rhs = jnp.concatenate([vb[c * CHUNK:(c + 1) * CHUNK, cols],
                                       vb[(c + 1) * CHUNK:(c + 2) * CHUNK, cols]], axis=1)
                res = _dot(_as_bf16(wsm_ref[g]), rhs)
                pair_rows.append(res[:, :gd] + bias[:, cols])
                pair_rows.append(res[:, gd:] + bias[:, cols])
            s_cols.append(jnp.concatenate(pair_rows, axis=0))
        s = jnp.concatenate(s_cols, axis=1)
        return (u * s * _silu(z_a)).astype(_BF16)

    def mixer_b(b_g, c_g, h_b, z_b):
        z = c_g * h_b
        hist = carried if not tails else tails[-1]
        tails.append(z[sub - 8:sub, :])
        top_rows = lax.broadcasted_iota(jnp.int32, (8, d), 0)

        def delayed(k):
            rolled = pltpu.roll(z, k, axis=0)
            top = jnp.where(top_rows < k, pltpu.roll(hist, k, axis=0), rolled[0:8, :])
            return jnp.concatenate([top, rolled[8:, :]], axis=0)

        conv = bconv_ref[...] + wconv[0] * delayed(2) + wconv[1] * delayed(1) + wconv[2] * z
        return (b_g * conv * _silu(z_b)).astype(_BF16)

    def merge(out_a, out_b, g_a, g_b):
        return (jax.nn.sigmoid(g_a) * _dot(out_a, _as_bf16(wpa_ref[...]))
                + jax.nn.sigmoid(g_b) * _dot(out_b, _as_bf16(wpb_ref[...]))).astype(_BF16)

    def finish(r0, mg):
        x = x_ref[r0:r0 + sub, :]
        y_ref[r0:r0 + sub, :] = _layer_norm(
            alpha * x + (1.0 + gate) * _dot(mg, _as_bf16(wo_ref[...])), lng_ref[...], lnb_ref[...])

    r_a, r_b = starts
    u, v, z_a, b_g, c_g, h_b, z_b, g_a, g_b = project(modulated(r_a), range(9))
    hb = modulated(r_b)
    u2, v2, z_a2 = project(hb, range(0, 3))
    out_a = mixer_a(u, v, z_a)
    b_g2, c_g2, h_b2, z_b2 = project(hb, range(3, 7))
    out_b = mixer_b(b_g, c_g, h_b, z_b)
    g_a2, g_b2 = project(hb, range(7, 9))
    out_a2 = mixer_a(u2, v2, z_a2)
    mg = merge(out_a, out_b, g_a, g_b)
    out_b2 = mixer_b(b_g2, c_g2, h_b2, z_b2)
    finish(r_a, mg)
    finish(r_b, merge(out_a2, out_b2, g_a2, g_b2))

    z_scr[...] = tails[-1]
    conv_ref[0] = tails[-1][8 - (CONV_WIDTH - 1):8, :]


def _resident(shape):
    nd = len(shape)
    return pl.BlockSpec(shape, lambda i: (0,) * nd, pipeline_mode=pl.Buffered(1))


def _prompt_call(x, mod, win, wpa, wpb, wo, wsm, bias, alg, alb, wconv, bconv, lng, lnb, alpha):
    nb, length, d = x.shape
    tm = ROW_TILE
    assert length % tm == 0 and tm == 2 * SUB_TILE and SUB_TILE % (2 * CHUNK) == 0
    assert d % GROUPS == 0 and d // GROUPS == CHUNK
    steps_per_seq = length // tm
    small = [alg, alb, wconv, bconv, lng, lnb]
    y, conv = pl.pallas_call(
        functools.partial(_prompt_kernel, alpha=alpha, steps_per_seq=steps_per_seq),
        grid=(nb * steps_per_seq,),
        in_specs=[
            pl.BlockSpec((tm, d), lambda i: (i, 0)),
            _resident(mod.shape),
            _resident(win.shape), _resident(wpa.shape), _resident(wpb.shape), _resident(wo.shape),
            _resident(wsm.shape), _resident(bias.shape),
        ] + [_resident(a.shape) for a in small],
        out_specs=[
            pl.BlockSpec((tm, d), lambda i: (i, 0)),
            pl.BlockSpec((1, CONV_WIDTH - 1, d), lambda i: (i // steps_per_seq, 0, 0)),
        ],
        out_shape=[
            jax.ShapeDtypeStruct((nb * length, d), _F32),
            jax.ShapeDtypeStruct((nb, CONV_WIDTH - 1, d), _F32),
        ],
        scratch_shapes=[pltpu.VMEM((8, d), _F32)],
        compiler_params=pltpu.CompilerParams(
            dimension_semantics=("arbitrary",),
            vmem_limit_bytes=VMEM_LIMIT_BYTES),
        name="prompt_layer",
    )(x.reshape(nb * length, d), mod, win, wpa, wpb, wo, wsm, bias, *small)
    return y.reshape(nb, length, d), conv


_PROJ_ORDER = (1, 0, 2, 4, 5, 3, 6, 7, 8)


def _sample_kernel(bs_ref, ws_ref, cs_ref, cp_ref, bc_ref,
                   alg_ref, alb_ref, wconv_ref, bconv_ref, lng_ref, lnb_ref,
                   x_hbm, st_hbm, ws_hbm, wc_hbm, win_hbm, wpa_hbm, wpb_hbm, wo_hbm,
                   y_hbm, conv_hbm, v_hbm, win_out, wpa_out, wpb_out, wo_out, modp_ref,
                   wsm_ref, bias_ref,
                   win_ref, wpa_ref, wpb_ref, wo_ref, stage, xbuf, stbuf, ybuf, cvbuf, vbuf, ws_smem,
                   sem, exp_sem, in_sem, res_sem, ws_sem, *, alpha):
    steps, n, d = xbuf.shape
    gd = d // GROUPS
    hist_rows = CONV_WIDTH - 1

    ws_copy = pltpu.make_async_copy(ws_hbm.at[:, 0:ws_smem.shape[1], :], ws_smem, ws_sem.at[0])

    in_copies = ([pltpu.make_async_copy(x_hbm.at[:, t, :], xbuf.at[t], in_sem.at[t])
                  for t in range(steps)]
                 + [pltpu.make_async_copy(st_hbm.at[:, k, :], stbuf.at[k], in_sem.at[steps + k])
                    for k in range(hist_rows)])

    blocks = ([(wc_hbm, None, None, j) for j in range(wc_hbm.shape[1] // d)]
              + [(win_hbm, win_ref, win_out, k) for k in _PROJ_ORDER]
              + [(wpa_hbm, wpa_ref, wpa_out, 0), (wpb_hbm, wpb_ref, wpb_out, 0),
                 (wo_hbm, wo_ref, wo_out, 0)])
    exports = []
    position = [0]

    def stage_copy(i):
        src, _, _, k = blocks[i]
        slot = i % STAGE_SLOTS
        return pltpu.make_async_copy(src.at[:, k * d:(k + 1) * d], stage.at[slot], sem.at[slot])

    def next_weight():
        i = position[0]
        position[0] += 1
        stage_copy(i).wait()
        if i + STAGE_SLOTS - 1 < len(blocks):
            stage_copy(i + STAGE_SLOTS - 1).start()
        w = stage[i % STAGE_SLOTS].astype(_BF16)
        _, keep, out, k = blocks[i]
        if keep is not None:
            cols = slice(k * d, (k + 1) * d)
            keep[:, cols] = pltpu.bitcast(w, _PACKED)
            cp = pltpu.make_async_copy(keep.at[:, cols], out.at[:, cols], exp_sem.at[len(exports)])
            cp.start()
            exports.append(cp)
        return w

    for i in range(STAGE_SLOTS - 1):
        stage_copy(i).start()
    for cp in in_copies:
        cp.start()
    ws_copy.start()

    c_all = jnp.concatenate([cs_ref[...], cp_ref[...]], axis=0).astype(_BF16)
    bc = bc_ref[...]
    mods = []
    for j in range(3):
        m = _dot(c_all, next_weight()) + bc[:, j * d:(j + 1) * d]
        modp_ref[:, j * d:(j + 1) * d] = m[n:, :]
        mods.append(m[0:n, :])
    shift, scale, gate = mods

    row = lax.broadcasted_iota(jnp.int32, (CHUNK, CHUNK), 0)
    col = lax.broadcasted_iota(jnp.int32, (CHUNK, CHUNK), 1)
    for g in range(GROUPS):
        wsm_ref[g] = pltpu.bitcast(jnp.where(col <= row, ws_ref[g], 0.0).astype(_BF16), _PACKED)

    for t in range(CHUNK):
        for g in range(GROUPS):
            bias_ref[t:t + 1, g * gd:(g + 1) * gd] = jnp.full((1, gd), bs_ref[g, t], _F32)

    for cp in in_copies:
        cp.wait()
    ws_copy.wait()
    h = jnp.concatenate([(xbuf[t] * (1.0 + scale) + shift).astype(_BF16) for t in range(steps)],
                        axis=0)

    def rows(a, t):
        return a[t * n:(t + 1) * n, :]

    def proj():
        return _dot(h, next_weight())

    results = []

    def send(buf, t, dst):
        cp = pltpu.make_async_copy(buf.at[t], dst.at[:, t, :], res_sem.at[len(results)])
        cp.start()
        results.append(cp)

    vn = _layer_norm(proj(), alg_ref[...], alb_ref[...])
    s_rows = []
    for t in range(steps):
        vbuf[t] = rows(vn, t)
        send(vbuf, t, v_hbm)
        s_cols = []
        for g in range(GROUPS):
            cols = slice(g * gd, (g + 1) * gd)
            acc = jnp.full((n, gd), bs_ref[g, t], _F32)
            for jj in range(t + 1):
                acc = acc + ws_smem[g, t, jj] * rows(vn, jj)[:, cols]
            s_cols.append(acc)
        s_rows.append(jnp.concatenate(s_cols, axis=1))
    s = jnp.concatenate(s_rows, axis=0)
    u = proj()
    out_a = (u * s * _silu(proj())).astype(_BF16)

    z = proj()
    z = z * proj()
    hist = [stbuf[k] for k in range(hist_rows)] + [rows(z, t) for t in range(steps)]
    for k in range(hist_rows):
        cvbuf[k] = hist[steps + k]
        send(cvbuf, k, conv_hbm)
    wconv = [wconv_ref[:, k * d:(k + 1) * d] for k in range(CONV_WIDTH)]
    conv = jnp.concatenate(
        [bconv_ref[...] + sum(wconv[k] * hist[t + k] for k in range(CONV_WIDTH))
         for t in range(steps)], axis=0)
    b_g = proj()
    out_b = (b_g * conv * _silu(proj())).astype(_BF16)

    sg_a = jax.nn.sigmoid(proj())
    sg_b = jax.nn.sigmoid(proj())
    pa = _dot(out_a, next_weight())
    pb = _dot(out_b, next_weight())
    merged = (sg_a * pa + sg_b * pb).astype(_BF16)
    y = _dot(merged, next_weight())
    for t in range(steps):
        ybuf[t] = _layer_norm(alpha * xbuf[t] + (1.0 + gate) * rows(y, t), lng_ref[...], lnb_ref[...])
        send(ybuf, t, y_hbm)

    for cp in exports + results:
        cp.wait()


def _sample_call(x, state, c_s, c_p, w_c, b_c, win, wpa, wpb, wo, w_s, b_s,
                 alg, alb, wconv, bconv, lng, lnb, alpha):
    n, steps, d = x.shape
    nb = c_p.shape[0]
    hist_rows = CONV_WIDTH - 1
    assert hist_rows <= steps <= CHUNK and w_c.shape == (d, 3 * d)
    assert w_s.shape == (GROUPS, CHUNK, CHUNK) and d == GROUPS * CHUNK
    vmem = pl.BlockSpec(memory_space=pltpu.VMEM)
    smem = pl.BlockSpec(memory_space=pltpu.SMEM)
    hbm = pl.BlockSpec(memory_space=pl.ANY)
    weights = (win, wpa, wpb, wo)
    n_exports = sum(w.shape[1] // d for w in weights)
    n_results = 2 * steps + hist_rows
    ws_rows = -(-steps // 8) * 8
    y, conv, v, *weights_bf16, mod_p, wsm, bias = pl.pallas_call(
        functools.partial(_sample_kernel, alpha=alpha),
        in_specs=[smem] + [vmem] * 10 + [hbm] * 8,
        out_specs=[hbm] * (3 + len(weights)) + [vmem] * 3,
        out_shape=[
            jax.ShapeDtypeStruct((n, steps, d), _F32),
            jax.ShapeDtypeStruct((n, hist_rows, d), _F32),
            jax.ShapeDtypeStruct((n, steps, d), _F32),
        ] + [jax.ShapeDtypeStruct(_packed_shape(w), _PACKED) for w in weights] + [
            jax.ShapeDtypeStruct((nb, 3 * d), _F32),
            jax.ShapeDtypeStruct((GROUPS, CHUNK // 2, CHUNK), _PACKED),
            jax.ShapeDtypeStruct((CHUNK, d), _F32)],
        scratch_shapes=[pltpu.VMEM(_packed_shape(w), _PACKED) for w in weights] + [
            pltpu.VMEM((STAGE_SLOTS, d, d), _F32),
            pltpu.VMEM((steps, n, d), _F32), pltpu.VMEM((hist_rows, n, d), _F32),
            pltpu.VMEM((steps, n, d), _F32), pltpu.VMEM((hist_rows, n, d), _F32),
            pltpu.VMEM((steps, n, d), _F32),
            pltpu.SMEM((w_s.shape[0], ws_rows, w_s.shape[2]), _F32),
            pltpu.SemaphoreType.DMA((STAGE_SLOTS,)), pltpu.SemaphoreType.DMA((n_exports,)),
            pltpu.SemaphoreType.DMA((steps + hist_rows,)), pltpu.SemaphoreType.DMA((n_results,)),
            pltpu.SemaphoreType.DMA((1,))],
        compiler_params=pltpu.CompilerParams(vmem_limit_bytes=VMEM_LIMIT_BYTES),
        name="sample_layer",
    )(b_s, w_s, c_s, c_p, b_c.reshape(1, 3 * d), alg, alb, wconv, bconv, lng, lnb,
      x, state, w_s, w_c, win, wpa, wpb, wo)
    return y, conv, v, weights_bf16, mod_p, wsm, bias


def kernel(x_prompt, x_sample, state_conv, c_prompt, c_sample, w_c, b_c, w_in, a_ln_g, a_ln_b,
           w_s, b_s, w_conv, b_conv, w_pa, w_pb, w_o, ln_g, ln_b):
    depth = w_in.shape[0]
    d = x_prompt.shape[-1]
    alpha = (2.0 * depth) ** 0.25

    xp, xs = x_prompt, x_sample
    conv_p_rows, conv_s_rows, v_rows = [], [], []
    for l in range(depth):
        row = lambda a: a.reshape(1, d)
        small = (row(a_ln_g[l]), row(a_ln_b[l]), w_conv[l].reshape(1, CONV_WIDTH * d),
                 row(b_conv[l]), row(ln_g[l]), row(ln_b[l]))
        xs, conv_s, v_s, weights_bf16, mod_p, wsm, bias = _sample_call(
            xs, state_conv[l], c_sample, c_prompt, w_c[l], b_c[l],
            w_in[l], w_pa[l], w_pb[l], w_o[l], w_s[l], b_s[l], *small, alpha)
        xp, conv_p = _prompt_call(xp, mod_p, *weights_bf16, wsm, bias, *small, alpha)
        conv_p_rows.append(conv_p)
        conv_s_rows.append(conv_s)
        v_rows.append(v_s)
    stack = (lambda rows: rows[0][None]) if depth == 1 else jnp.stack
    return (xp, xs, stack(conv_p_rows), stack(conv_s_rows), stack(v_rows))
```

```python
import functools

import jax
import jax.numpy as jnp
from jax import lax
from jax.experimental import pallas as pl
from jax.experimental.pallas import tpu as pltpu

CHUNK = 128
GROUPS = 8
CONV_WIDTH = 3
LN_EPS = 1e-5
ROW_TILE = 512
SUB_TILE = 256
STAGE_SLOTS = 4
VMEM_LIMIT_BYTES = 62 * 1024 * 1024

_F32 = jnp.float32
_BF16 = jnp.bfloat16


def _dot(a, b):
    return jnp.dot(a, b, preferred_element_type=_F32)


def _layer_norm(x, gain, bias):
    mu = jnp.mean(x, axis=-1, keepdims=True)
    xc = x - mu
    var = jnp.mean(xc * xc, axis=-1, keepdims=True)
    return xc * lax.rsqrt(var + LN_EPS) * gain + bias


def _silu(x):
    return x * jax.nn.sigmoid(x)


_PACKED = jnp.uint32


def _packed_shape(w):
    return (w.shape[0] // 2, w.shape[1])


def _as_bf16(words):
    return pltpu.bitcast(words, _BF16)


def _prompt_kernel(x_ref, mod_ref, win_ref, wpa_ref, wpb_ref, wo_ref, wsm_ref, bias_ref,
                   alg_ref, alb_ref, wconv_ref, bconv_ref, lng_ref, lnb_ref,
                   y_ref, conv_ref, z_scr, *, alpha, steps_per_seq):
    tm, d = x_ref.shape
    sub = SUB_TILE
    gd = d // GROUPS
    i = pl.program_id(0)

    mod = mod_ref[pl.ds(i // steps_per_seq, 1), :]
    shift, scale, gate = mod[:, 0:d], mod[:, d:2 * d], mod[:, 2 * d:3 * d]
    bias = bias_ref[...]
    carried = jnp.where(i % steps_per_seq == 0, 0.0, z_scr[...])
    wconv = [wconv_ref[:, k * d:(k + 1) * d] for k in range(CONV_WIDTH)]
    starts = range(0, tm, sub)
    tails = []

    def modulated(r0):
        return (x_ref[r0:r0 + sub, :] * (1.0 + scale) + shift).astype(_BF16)

    def project(h, ks):
        return [_dot(h, _as_bf16(win_ref[:, k * d:(k + 1) * d])) for k in ks]

    def mixer_a(u, v, z_a):
        vb = _layer_norm(v, alg_ref[...], alb_ref[...]).astype(_BF16)
        s_cols = []
        for g in range(GROUPS):
            cols = slice(g * gd, (g + 1) * gd)
            pair_rows = []
            for c in range(0, sub // CHUNK, 2):
                rhs = jnp.concatenate([vb[c * CHUNK:(c + 1) * CHUNK, cols],
                                       vb[(c + 1) * CHUNK:(c + 2) * CHUNK, cols]], axis=1)
                res = _dot(_as_bf16(wsm_ref[g]), rhs)
                pair_rows.append(res[:, :gd] + bias[:, cols])
                pair_rows.append(res[:, gd:] + bias[:, cols])
            s_cols.append(jnp.concatenate(pair_rows, axis=0))
        s = jnp.concatenate(s_cols, axis=1)
        return (u * s * _silu(z_a)).astype(_BF16)

    def mixer_b(b_g, c_g, h_b, z_b):
        z = c_g * h_b
        hist = carried if not tails else tails[-1]
        tails.append(z[sub - 8:sub, :])
        top_rows = lax.broadcasted_iota(jnp.int32, (8, d), 0)

        def delayed(k):
            rolled = pltpu.roll(z, k, axis=0)
            top = jnp.where(top_rows < k, pltpu.roll(hist, k, axis=0), rolled[0:8, :])
            return jnp.concatenate([top, rolled[8:, :]], axis=0)

        conv = bconv_ref[...] + wconv[0] * delayed(2) + wconv[1] * delayed(1) + wconv[2] * z
        return (b_g * conv * _silu(z_b)).astype(_BF16)

    def merge(out_a, out_b, g_a, g_b):
        return (jax.nn.sigmoid(g_a) * _dot(out_a, _as_bf16(wpa_ref[...]))
                + jax.nn.sigmoid(g_b) * _dot(out_b, _as_bf16(wpb_ref[...]))).astype(_BF16)

    def finish(r0, mg):
        x = x_ref[r0:r0 + sub, :]
        y_ref[r0:r0 + sub, :] = _layer_norm(
            alpha * x + (1.0 + gate) * _dot(mg, _as_bf16(wo_ref[...])), lng_ref[...], lnb_ref[...])

    r_a, r_b = starts
    u, v, z_a, b_g, c_g, h_b, z_b, g_a, g_b = project(modulated(r_a), range(9))
    hb = modulated(r_b)
    u2, v2, z_a2 = project(hb, range(0, 3))
    out_a = mixer_a(u, v, z_a)
    b_g2, c_g2, h_b2, z_b2 = project(hb, range(3, 7))
    out_b = mixer_b(b_g, c_g, h_b, z_b)
    g_a2, g_b2 = project(hb, range(7, 9))
    out_a2 = mixer_a(u2, v2, z_a2)
    mg = merge(out_a, out_b, g_a, g_b)
    out_b2 = mixer_b(b_g2, c_g2, h_b2, z_b2)
    finish(r_a, mg)
    finish(r_b, merge(out_a2, out_b2, g_a2, g_b2))

    z_scr[...] = tails[-1]
    conv_ref[0] = tails[-1][8 - (CONV_WIDTH - 1):8, :]


def _resident(shape):
    nd = len(shape)
    return pl.BlockSpec(shape, lambda i: (0,) * nd, pipeline_mode=pl.Buffered(1))


def _prompt_call(x, mod, win, wpa, wpb, wo, wsm, bias, alg, alb, wconv, bconv, lng, lnb, alpha):
    nb, length, d = x.shape
    tm = ROW_TILE
    assert length % tm == 0 and tm == 2 * SUB_TILE and SUB_TILE % (2 * CHUNK) == 0
    assert d % GROUPS == 0 and d // GROUPS == CHUNK
    steps_per_seq = length // tm
    small = [alg, alb, wconv, bconv, lng, lnb]
    y, conv = pl.pallas_call(
        functools.partial(_prompt_kernel, alpha=alpha, steps_per_seq=steps_per_seq),
        grid=(nb * steps_per_seq,),
        in_specs=[
            pl.BlockSpec((tm, d), lambda i: (i, 0)),
            _resident(mod.shape),
            _resident(win.shape), _resident(wpa.shape), _resident(wpb.shape), _resident(wo.shape),
            _resident(wsm.shape), _resident(bias.shape),
        ] + [_resident(a.shape) for a in small],
        out_specs=[
            pl.BlockSpec((tm, d), lambda i: (i, 0)),
            pl.BlockSpec((1, CONV_WIDTH - 1, d), lambda i: (i // steps_per_seq, 0, 0)),
        ],
        out_shape=[
            jax.ShapeDtypeStruct((nb * length, d), _F32),
            jax.ShapeDtypeStruct((nb, CONV_WIDTH - 1, d), _F32),
        ],
        scratch_shapes=[pltpu.VMEM((8, d), _F32)],
        compiler_params=pltpu.CompilerParams(
            dimension_semantics=("arbitrary",),
            vmem_limit_bytes=VMEM_LIMIT_BYTES),
        name="prompt_layer",
    )(x.reshape(nb * length, d), mod, win, wpa, wpb, wo, wsm, bias, *small)
    return y.reshape(nb, length, d), conv


_PROJ_ORDER = (1, 0, 2, 4, 5, 3, 6, 7, 8)


def _sample_kernel(bs_ref, ws_ref, cs_ref, cp_ref, bc_ref,
                   alg_ref, alb_ref, wconv_ref, bconv_ref, lng_ref, lnb_ref,
                   x_hbm, st_hbm, ws_hbm, wc_hbm, win_hbm, wpa_hbm, wpb_hbm, wo_hbm,
                   y_hbm, conv_hbm, v_hbm, win_out, wpa_out, wpb_out, wo_out, modp_ref,
                   wsm_ref, bias_ref,
                   win_ref, wpa_ref, wpb_ref, wo_ref, stage, xbuf, stbuf, ybuf, cvbuf, vbuf, ws_smem,
                   sem, exp_sem, in_sem, res_sem, ws_sem, *, alpha):
    steps, n, d = xbuf.shape
    gd = d // GROUPS
    hist_rows = CONV_WIDTH - 1

    ws_copy = pltpu.make_async_copy(ws_hbm.at[:, 0:ws_smem.shape[1], :], ws_smem, ws_sem.at[0])
    ws_copy.start()

    in_copies = ([pltpu.make_async_copy(x_hbm.at[:, t, :], xbuf.at[t], in_sem.at[t])
                  for t in range(steps)]
                 + [pltpu.make_async_copy(st_hbm.at[:, k, :], stbuf.at[k], in_sem.at[steps + k])
                    for k in range(hist_rows)])
    for cp in in_copies:
        cp.start()

    blocks = ([(wc_hbm, None, None, j) for j in range(wc_hbm.shape[1] // d)]
              + [(win_hbm, win_ref, win_out, k) for k in _PROJ_ORDER]
              + [(wpa_hbm, wpa_ref, wpa_out, 0), (wpb_hbm, wpb_ref, wpb_out, 0),
                 (wo_hbm, wo_ref, wo_out, 0)])
    exports = []
    position = [0]

    def stage_copy(i):
        src, _, _, k = blocks[i]
        slot = i % STAGE_SLOTS
        return pltpu.make_async_copy(src.at[:, k * d:(k + 1) * d], stage.at[slot], sem.at[slot])

    def next_weight():
        i = position[0]
        position[0] += 1
        stage_copy(i).wait()
        if i + STAGE_SLOTS - 1 < len(blocks):
            stage_copy(i + STAGE_SLOTS - 1).start()
        w = stage[i % STAGE_SLOTS].astype(_BF16)
        _, keep, out, k = blocks[i]
        if keep is not None:
            cols = slice(k * d, (k + 1) * d)
            keep[:, cols] = pltpu.bitcast(w, _PACKED)
            cp = pltpu.make_async_copy(keep.at[:, cols], out.at[:, cols], exp_sem.at[len(exports)])
            cp.start(priority=1)
            exports.append(cp)
        return w

    for i in range(STAGE_SLOTS - 1):
        stage_copy(i).start()

    c_all = jnp.concatenate([cs_ref[...], cp_ref[...]], axis=0).astype(_BF16)
    bc = bc_ref[...]
    mods = []
    for j in range(3):
        m = _dot(c_all, next_weight()) + bc[:, j * d:(j + 1) * d]
        modp_ref[:, j * d:(j + 1) * d] = m[n:, :]
        mods.append(m[0:n, :])
    shift, scale, gate = mods

    row = lax.broadcasted_iota(jnp.int32, (CHUNK, CHUNK), 0)
    col = lax.broadcasted_iota(jnp.int32, (CHUNK, CHUNK), 1)
    for g in range(GROUPS):
        wsm_ref[g] = pltpu.bitcast(jnp.where(col <= row, ws_ref[g], 0.0).astype(_BF16), _PACKED)

    for t in range(CHUNK):
        for g in range(GROUPS):
            bias_ref[t:t + 1, g * gd:(g + 1) * gd] = jnp.full((1, gd), bs_ref[g, t], _F32)

    for cp in in_copies:
        cp.wait()
    ws_copy.wait()
    h = jnp.concatenate([(xbuf[t] * (1.0 + scale) + shift).astype(_BF16) for t in range(steps)],
                        axis=0)

    def rows(a, t):
        return a[t * n:(t + 1) * n, :]

    def proj():
        return _dot(h, next_weight())

    results = []

    def send(buf, t, dst):
        cp = pltpu.make_async_copy(buf.at[t], dst.at[:, t, :], res_sem.at[len(results)])
        cp.start()
        results.append(cp)

    vn = _layer_norm(proj(), alg_ref[...], alb_ref[...])
    s_rows = []
    for t in range(steps):
        vbuf[t] = rows(vn, t)
        send(vbuf, t, v_hbm)
        s_cols = []
        for g in range(GROUPS):
            cols = slice(g * gd, (g + 1) * gd)
            acc = jnp.full((n, gd), bs_ref[g, t], _F32)
            for jj in range(t + 1):
                acc = acc + ws_smem[g, t, jj] * rows(vn, jj)[:, cols]
            s_cols.append(acc)
        s_rows.append(jnp.concatenate(s_cols, axis=1))
    s = jnp.concatenate(s_rows, axis=0)
    u = proj()
    out_a = (u * s * _silu(proj())).astype(_BF16)

    z = proj()
    z = z * proj()
    hist = [stbuf[k] for k in range(hist_rows)] + [rows(z, t) for t in range(steps)]
    for k in range(hist_rows):
        cvbuf[k] = hist[steps + k]
        send(cvbuf, k, conv_hbm)
    wconv = [wconv_ref[:, k * d:(k + 1) * d] for k in range(CONV_WIDTH)]
    conv = jnp.concatenate(
        [bconv_ref[...] + sum(wconv[k] * hist[t + k] for k in range(CONV_WIDTH))
         for t in range(steps)], axis=0)
    b_g = proj()
    out_b = (b_g * conv * _silu(proj())).astype(_BF16)

    sg_a = jax.nn.sigmoid(proj())
    sg_b = jax.nn.sigmoid(proj())
    pa = _dot(out_a, next_weight())
    pb = _dot(out_b, next_weight())
    merged = (sg_a * pa + sg_b * pb).astype(_BF16)
    y = _dot(merged, next_weight())
    for t in range(steps):
        ybuf[t] = _layer_norm(alpha * xbuf[t] + (1.0 + gate) * rows(y, t), lng_ref[...], lnb_ref[...])
        send(ybuf, t, y_hbm)

    for cp in exports + results:
        cp.wait()


def _sample_call(x, state, c_s, c_p, w_c, b_c, win, wpa, wpb, wo, w_s, b_s,
                 alg, alb, wconv, bconv, lng, lnb, alpha):
    n, steps, d = x.shape
    nb = c_p.shape[0]
    hist_rows = CONV_WIDTH - 1
    assert hist_rows <= steps <= CHUNK and w_c.shape == (d, 3 * d)
    assert w_s.shape == (GROUPS, CHUNK, CHUNK) and d == GROUPS * CHUNK
    vmem = pl.BlockSpec(memory_space=pltpu.VMEM)
    smem = pl.BlockSpec(memory_space=pltpu.SMEM)
    hbm = pl.BlockSpec(memory_space=pl.ANY)
    weights = (win, wpa, wpb, wo)
    n_exports = sum(w.shape[1] // d for w in weights)
    n_results = 2 * steps + hist_rows
    ws_rows = -(-steps // 8) * 8
    y, conv, v, *weights_bf16, mod_p, wsm, bias = pl.pallas_call(
        functools.partial(_sample_kernel, alpha=alpha),
        in_specs=[smem] + [vmem] * 10 + [hbm] * 8,
        out_specs=[hbm] * (3 + len(weights)) + [vmem] * 3,
        out_shape=[
            jax.ShapeDtypeStruct((n, steps, d), _F32),
            jax.ShapeDtypeStruct((n, hist_rows, d), _F32),
            jax.ShapeDtypeStruct((n, steps, d), _F32),
        ] + [jax.ShapeDtypeStruct(_packed_shape(w), _PACKED) for w in weights] + [
            jax.ShapeDtypeStruct((nb, 3 * d), _F32),
            jax.ShapeDtypeStruct((GROUPS, CHUNK // 2, CHUNK), _PACKED),
            jax.ShapeDtypeStruct((CHUNK, d), _F32)],
        scratch_shapes=[pltpu.VMEM(_packed_shape(w), _PACKED) for w in weights] + [
            pltpu.VMEM((STAGE_SLOTS, d, d), _F32),
            pltpu.VMEM((steps, n, d), _F32), pltpu.VMEM((hist_rows, n, d), _F32),
            pltpu.VMEM((steps, n, d), _F32), pltpu.VMEM((hist_rows, n, d), _F32),
            pltpu.VMEM((steps, n, d), _F32),
            pltpu.SMEM((w_s.shape[0], ws_rows, w_s.shape[2]), _F32),
            pltpu.SemaphoreType.DMA((STAGE_SLOTS,)), pltpu.SemaphoreType.DMA((n_exports,)),
            pltpu.SemaphoreType.DMA((steps + hist_rows,)), pltpu.SemaphoreType.DMA((n_results,)),
            pltpu.SemaphoreType.DMA((1,))],
        compiler_params=pltpu.CompilerParams(vmem_limit_bytes=VMEM_LIMIT_BYTES),
        name="sample_layer",
    )(b_s, w_s, c_s, c_p, b_c.reshape(1, 3 * d), alg, alb, wconv, bconv, lng, lnb,
      x, state, w_s, w_c, win, wpa, wpb, wo)
    return y, conv, v, weights_bf16, mod_p, wsm, bias


def kernel(x_prompt, x_sample, state_conv, c_prompt, c_sample, w_c, b_c, w_in, a_ln_g, a_ln_b,
           w_s, b_s, w_conv, b_conv, w_pa, w_pb, w_o, ln_g, ln_b):
    depth = w_in.shape[0]
    d = x_prompt.shape[-1]
    alpha = (2.0 * depth) ** 0.25

    xp, xs = x_prompt, x_sample
    conv_p_rows, conv_s_rows, v_rows = [], [], []
    for l in range(depth):
        row = lambda a: a.reshape(1, d)
        small = (row(a_ln_g[l]), row(a_ln_b[l]), w_conv[l].reshape(1, CONV_WIDTH * d),
                 row(b_conv[l]), row(ln_g[l]), row(ln_b[l]))
        xs, conv_s, v_s, weights_bf16, mod_p, wsm, bias = _sample_call(
            xs, state_conv[l], c_sample, c_prompt, w_c[l], b_c[l],
            w_in[l], w_pa[l], w_pb[l], w_o[l], w_s[l], b_s[l], *small, alpha)
        xp, conv_p = _prompt_call(xp, mod_p, *weights_bf16, wsm, bias, *small, alpha)
        conv_p_rows.append(conv_p)
        conv_s_rows.append(conv_s)
        v_rows.append(v_s)
    stack = (lambda rows: rows[0][None]) if depth == 1 else jnp.stack
    return (xp, xs, stack(conv_p_rows), stack(conv_s_rows), stack(v_rows))
```

```python
import functools

import jax
import jax.numpy as jnp
from jax import lax
from jax.experimental import pallas as pl
from jax.experimental.pallas import tpu as pltpu

CHUNK = 128
GROUPS = 8
CONV_WIDTH = 3
LN_EPS = 1e-5
ROW_TILE = 512
SUB_TILE = 256
STAGE_SLOTS = 4
VMEM_LIMIT_BYTES = 62 * 1024 * 1024

_F32 = jnp.float32
_BF16 = jnp.bfloat16


def _dot(a, b):
    return jnp.dot(a, b, preferred_element_type=_F32)


def _layer_norm(x, gain, bias):
    mu = jnp.mean(x, axis=-1, keepdims=True)
    xc = x - mu
    var = jnp.mean(xc * xc, axis=-1, keepdims=True)
    return xc * lax.rsqrt(var + LN_EPS) * gain + bias


def _silu(x):
    return x * jax.nn.sigmoid(x)


_PACKED = jnp.uint32


def _packed_shape(w):
    return (w.shape[0] // 2, w.shape[1])


def _as_bf16(words):
    return pltpu.bitcast(words, _BF16)


def _prompt_kernel(x_hbm, mod_ref, win_ref, wpa_ref, wpb_ref, wo_ref, wsm_ref, bias_ref,
                   alg_ref, alb_ref, wconv_ref, bconv_ref, lng_ref, lnb_ref,
                   y_hbm, conv_ref, z_scr, *, alpha, steps_per_seq, n_steps):
    d = x_hbm.shape[1]
    tile_spec = pl.BlockSpec((ROW_TILE, d), lambda i: (i, 0))
    step = functools.partial(
        _prompt_step, mod_ref, win_ref, wpa_ref, wpb_ref, wo_ref, wsm_ref, bias_ref,
        alg_ref, alb_ref, wconv_ref, bconv_ref, lng_ref, lnb_ref, conv_ref, z_scr,
        alpha, steps_per_seq)
    pltpu.emit_pipeline(step, grid=(n_steps,), in_specs=[tile_spec], out_specs=[tile_spec],
                        _explicit_indices=True)(x_hbm, y_hbm)


def _prompt_step(mod_ref, win_ref, wpa_ref, wpb_ref, wo_ref, wsm_ref, bias_ref,
                 alg_ref, alb_ref, wconv_ref, bconv_ref, lng_ref, lnb_ref, conv_ref, z_scr,
                 alpha, steps_per_seq, indices, x_ref, y_ref):
    tm, d = x_ref.shape
    sub = SUB_TILE
    gd = d // GROUPS
    i = indices[0]

    mod = mod_ref[pl.ds(i // steps_per_seq, 1), :]
    shift, scale, gate = mod[:, 0:d], mod[:, d:2 * d], mod[:, 2 * d:3 * d]
    bias = bias_ref[...]
    carried = jnp.where(i % steps_per_seq == 0, 0.0, z_scr[...])
    wconv = [wconv_ref[:, k * d:(k + 1) * d] for k in range(CONV_WIDTH)]
    starts = range(0, tm, sub)
    tails = []

    def modulated(r0):
        return (x_ref[r0:r0 + sub, :] * (1.0 + scale) + shift).astype(_BF16)

    def project(h, ks):
        return [_dot(h, _as_bf16(win_ref[:, k * d:(k + 1) * d])) for k in ks]

    def mixer_a(u, v, z_a):
        vb = _layer_norm(v, alg_ref[...], alb_ref[...]).astype(_BF16)
        s_cols = []
        for g in range(GROUPS):
            cols = slice(g * gd, (g + 1) * gd)
            pair_rows = []
            for c in range(0, sub // CHUNK, 2):
                rhs = jnp.concatenate([vb[c * CHUNK:(c + 1) * CHUNK, cols],
                                       vb[(c + 1) * CHUNK:(c + 2) * CHUNK, cols]], axis=1)
                res = _dot(_as_bf16(wsm_ref[g]), rhs)
                pair_rows.append(res[:, :gd] + bias[:, cols])
                pair_rows.append(res[:, gd:] + bias[:, cols])
            s_cols.append(jnp.concatenate(pair_rows, axis=0))
        s = jnp.concatenate(s_cols, axis=1)
        return (u * s * _silu(z_a)).astype(_BF16)

    def mixer_b(b_g, c_g, h_b, z_b):
        z = c_g * h_b
        hist = carried if not tails else tails[-1]
        tails.append(z[sub - 8:sub, :])
        top_rows = lax.broadcasted_iota(jnp.int32, (8, d), 0)

        def delayed(k):
            rolled = pltpu.roll(z, k, axis=0)
            top = jnp.where(top_rows < k, pltpu.roll(hist, k, axis=0), rolled[0:8, :])
            return jnp.concatenate([top, rolled[8:, :]], axis=0)

        conv = bconv_ref[...] + wconv[0] * delayed(2) + wconv[1] * delayed(1) + wconv[2] * z
        return (b_g * conv * _silu(z_b)).astype(_BF16)

    def merge(out_a, out_b, g_a, g_b):
        return (jax.nn.sigmoid(g_a) * _dot(out_a, _as_bf16(wpa_ref[...]))
                + jax.nn.sigmoid(g_b) * _dot(out_b, _as_bf16(wpb_ref[...]))).astype(_BF16)

    def finish(r0, mg):
        x = x_ref[r0:r0 + sub, :]
        y_ref[r0:r0 + sub, :] = _layer_norm(
            alpha * x + (1.0 + gate) * _dot(mg, _as_bf16(wo_ref[...])), lng_ref[...], lnb_ref[...])

    r_a, r_b = starts
    u, v, z_a, b_g, c_g, h_b, z_b, g_a, g_b = project(modulated(r_a), range(9))
    hb = modulated(r_b)
    u2, v2, z_a2 = project(hb, range(0, 3))
    out_a = mixer_a(u, v, z_a)
    b_g2, c_g2, h_b2, z_b2 = project(hb, range(3, 7))
    out_b = mixer_b(b_g, c_g, h_b, z_b)
    g_a2, g_b2 = project(hb, range(7, 9))
    out_a2 = mixer_a(u2, v2, z_a2)
    mg = merge(out_a, out_b, g_a, g_b)
    out_b2 = mixer_b(b_g2, c_g2, h_b2, z_b2)
    finish(r_a, mg)
    finish(r_b, merge(out_a2, out_b2, g_a2, g_b2))

    z_scr[...] = tails[-1]
    conv_ref[i // steps_per_seq] = tails[-1][8 - (CONV_WIDTH - 1):8, :]


def _prompt_call(x, mod, win, wpa, wpb, wo, wsm, bias, alg, alb, wconv, bconv, lng, lnb, alpha):
    nb, length, d = x.shape
    tm = ROW_TILE
    assert length % tm == 0 and tm == 2 * SUB_TILE and SUB_TILE % (2 * CHUNK) == 0
    assert d % GROUPS == 0 and d // GROUPS == CHUNK
    steps_per_seq = length // tm
    small = [alg, alb, wconv, bconv, lng, lnb]
    vmem = pl.BlockSpec(memory_space=pltpu.VMEM)
    hbm = pl.BlockSpec(memory_space=pl.ANY)
    y, conv = pl.pallas_call(
        functools.partial(_prompt_kernel, alpha=alpha, steps_per_seq=steps_per_seq,
                          n_steps=nb * steps_per_seq),
        in_specs=[hbm] + [vmem] * (7 + len(small)),
        out_specs=[hbm, vmem],
        out_shape=[
            jax.ShapeDtypeStruct((nb * length, d), _F32),
            jax.ShapeDtypeStruct((nb, CONV_WIDTH - 1, d), _F32),
        ],
        scratch_shapes=[pltpu.VMEM((8, d), _F32)],
        compiler_params=pltpu.CompilerParams(vmem_limit_bytes=VMEM_LIMIT_BYTES),
        name="prompt_layer",
    )(x.reshape(nb * length, d), mod, win, wpa, wpb, wo, wsm, bias, *small)
    return y.reshape(nb, length, d), conv


_PROJ_ORDER = (1, 0, 2, 4, 5, 3, 6, 7, 8)


def _sample_kernel(bs_ref, ws_ref, cs_ref, cp_ref, bc_ref,
                   alg_ref, alb_ref, wconv_ref, bconv_ref, lng_ref, lnb_ref,
                   x_hbm, st_hbm, ws_hbm, wc_hbm, win_hbm, wpa_hbm, wpb_hbm, wo_hbm,
                   y_hbm, conv_hbm, v_hbm, win_out, wpa_out, wpb_out, wo_out, modp_ref,
                   wsm_ref, bias_ref,
                   win_ref, wpa_ref, wpb_ref, wo_ref, stage, xbuf, stbuf, ybuf, cvbuf, vbuf, ws_smem,
                   sem, exp_sem, in_sem, res_sem, ws_sem, *, alpha):
    steps, n, d = xbuf.shape
    gd = d // GROUPS
    hist_rows = CONV_WIDTH - 1

    ws_copy = pltpu.make_async_copy(ws_hbm.at[:, 0:ws_smem.shape[1], :], ws_smem, ws_sem.at[0])
    ws_copy.start()

    in_copies = ([pltpu.make_async_copy(x_hbm.at[:, t, :], xbuf.at[t], in_sem.at[t])
                  for t in range(steps)]
                 + [pltpu.make_async_copy(st_hbm.at[:, k, :], stbuf.at[k], in_sem.at[steps + k])
                    for k in range(hist_rows)])
    for cp in in_copies:
        cp.start()

    blocks = ([(wc_hbm, None, None, j) for j in range(wc_hbm.shape[1] // d)]
              + [(win_hbm, win_ref, win_out, k) for k in _PROJ_ORDER]
              + [(wpa_hbm, wpa_ref, wpa_out, 0), (wpb_hbm, wpb_ref, wpb_out, 0),
                 (wo_hbm, wo_ref, wo_out, 0)])
    exports = []
    position = [0]

    def stage_copy(i):
        src, _, _, k = blocks[i]
        slot = i % STAGE_SLOTS
        return pltpu.make_async_copy(src.at[:, k * d:(k + 1) * d], stage.at[slot], sem.at[slot])

    def next_weight():
        i = position[0]
        position[0] += 1
        stage_copy(i).wait()
        if i + STAGE_SLOTS - 1 < len(blocks):
            stage_copy(i + STAGE_SLOTS - 1).start()
        w = stage[i % STAGE_SLOTS].astype(_BF16)
        _, keep, out, k = blocks[i]
        if keep is not None:
            cols = slice(k * d, (k + 1) * d)
            keep[:, cols] = pltpu.bitcast(w, _PACKED)
            cp = pltpu.make_async_copy(keep.at[:, cols], out.at[:, cols], exp_sem.at[len(exports)])
            cp.start()
            exports.append(cp)
        return w

    for i in range(STAGE_SLOTS - 1):
        stage_copy(i).start()

    c_all = jnp.concatenate([cs_ref[...], cp_ref[...]], axis=0).astype(_BF16)
    bc = bc_ref[...]
    mods = []
    for j in range(3):
        m = _dot(c_all, next_weight()) + bc[:, j * d:(j + 1) * d]
        modp_ref[:, j * d:(j + 1) * d] = m[n:, :]
        mods.append(m[0:n, :])
    shift, scale, gate = mods

    row = lax.broadcasted_iota(jnp.int32, (CHUNK, CHUNK), 0)
    col = lax.broadcasted_iota(jnp.int32, (CHUNK, CHUNK), 1)
    for g in range(GROUPS):
        wsm_ref[g] = pltpu.bitcast(jnp.where(col <= row, ws_ref[g], 0.0).astype(_BF16), _PACKED)

    for t in range(CHUNK):
        for g in range(GROUPS):
            bias_ref[t:t + 1, g * gd:(g + 1) * gd] = jnp.full((1, gd), bs_ref[g, t], _F32)

    for cp in in_copies:
        cp.wait()
    ws_copy.wait()
    h = jnp.concatenate([(xbuf[t] * (1.0 + scale) + shift).astype(_BF16) for t in range(steps)],
                        axis=0)

    def rows(a, t):
        return a[t * n:(t + 1) * n, :]

    def proj():
        return _dot(h, next_weight())

    results = []

    def send(buf, t, dst):
        cp = pltpu.make_async_copy(buf.at[t], dst.at[:, t, :], res_sem.at[len(results)])
        cp.start()
        results.append(cp)

    vn = _layer_norm(proj(), alg_ref[...], alb_ref[...])
    s_rows = []
    for t in range(steps):
        vbuf[t] = rows(vn, t)
        send(vbuf, t, v_hbm)
        s_cols = []
        for g in range(GROUPS):
            cols = slice(g * gd, (g + 1) * gd)
            acc = jnp.full((n, gd), bs_ref[g, t], _F32)
            for jj in range(t + 1):
                acc = acc + ws_smem[g, t, jj] * rows(vn, jj)[:, cols]
            s_cols.append(acc)
        s_rows.append(jnp.concatenate(s_cols, axis=1))
    s = jnp.concatenate(s_rows, axis=0)
    u = proj()
    out_a = (u * s * _silu(proj())).astype(_BF16)

    z = proj()
    z = z * proj()
    hist = [stbuf[k] for k in range(hist_rows)] + [rows(z, t) for t in range(steps)]
    for k in range(hist_rows):
        cvbuf[k] = hist[steps + k]
        send(cvbuf, k, conv_hbm)
    wconv = [wconv_ref[:, k * d:(k + 1) * d] for k in range(CONV_WIDTH)]
    conv = jnp.concatenate(
        [bconv_ref[...] + sum(wconv[k] * hist[t + k] for k in range(CONV_WIDTH))
         for t in range(steps)], axis=0)
    b_g = proj()
    out_b = (b_g * conv * _silu(proj())).astype(_BF16)

    sg_a = jax.nn.sigmoid(proj())
    sg_b = jax.nn.sigmoid(proj())
    pa = _dot(out_a, next_weight())
    pb = _dot(out_b, next_weight())
    merged = (sg_a * pa + sg_b * pb).astype(_BF16)
    y = _dot(merged, next_weight())
    for t in range(steps):
        ybuf[t] = _layer_norm(alpha * xbuf[t] + (1.0 + gate) * rows(y, t), lng_ref[...], lnb_ref[...])
        send(ybuf, t, y_hbm)

    for cp in exports + results:
        cp.wait()


def _sample_call(x, state, c_s, c_p, w_c, b_c, win, wpa, wpb, wo, w_s, b_s,
                 alg, alb, wconv, bconv, lng, lnb, alpha):
    n, steps, d = x.shape
    nb = c_p.shape[0]
    hist_rows = CONV_WIDTH - 1
    assert hist_rows <= steps <= CHUNK and w_c.shape == (d, 3 * d)
    assert w_s.shape == (GROUPS, CHUNK, CHUNK) and d == GROUPS * CHUNK
    vmem = pl.BlockSpec(memory_space=pltpu.VMEM)
    smem = pl.BlockSpec(memory_space=pltpu.SMEM)
    hbm = pl.BlockSpec(memory_space=pl.ANY)
    weights = (win, wpa, wpb, wo)
    n_exports = sum(w.shape[1] // d for w in weights)
    n_results = 2 * steps + hist_rows
    ws_rows = -(-steps // 8) * 8
    y, conv, v, *weights_bf16, mod_p, wsm, bias = pl.pallas_call(
        functools.partial(_sample_kernel, alpha=alpha),
        in_specs=[smem] + [vmem] * 10 + [hbm] * 8,
        out_specs=[hbm] * (3 + len(weights)) + [vmem] * 3,
        out_shape=[
            jax.ShapeDtypeStruct((n, steps, d), _F32),
            jax.ShapeDtypeStruct((n, hist_rows, d), _F32),
            jax.ShapeDtypeStruct((n, steps, d), _F32),
        ] + [jax.ShapeDtypeStruct(_packed_shape(w), _PACKED) for w in weights] + [
            jax.ShapeDtypeStruct((nb, 3 * d), _F32),
            jax.ShapeDtypeStruct((GROUPS, CHUNK // 2, CHUNK), _PACKED),
            jax.ShapeDtypeStruct((CHUNK, d), _F32)],
        scratch_shapes=[pltpu.VMEM(_packed_shape(w), _PACKED) for w in weights] + [
            pltpu.VMEM((STAGE_SLOTS, d, d), _F32),
            pltpu.VMEM((steps, n, d), _F32), pltpu.VMEM((hist_rows, n, d), _F32),
            pltpu.VMEM((steps, n, d), _F32), pltpu.VMEM((hist_rows, n, d), _F32),
            pltpu.VMEM((steps, n, d), _F32),
            pltpu.SMEM((w_s.shape[0], ws_rows, w_s.shape[2]), _F32),
            pltpu.SemaphoreType.DMA((STAGE_SLOTS,)), pltpu.SemaphoreType.DMA((n_exports,)),
            pltpu.SemaphoreType.DMA((steps + hist_rows,)), pltpu.SemaphoreType.DMA((n_results,)),
            pltpu.SemaphoreType.DMA((1,))],
        compiler_params=pltpu.CompilerParams(vmem_limit_bytes=VMEM_LIMIT_BYTES),
        name="sample_layer",
    )(b_s, w_s, c_s, c_p, b_c.reshape(1, 3 * d), alg, alb, wconv, bconv, lng, lnb,
      x, state, w_s, w_c, win, wpa, wpb, wo)
    return y, conv, v, weights_bf16, mod_p, wsm, bias


def kernel(x_prompt, x_sample, state_conv, c_prompt, c_sample, w_c, b_c, w_in, a_ln_g, a_ln_b,
           w_s, b_s, w_conv, b_conv, w_pa, w_pb, w_o, ln_g, ln_b):
    depth = w_in.shape[0]
    d = x_prompt.shape[-1]
    alpha = (2.0 * depth) ** 0.25

    xp, xs = x_prompt, x_sample
    conv_p_rows, conv_s_rows, v_rows = [], [], []
    for l in range(depth):
        row = lambda a: a.reshape(1, d)
        small = (row(a_ln_g[l]), row(a_ln_b[l]), w_conv[l].reshape(1, CONV_WIDTH * d),
                 row(b_conv[l]), row(ln_g[l]), row(ln_b[l]))
        xs, conv_s, v_s, weights_bf16, mod_p, wsm, bias = _sample_call(
            xs, state_conv[l], c_sample, c_prompt, w_c[l], b_c[l],
            w_in[l], w_pa[l], w_pb[l], w_o[l], w_s[l], b_s[l], *small, alpha)
        xp, conv_p = _prompt_call(xp, mod_p, *weights_bf16, wsm, bias, *small, alpha)
        conv_p_rows.append(conv_p)
        conv_s_rows.append(conv_s)
        v_rows.append(v_s)
    stack = (lambda rows: rows[0][None]) if depth == 1 else jnp.stack
    return (xp, xs, stack(conv_p_rows), stack(conv_s_rows), stack(v_rows))
```

```python
import functools

import jax
import jax.numpy as jnp
from jax import lax
from jax.experimental import pallas as pl
from jax.experimental.pallas import tpu as pltpu

CHUNK = 128
GROUPS = 8
CONV_WIDTH = 3
LN_EPS = 1e-5
ROW_TILE = 512
SUB_TILE = 256
STAGE_SLOTS = 4
VMEM_LIMIT_BYTES = 62 * 1024 * 1024

_F32 = jnp.float32
_BF16 = jnp.bfloat16


def _dot(a, b):
    return jnp.dot(a, b, preferred_element_type=_F32)


def _layer_norm(x, gain, bias):
    mu = jnp.mean(x, axis=-1, keepdims=True)
    xc = x - mu
    var = jnp.mean(xc * xc, axis=-1, keepdims=True)
    return xc * lax.rsqrt(var + LN_EPS) * gain + bias


def _silu(x):
    return x * jax.nn.sigmoid(x)


_PACKED = jnp.uint32


def _packed_shape(w):
    return (w.shape[0] // 2, w.shape[1])


def _as_bf16(words):
    return pltpu.bitcast(words, _BF16)


def _prompt_kernel(x_ref, mod_ref, win_ref, wpa_ref, wpb_ref, wo_ref, wsm_ref, bias_ref, vec_ref,
                   y_ref, conv_ref, z_scr, *, alpha, steps_per_seq):
    tm, d = x_ref.shape
    alg, alb, bconv, lng, lnb = (vec_ref[r:r + 1, :] for r in range(5))
    sub = SUB_TILE
    gd = d // GROUPS
    i = pl.program_id(0)

    mod = mod_ref[pl.ds(i // steps_per_seq, 1), :]
    shift, scale, gate = mod[:, 0:d], mod[:, d:2 * d], mod[:, 2 * d:3 * d]
    bias = bias_ref[...]
    carried = jnp.where(i % steps_per_seq == 0, 0.0, z_scr[...])
    wconv = [vec_ref[5 + k:6 + k, :] for k in range(CONV_WIDTH)]
    starts = range(0, tm, sub)
    tails = []

    def modulated(r0):
        return (x_ref[r0:r0 + sub, :] * (1.0 + scale) + shift).astype(_BF16)

    def project(h, ks):
        return [_dot(h, _as_bf16(win_ref[:, k * d:(k + 1) * d])) for k in ks]

    def mixer_a(u, v, z_a):
        vb = _layer_norm(v, alg, alb).astype(_BF16)
        s_cols = []
        for g in range(GROUPS):
            cols = slice(g * gd, (g + 1) * gd)
            pair_rows = []
            for c in range(0, sub // CHUNK, 2):
                rhs = jnp.concatenate([vb[c * CHUNK:(c + 1) * CHUNK, cols],
                                       vb[(c + 1) * CHUNK:(c + 2) * CHUNK, cols]], axis=1)
                res = _dot(_as_bf16(wsm_ref[g]), rhs)
                pair_rows.append(res[:, :gd] + bias[:, cols])
                pair_rows.append(res[:, gd:] + bias[:, cols])
            s_cols.append(jnp.concatenate(pair_rows, axis=0))
        s = jnp.concatenate(s_cols, axis=1)
        return (u * s * _silu(z_a)).astype(_BF16)

    def mixer_b(b_g, c_g, h_b, z_b):
        z = c_g * h_b
        hist = carried if not tails else tails[-1]
        tails.append(z[sub - 8:sub, :])
        top_rows = lax.broadcasted_iota(jnp.int32, (8, d), 0)

        def delayed(k):
            rolled = pltpu.roll(z, k, axis=0)
            top = jnp.where(top_rows < k, pltpu.roll(hist, k, axis=0), rolled[0:8, :])
            return jnp.concatenate([top, rolled[8:, :]], axis=0)

        conv = bconv + wconv[0] * delayed(2) + wconv[1] * delayed(1) + wconv[2] * z
        return (b_g * conv * _silu(z_b)).astype(_BF16)

    def merge(out_a, out_b, g_a, g_b):
        return (jax.nn.sigmoid(g_a) * _dot(out_a, _as_bf16(wpa_ref[...]))
                + jax.nn.sigmoid(g_b) * _dot(out_b, _as_bf16(wpb_ref[...]))).astype(_BF16)

    def finish(r0, mg):
        x = x_ref[r0:r0 + sub, :]
        y_ref[r0:r0 + sub, :] = _layer_norm(
            alpha * x + (1.0 + gate) * _dot(mg, _as_bf16(wo_ref[...])), lng, lnb)

    r_a, r_b = starts
    u, v, z_a, b_g, c_g, h_b, z_b, g_a, g_b = project(modulated(r_a), range(9))
    hb = modulated(r_b)
    u2, v2, z_a2 = project(hb, range(0, 3))
    out_a = mixer_a(u, v, z_a)
    b_g2, c_g2, h_b2, z_b2 = project(hb, range(3, 7))
    out_b = mixer_b(b_g, c_g, h_b, z_b)
    g_a2, g_b2 = project(hb, range(7, 9))
    out_a2 = mixer_a(u2, v2, z_a2)
    mg = merge(out_a, out_b, g_a, g_b)
    out_b2 = mixer_b(b_g2, c_g2, h_b2, z_b2)
    finish(r_a, mg)
    finish(r_b, merge(out_a2, out_b2, g_a2, g_b2))

    z_scr[...] = tails[-1]
    conv_ref[0] = tails[-1][8 - (CONV_WIDTH - 1):8, :]


def _resident(shape):
    nd = len(shape)
    return pl.BlockSpec(shape, lambda i: (0,) * nd, pipeline_mode=pl.Buffered(1))


def _prompt_call(x, mod, win, wpa, wpb, wo, wsm, bias, vecs, alpha):
    nb, length, d = x.shape
    tm = ROW_TILE
    assert length % tm == 0 and tm == 2 * SUB_TILE and SUB_TILE % (2 * CHUNK) == 0
    assert d % GROUPS == 0 and d // GROUPS == CHUNK
    steps_per_seq = length // tm
    y, conv = pl.pallas_call(
        functools.partial(_prompt_kernel, alpha=alpha, steps_per_seq=steps_per_seq),
        grid=(nb * steps_per_seq,),
        in_specs=[
            pl.BlockSpec((tm, d), lambda i: (i, 0)),
            _resident(mod.shape),
            _resident(win.shape), _resident(wpa.shape), _resident(wpb.shape), _resident(wo.shape),
            _resident(wsm.shape), _resident(bias.shape), _resident(vecs.shape),
        ],
        out_specs=[
            pl.BlockSpec((tm, d), lambda i: (i, 0)),
            pl.BlockSpec((1, CONV_WIDTH - 1, d), lambda i: (i // steps_per_seq, 0, 0)),
        ],
        out_shape=[
            jax.ShapeDtypeStruct((nb * length, d), _F32),
            jax.ShapeDtypeStruct((nb, CONV_WIDTH - 1, d), _F32),
        ],
        scratch_shapes=[pltpu.VMEM((8, d), _F32)],
        compiler_params=pltpu.CompilerParams(
            dimension_semantics=("arbitrary",),
            vmem_limit_bytes=VMEM_LIMIT_BYTES),
        name="prompt_layer",
    )(x.reshape(nb * length, d), mod, win, wpa, wpb, wo, wsm, bias, vecs)
    return y.reshape(nb, length, d), conv


_PROJ_ORDER = (1, 0, 2, 4, 5, 3, 6, 7, 8)


def _sample_kernel(bs_ref, ws_ref, cs_ref, cp_ref, bc_ref,
                   alg_ref, alb_ref, wconv_ref, bconv_ref, lng_ref, lnb_ref,
                   x_hbm, st_hbm, ws_hbm, wc_hbm, win_hbm, wpa_hbm, wpb_hbm, wo_hbm,
                   y_hbm, conv_hbm, v_hbm, win_out, wpa_out, wpb_out, wo_out, modp_ref,
                   wsm_ref, bias_ref, vec_ref,
                   win_ref, wpa_ref, wpb_ref, wo_ref, stage, xbuf, stbuf, ybuf, cvbuf, vbuf, ws_smem,
                   sem, exp_sem, in_sem, res_sem, ws_sem, *, alpha):
    steps, n, d = xbuf.shape
    gd = d // GROUPS
    hist_rows = CONV_WIDTH - 1

    ws_copy = pltpu.make_async_copy(ws_hbm.at[:, 0:ws_smem.shape[1], :], ws_smem, ws_sem.at[0])
    ws_copy.start()

    in_copies = ([pltpu.make_async_copy(x_hbm.at[:, t, :], xbuf.at[t], in_sem.at[t])
                  for t in range(steps)]
                 + [pltpu.make_async_copy(st_hbm.at[:, k, :], stbuf.at[k], in_sem.at[steps + k])
                    for k in range(hist_rows)])
    for cp in in_copies:
        cp.start()

    blocks = ([(wc_hbm, None, None, j) for j in range(wc_hbm.shape[1] // d)]
              + [(win_hbm, win_ref, win_out, k) for k in _PROJ_ORDER]
              + [(wpa_hbm, wpa_ref, wpa_out, 0), (wpb_hbm, wpb_ref, wpb_out, 0),
                 (wo_hbm, wo_ref, wo_out, 0)])
    exports = []
    position = [0]

    def stage_copy(i):
        src, _, _, k = blocks[i]
        slot = i % STAGE_SLOTS
        return pltpu.make_async_copy(src.at[:, k * d:(k + 1) * d], stage.at[slot], sem.at[slot])

    def next_weight():
        i = position[0]
        position[0] += 1
        stage_copy(i).wait()
        if i + STAGE_SLOTS - 1 < len(blocks):
            stage_copy(i + STAGE_SLOTS - 1).start()
        w = stage[i % STAGE_SLOTS].astype(_BF16)
        _, keep, out, k = blocks[i]
        if keep is not None:
            cols = slice(k * d, (k + 1) * d)
            keep[:, cols] = pltpu.bitcast(w, _PACKED)
            cp = pltpu.make_async_copy(keep.at[:, cols], out.at[:, cols], exp_sem.at[len(exports)])
            cp.start()
            exports.append(cp)
        return w

    for i in range(STAGE_SLOTS - 1):
        stage_copy(i).start()

    c_all = jnp.concatenate([cs_ref[...], cp_ref[...]], axis=0).astype(_BF16)
    bc = bc_ref[...]
    mods = []
    for j in range(3):
        m = _dot(c_all, next_weight()) + bc[:, j * d:(j + 1) * d]
        modp_ref[:, j * d:(j + 1) * d] = m[n:, :]
        mods.append(m[0:n, :])
    shift, scale, gate = mods

    row = lax.broadcasted_iota(jnp.int32, (CHUNK, CHUNK), 0)
    col = lax.broadcasted_iota(jnp.int32, (CHUNK, CHUNK), 1)
    for g in range(GROUPS):
        wsm_ref[g] = pltpu.bitcast(jnp.where(col <= row, ws_ref[g], 0.0).astype(_BF16), _PACKED)

    for t in range(CHUNK):
        for g in range(GROUPS):
            bias_ref[t:t + 1, g * gd:(g + 1) * gd] = jnp.full((1, gd), bs_ref[g, t], _F32)
    for r, ref in enumerate((alg_ref, alb_ref, bconv_ref, lng_ref, lnb_ref)):
        vec_ref[r:r + 1, :] = ref[...]
    for k in range(CONV_WIDTH):
        vec_ref[5 + k:6 + k, :] = wconv_ref[:, k * d:(k + 1) * d]

    for cp in in_copies:
        cp.wait()
    ws_copy.wait()
    h = jnp.concatenate([(xbuf[t] * (1.0 + scale) + shift).astype(_BF16) for t in range(steps)],
                        axis=0)

    def rows(a, t):
        return a[t * n:(t + 1) * n, :]

    def proj():
        return _dot(h, next_weight())

    results = []

    def send(buf, t, dst):
        cp = pltpu.make_async_copy(buf.at[t], dst.at[:, t, :], res_sem.at[len(results)])
        cp.start()
        results.append(cp)

    vn = _layer_norm(proj(), alg_ref[...], alb_ref[...])
    s_rows = []
    for t in range(steps):
        vbuf[t] = rows(vn, t)
        send(vbuf, t, v_hbm)
        s_cols = []
        for g in range(GROUPS):
            cols = slice(g * gd, (g + 1) * gd)
            acc = jnp.full((n, gd), bs_ref[g, t], _F32)
            for jj in range(t + 1):
                acc = acc + ws_smem[g, t, jj] * rows(vn, jj)[:, cols]
            s_cols.append(acc)
        s_rows.append(jnp.concatenate(s_cols, axis=1))
    s = jnp.concatenate(s_rows, axis=0)
    u = proj()
    out_a = (u * s * _silu(proj())).astype(_BF16)

    z = proj()
    z = z * proj()
    hist = [stbuf[k] for k in range(hist_rows)] + [rows(z, t) for t in range(steps)]
    for k in range(hist_rows):
        cvbuf[k] = hist[steps + k]
        send(cvbuf, k, conv_hbm)
    wconv = [wconv_ref[:, k * d:(k + 1) * d] for k in range(CONV_WIDTH)]
    conv = jnp.concatenate(
        [bconv_ref[...] + sum(wconv[k] * hist[t + k] for k in range(CONV_WIDTH))
         for t in range(steps)], axis=0)
    b_g = proj()
    out_b = (b_g * conv * _silu(proj())).astype(_BF16)

    sg_a = jax.nn.sigmoid(proj())
    sg_b = jax.nn.sigmoid(proj())
    pa = _dot(out_a, next_weight())
    pb = _dot(out_b, next_weight())
    merged = (sg_a * pa + sg_b * pb).astype(_BF16)
    y = _dot(merged, next_weight())
    for t in range(steps):
        ybuf[t] = _layer_norm(alpha * xbuf[t] + (1.0 + gate) * rows(y, t), lng_ref[...], lnb_ref[...])
        send(ybuf, t, y_hbm)

    for cp in exports + results:
        cp.wait()


def _sample_call(x, state, c_s, c_p, w_c, b_c, win, wpa, wpb, wo, w_s, b_s,
                 alg, alb, wconv, bconv, lng, lnb, alpha):
    n, steps, d = x.shape
    nb = c_p.shape[0]
    hist_rows = CONV_WIDTH - 1
    assert hist_rows <= steps <= CHUNK and w_c.shape == (d, 3 * d)
    assert w_s.shape == (GROUPS, CHUNK, CHUNK) and d == GROUPS * CHUNK
    vmem = pl.BlockSpec(memory_space=pltpu.VMEM)
    smem = pl.BlockSpec(memory_space=pltpu.SMEM)
    hbm = pl.BlockSpec(memory_space=pl.ANY)
    weights = (win, wpa, wpb, wo)
    n_exports = sum(w.shape[1] // d for w in weights)
    n_results = 2 * steps + hist_rows
    ws_rows = -(-steps // 8) * 8
    y, conv, v, *weights_bf16, mod_p, wsm, bias, vecs = pl.pallas_call(
        functools.partial(_sample_kernel, alpha=alpha),
        in_specs=[smem] + [vmem] * 10 + [hbm] * 8,
        out_specs=[hbm] * (3 + len(weights)) + [vmem] * 4,
        out_shape=[
            jax.ShapeDtypeStruct((n, steps, d), _F32),
            jax.ShapeDtypeStruct((n, hist_rows, d), _F32),
            jax.ShapeDtypeStruct((n, steps, d), _F32),
        ] + [jax.ShapeDtypeStruct(_packed_shape(w), _PACKED) for w in weights] + [
            jax.ShapeDtypeStruct((nb, 3 * d), _F32),
            jax.ShapeDtypeStruct((GROUPS, CHUNK // 2, CHUNK), _PACKED),
            jax.ShapeDtypeStruct((CHUNK, d), _F32),
            jax.ShapeDtypeStruct((5 + CONV_WIDTH, d), _F32)],
        scratch_shapes=[pltpu.VMEM(_packed_shape(w), _PACKED) for w in weights] + [
            pltpu.VMEM((STAGE_SLOTS, d, d), _F32),
            pltpu.VMEM((steps, n, d), _F32), pltpu.VMEM((hist_rows, n, d), _F32),
            pltpu.VMEM((steps, n, d), _F32), pltpu.VMEM((hist_rows, n, d), _F32),
            pltpu.VMEM((steps, n, d), _F32),
            pltpu.SMEM((w_s.shape[0], ws_rows, w_s.shape[2]), _F32),
            pltpu.SemaphoreType.DMA((STAGE_SLOTS,)), pltpu.SemaphoreType.DMA((n_exports,)),
            pltpu.SemaphoreType.DMA((steps + hist_rows,)), pltpu.SemaphoreType.DMA((n_results,)),
            pltpu.SemaphoreType.DMA((1,))],
        compiler_params=pltpu.CompilerParams(vmem_limit_bytes=VMEM_LIMIT_BYTES),
        name="sample_layer",
    )(b_s, w_s, c_s, c_p, b_c.reshape(1, 3 * d), alg, alb, wconv, bconv, lng, lnb,
      x, state, w_s, w_c, win, wpa, wpb, wo)
    return y, conv, v, weights_bf16, mod_p, wsm, bias, vecs


def kernel(x_prompt, x_sample, state_conv, c_prompt, c_sample, w_c, b_c, w_in, a_ln_g, a_ln_b,
           w_s, b_s, w_conv, b_conv, w_pa, w_pb, w_o, ln_g, ln_b):
    depth = w_in.shape[0]
    d = x_prompt.shape[-1]
    alpha = (2.0 * depth) ** 0.25

    xp, xs = x_prompt, x_sample
    conv_p_rows, conv_s_rows, v_rows = [], [], []
    for l in range(depth):
        row = lambda a: a.reshape(1, d)
        small = (row(a_ln_g[l]), row(a_ln_b[l]), w_conv[l].reshape(1, CONV_WIDTH * d),
                 row(b_conv[l]), row(ln_g[l]), row(ln_b[l]))
        xs, conv_s, v_s, weights_bf16, mod_p, wsm, bias, vecs = _sample_call(
            xs, state_conv[l], c_sample, c_prompt, w_c[l], b_c[l],
            w_in[l], w_pa[l], w_pb[l], w_o[l], w_s[l], b_s[l], *small, alpha)
        xp, conv_p = _prompt_call(xp, mod_p, *weights_bf16, wsm, bias, vecs, alpha)
        conv_p_rows.append(conv_p)
        conv_s_rows.append(conv_s)
        v_rows.append(v_s)
    stack = (lambda rows: rows[0][None]) if depth == 1 else jnp.stack
    return (xp, xs, stack(conv_p_rows), stack(conv_s_rows), stack(v_rows))
```

```python
import functools

import jax
import jax.numpy as jnp
from jax import lax
from jax.experimental import pallas as pl
from jax.experimental.pallas import tpu as pltpu

CHUNK = 128
GROUPS = 8
CONV_WIDTH = 3
LN_EPS = 1e-5
ROW_TILE = 512
SUB_TILE = 256
STAGE_SLOTS = 4
VMEM_LIMIT_BYTES = 62 * 1024 * 1024

_F32 = jnp.float32
_BF16 = jnp.bfloat16


def _dot(a, b):
    return jnp.dot(a, b, preferred_element_type=_F32)


def _layer_norm(x, gain, bias):
    mu = jnp.mean(x, axis=-1, keepdims=True)
    xc = x - mu
    var = jnp.mean(xc * xc, axis=-1, keepdims=True)
    return xc * lax.rsqrt(var + LN_EPS) * gain + bias


def _silu(x):
    return x * jax.nn.sigmoid(x)


_PACKED = jnp.uint32


def _packed_shape(w):
    return (w.shape[0] // 2, w.shape[1])


def _as_bf16(words):
    return pltpu.bitcast(words, _BF16)


def _prompt_kernel(x_ref, mod_ref, win_ref, wpa_ref, wpb_ref, wo_ref, wsm_ref, bias_ref, vec_ref,
                   y_ref, conv_ref, z_scr, *, alpha, steps_per_seq):
    tm, d = x_ref.shape
    alg, alb, bconv, lng, lnb = (vec_ref[r:r + 1, :] for r in range(5))
    sub = SUB_TILE
    gd = d // GROUPS
    i = pl.program_id(0)

    mod = mod_ref[pl.ds(i // steps_per_seq, 1), :]
    shift, scale, gate = mod[:, 0:d], mod[:, d:2 * d], mod[:, 2 * d:3 * d]
    bias = bias_ref[...]
    carried = jnp.where(i % steps_per_seq == 0, 0.0, z_scr[...])
    wconv = [vec_ref[5 + k:6 + k, :] for k in range(CONV_WIDTH)]
    starts = range(0, tm, sub)
    tails = []

    def modulated(r0):
        return (x_ref[r0:r0 + sub, :] * (1.0 + scale) + shift).astype(_BF16)

    def project(h, ks):
        return [_dot(h, _as_bf16(win_ref[:, k * d:(k + 1) * d])) for k in ks]

    def mixer_a(u, v, z_a):
        vb = _layer_norm(v, alg, alb).astype(_BF16)
        s_cols = []
        for g in range(GROUPS):
            cols = slice(g * gd, (g + 1) * gd)
            pair_rows = []
            for c in range(0, sub // CHUNK, 2):
                rhs = jnp.concatenate([vb[c * CHUNK:(c + 1) * CHUNK, cols],
                                       vb[(c + 1) * CHUNK:(c + 2) * CHUNK, cols]], axis=1)
                res = _dot(_as_bf16(wsm_ref[g]), rhs)
                pair_rows.append(res[:, :gd] + bias[:, cols])
                pair_rows.append(res[:, gd:] + bias[:, cols])
            s_cols.append(jnp.concatenate(pair_rows, axis=0))
        s = jnp.concatenate(s_cols, axis=1)
        hw = d // 2
        return jnp.concatenate(
            [(u[:, c:c + hw] * s[:, c:c + hw] * _silu(z_a[:, c:c + hw])).astype(_BF16)
             for c in (0, hw)], axis=1)

    def mixer_b(b_g, c_g, h_b, z_b):
        z = c_g * h_b
        hist = carried if not tails else tails[-1]
        tails.append(z[sub - 8:sub, :])
        top_rows = lax.broadcasted_iota(jnp.int32, (8, d), 0)

        def delayed(k):
            rolled = pltpu.roll(z, k, axis=0)
            top = jnp.where(top_rows < k, pltpu.roll(hist, k, axis=0), rolled[0:8, :])
            return jnp.concatenate([top, rolled[8:, :]], axis=0)

        conv = bconv + wconv[0] * delayed(2) + wconv[1] * delayed(1) + wconv[2] * z
        hw = d // 2
        return jnp.concatenate(
            [(b_g[:, c:c + hw] * conv[:, c:c + hw] * _silu(z_b[:, c:c + hw])).astype(_BF16)
             for c in (0, hw)], axis=1)

    def merge(out_a, out_b, g_a, g_b):
        return (jax.nn.sigmoid(g_a) * _dot(out_a, _as_bf16(wpa_ref[...]))
                + jax.nn.sigmoid(g_b) * _dot(out_b, _as_bf16(wpb_ref[...]))).astype(_BF16)

    def finish(r0, mg):
        x = x_ref[r0:r0 + sub, :]
        y_ref[r0:r0 + sub, :] = _layer_norm(
            alpha * x + (1.0 + gate) * _dot(mg, _as_bf16(wo_ref[...])), lng, lnb)

    r_a, r_b = starts
    u, v, z_a, b_g, c_g, h_b, z_b, g_a, g_b = project(modulated(r_a), range(9))
    hb = modulated(r_b)
    u2, v2, z_a2 = project(hb, range(0, 3))
    out_a = mixer_a(u, v, z_a)
    b_g2, c_g2, h_b2, z_b2 = project(hb, range(3, 7))
    out_b = mixer_b(b_g, c_g, h_b, z_b)
    g_a2, g_b2 = project(hb, range(7, 9))
    out_a2 = mixer_a(u2, v2, z_a2)
    mg = merge(out_a, out_b, g_a, g_b)
    out_b2 = mixer_b(b_g2, c_g2, h_b2, z_b2)
    finish(r_a, mg)
    finish(r_b, merge(out_a2, out_b2, g_a2, g_b2))

    z_scr[...] = tails[-1]
    conv_ref[0] = tails[-1][8 - (CONV_WIDTH - 1):8, :]


def _resident(shape):
    nd = len(shape)
    return pl.BlockSpec(shape, lambda i: (0,) * nd, pipeline_mode=pl.Buffered(1))


def _prompt_call(x, mod, win, wpa, wpb, wo, wsm, bias, vecs, alpha):
    nb, length, d = x.shape
    tm = ROW_TILE
    assert length % tm == 0 and tm == 2 * SUB_TILE and SUB_TILE % (2 * CHUNK) == 0
    assert d % GROUPS == 0 and d // GROUPS == CHUNK
    steps_per_seq = length // tm
    y, conv = pl.pallas_call(
        functools.partial(_prompt_kernel, alpha=alpha, steps_per_seq=steps_per_seq),
        grid=(nb * steps_per_seq,),
        in_specs=[
            pl.BlockSpec((tm, d), lambda i: (i, 0)),
            _resident(mod.shape),
            _resident(win.shape), _resident(wpa.shape), _resident(wpb.shape), _resident(wo.shape),
            _resident(wsm.shape), _resident(bias.shape), _resident(vecs.shape),
        ],
        out_specs=[
            pl.BlockSpec((tm, d), lambda i: (i, 0)),
            pl.BlockSpec((1, CONV_WIDTH - 1, d), lambda i: (i // steps_per_seq, 0, 0)),
        ],
        out_shape=[
            jax.ShapeDtypeStruct((nb * length, d), _F32),
            jax.ShapeDtypeStruct((nb, CONV_WIDTH - 1, d), _F32),
        ],
        scratch_shapes=[pltpu.VMEM((8, d), _F32)],
        compiler_params=pltpu.CompilerParams(
            dimension_semantics=("arbitrary",),
            vmem_limit_bytes=VMEM_LIMIT_BYTES),
        name="prompt_layer",
    )(x.reshape(nb * length, d), mod, win, wpa, wpb, wo, wsm, bias, vecs)
    return y.reshape(nb, length, d), conv


_PROJ_ORDER = (1, 0, 2, 4, 5, 3, 6, 7, 8)


def _sample_kernel(bs_ref, ws_ref, cs_ref, cp_ref, bc_ref,
                   alg_ref, alb_ref, wconv_ref, bconv_ref, lng_ref, lnb_ref,
                   x_hbm, st_hbm, ws_hbm, wc_hbm, win_hbm, wpa_hbm, wpb_hbm, wo_hbm,
                   y_hbm, conv_hbm, v_hbm, win_out, wpa_out, wpb_out, wo_out, modp_ref,
                   wsm_ref, bias_ref, vec_ref,
                   win_ref, wpa_ref, wpb_ref, wo_ref, stage, xbuf, stbuf, ybuf, cvbuf, vbuf, ws_smem,
                   sem, exp_sem, in_sem, res_sem, ws_sem, *, alpha):
    steps, n, d = xbuf.shape
    gd = d // GROUPS
    hist_rows = CONV_WIDTH - 1

    ws_copy = pltpu.make_async_copy(ws_hbm.at[:, 0:ws_smem.shape[1], :], ws_smem, ws_sem.at[0])
    ws_copy.start()

    in_copies = ([pltpu.make_async_copy(x_hbm.at[:, t, :], xbuf.at[t], in_sem.at[t])
                  for t in range(steps)]
                 + [pltpu.make_async_copy(st_hbm.at[:, k, :], stbuf.at[k], in_sem.at[steps + k])
                    for k in range(hist_rows)])
    for cp in in_copies:
        cp.start()

    blocks = ([(wc_hbm, None, None, j) for j in range(wc_hbm.shape[1] // d)]
              + [(win_hbm, win_ref, win_out, k) for k in _PROJ_ORDER]
              + [(wpa_hbm, wpa_ref, wpa_out, 0), (wpb_hbm, wpb_ref, wpb_out, 0),
                 (wo_hbm, wo_ref, wo_out, 0)])
    exports = []
    position = [0]

    def stage_copy(i):
        src, _, _, k = blocks[i]
        slot = i % STAGE_SLOTS
        return pltpu.make_async_copy(src.at[:, k * d:(k + 1) * d], stage.at[slot], sem.at[slot])

    def next_weight():
        i = position[0]
        position[0] += 1
        stage_copy(i).wait()
        if i + STAGE_SLOTS - 1 < len(blocks):
            stage_copy(i + STAGE_SLOTS - 1).start()
        w = stage[i % STAGE_SLOTS].astype(_BF16)
        _, keep, out, k = blocks[i]
        if keep is not None:
            cols = slice(k * d, (k + 1) * d)
            keep[:, cols] = pltpu.bitcast(w, _PACKED)
            cp = pltpu.make_async_copy(keep.at[:, cols], out.at[:, cols], exp_sem.at[len(exports)])
            cp.start()
            exports.append(cp)
        return w

    for i in range(STAGE_SLOTS - 1):
        stage_copy(i).start()

    c_all = jnp.concatenate([cs_ref[...], cp_ref[...]], axis=0).astype(_BF16)
    bc = bc_ref[...]
    mods = []
    for j in range(3):
        m = _dot(c_all, next_weight()) + bc[:, j * d:(j + 1) * d]
        modp_ref[:, j * d:(j + 1) * d] = m[n:, :]
        mods.append(m[0:n, :])
    shift, scale, gate = mods

    row = lax.broadcasted_iota(jnp.int32, (CHUNK, CHUNK), 0)
    col = lax.broadcasted_iota(jnp.int32, (CHUNK, CHUNK), 1)
    for g in range(GROUPS):
        wsm_ref[g] = pltpu.bitcast(jnp.where(col <= row, ws_ref[g], 0.0).astype(_BF16), _PACKED)

    for t in range(CHUNK):
        for g in range(GROUPS):
            bias_ref[t:t + 1, g * gd:(g + 1) * gd] = jnp.full((1, gd), bs_ref[g, t], _F32)
    for r, ref in enumerate((alg_ref, alb_ref, bconv_ref, lng_ref, lnb_ref)):
        vec_ref[r:r + 1, :] = ref[...]
    for k in range(CONV_WIDTH):
        vec_ref[5 + k:6 + k, :] = wconv_ref[:, k * d:(k + 1) * d]

    for cp in in_copies:
        cp.wait()
    ws_copy.wait()
    h = jnp.concatenate([(xbuf[t] * (1.0 + scale) + shift).astype(_BF16) for t in range(steps)],
                        axis=0)

    def rows(a, t):
        return a[t * n:(t + 1) * n, :]

    def proj():
        return _dot(h, next_weight())

    results = []

    def send(buf, t, dst):
        cp = pltpu.make_async_copy(buf.at[t], dst.at[:, t, :], res_sem.at[len(results)])
        cp.start()
        results.append(cp)

    vn = _layer_norm(proj(), alg_ref[...], alb_ref[...])
    s_rows = []
    for t in range(steps):
        vbuf[t] = rows(vn, t)
        send(vbuf, t, v_hbm)
        s_cols = []
        for g in range(GROUPS):
            cols = slice(g * gd, (g + 1) * gd)
            acc = jnp.full((n, gd), bs_ref[g, t], _F32)
            for jj in range(t + 1):
                acc = acc + ws_smem[g, t, jj] * rows(vn, jj)[:, cols]
            s_cols.append(acc)
        s_rows.append(jnp.concatenate(s_cols, axis=1))
    s = jnp.concatenate(s_rows, axis=0)
    u = proj()
    out_a = (u * s * _silu(proj())).astype(_BF16)

    z = proj()
    z = z * proj()
    hist = [stbuf[k] for k in range(hist_rows)] + [rows(z, t) for t in range(steps)]
    for k in range(hist_rows):
        cvbuf[k] = hist[steps + k]
        send(cvbuf, k, conv_hbm)
    wconv = [wconv_ref[:, k * d:(k + 1) * d] for k in range(CONV_WIDTH)]
    conv = jnp.concatenate(
        [bconv_ref[...] + sum(wconv[k] * hist[t + k] for k in range(CONV_WIDTH))
         for t in range(steps)], axis=0)
    b_g = proj()
    out_b = (b_g * conv * _silu(proj())).astype(_BF16)

    sg_a = jax.nn.sigmoid(proj())
    sg_b = jax.nn.sigmoid(proj())
    pa = _dot(out_a, next_weight())
    pb = _dot(out_b, next_weight())
    merged = (sg_a * pa + sg_b * pb).astype(_BF16)
    y = _dot(merged, next_weight())
    for t in range(steps):
        ybuf[t] = _layer_norm(alpha * xbuf[t] + (1.0 + gate) * rows(y, t), lng_ref[...], lnb_ref[...])
        send(ybuf, t, y_hbm)

    for cp in exports + results:
        cp.wait()


def _sample_call(x, state, c_s, c_p, w_c, b_c, win, wpa, wpb, wo, w_s, b_s,
                 alg, alb, wconv, bconv, lng, lnb, alpha):
    n, steps, d = x.shape
    nb = c_p.shape[0]
    hist_rows = CONV_WIDTH - 1
    assert hist_rows <= steps <= CHUNK and w_c.shape == (d, 3 * d)
    assert w_s.shape == (GROUPS, CHUNK, CHUNK) and d == GROUPS * CHUNK
    vmem = pl.BlockSpec(memory_space=pltpu.VMEM)
    smem = pl.BlockSpec(memory_space=pltpu.SMEM)
    hbm = pl.BlockSpec(memory_space=pl.ANY)
    weights = (win, wpa, wpb, wo)
    n_exports = sum(w.shape[1] // d for w in weights)
    n_results = 2 * steps + hist_rows
    ws_rows = -(-steps // 8) * 8
    y, conv, v, *weights_bf16, mod_p, wsm, bias, vecs = pl.pallas_call(
        functools.partial(_sample_kernel, alpha=alpha),
        in_specs=[smem] + [vmem] * 10 + [hbm] * 8,
        out_specs=[hbm] * (3 + len(weights)) + [vmem] * 4,
        out_shape=[
            jax.ShapeDtypeStruct((n, steps, d), _F32),
            jax.ShapeDtypeStruct((n, hist_rows, d), _F32),
            jax.ShapeDtypeStruct((n, steps, d), _F32),
        ] + [jax.ShapeDtypeStruct(_packed_shape(w), _PACKED) for w in weights] + [
            jax.ShapeDtypeStruct((nb, 3 * d), _F32),
            jax.ShapeDtypeStruct((GROUPS, CHUNK // 2, CHUNK), _PACKED),
            jax.ShapeDtypeStruct((CHUNK, d), _F32),
            jax.ShapeDtypeStruct((5 + CONV_WIDTH, d), _F32)],
        scratch_shapes=[pltpu.VMEM(_packed_shape(w), _PACKED) for w in weights] + [
            pltpu.VMEM((STAGE_SLOTS, d, d), _F32),
            pltpu.VMEM((steps, n, d), _F32), pltpu.VMEM((hist_rows, n, d), _F32),
            pltpu.VMEM((steps, n, d), _F32), pltpu.VMEM((hist_rows, n, d), _F32),
            pltpu.VMEM((steps, n, d), _F32),
            pltpu.SMEM((w_s.shape[0], ws_rows, w_s.shape[2]), _F32),
            pltpu.SemaphoreType.DMA((STAGE_SLOTS,)), pltpu.SemaphoreType.DMA((n_exports,)),
            pltpu.SemaphoreType.DMA((steps + hist_rows,)), pltpu.SemaphoreType.DMA((n_results,)),
            pltpu.SemaphoreType.DMA((1,))],
        compiler_params=pltpu.CompilerParams(vmem_limit_bytes=VMEM_LIMIT_BYTES),
        name="sample_layer",
    )(b_s, w_s, c_s, c_p, b_c.reshape(1, 3 * d), alg, alb, wconv, bconv, lng, lnb,
      x, state, w_s, w_c, win, wpa, wpb, wo)
    return y, conv, v, weights_bf16, mod_p, wsm, bias, vecs


def kernel(x_prompt, x_sample, state_conv, c_prompt, c_sample, w_c, b_c, w_in, a_ln_g, a_ln_b,
           w_s, b_s, w_conv, b_conv, w_pa, w_pb, w_o, ln_g, ln_b):
    depth = w_in.shape[0]
    d = x_prompt.shape[-1]
    alpha = (2.0 * depth) ** 0.25

    xp, xs = x_prompt, x_sample
    conv_p_rows, conv_s_rows, v_rows = [], [], []
    for l in range(depth):
        row = lambda a: a.reshape(1, d)
        small = (row(a_ln_g[l]), row(a_ln_b[l]), w_conv[l].reshape(1, CONV_WIDTH * d),
                 row(b_conv[l]), row(ln_g[l]), row(ln_b[l]))
        xs, conv_s, v_s, weights_bf16, mod_p, wsm, bias, vecs = _sample_call(
            xs, state_conv[l], c_sample, c_prompt, w_c[l], b_c[l],
            w_in[l], w_pa[l], w_pb[l], w_o[l], w_s[l], b_s[l], *small, alpha)
        xp, conv_p = _prompt_call(xp, mod_p, *weights_bf16, wsm, bias, vecs, alpha)
        conv_p_rows.append(conv_p)
        conv_s_rows.append(conv_s)
        v_rows.append(v_s)
    stack = (lambda rows: rows[0][None]) if depth == 1 else jnp.stack
    return (xp, xs, stack(conv_p_rows), stack(conv_s_rows), stack(v_rows))
```

```python
import functools

import jax
import jax.numpy as jnp
from jax import lax
from jax.experimental import pallas as pl
from jax.experimental.pallas import tpu as pltpu

CHUNK = 128
GROUPS = 8
CONV_WIDTH = 3
LN_EPS = 1e-5
ROW_TILE = 512
SUB_TILE = 256
STAGE_SLOTS = 4
VMEM_LIMIT_BYTES = 62 * 1024 * 1024

_F32 = jnp.float32
_BF16 = jnp.bfloat16


def _dot(a, b):
    return jnp.dot(a, b, preferred_element_type=_F32)


def _layer_norm(x, gain, bias):
    mu = jnp.mean(x, axis=-1, keepdims=True)
    xc = x - mu
    var = jnp.mean(xc * xc, axis=-1, keepdims=True)
    return xc * lax.rsqrt(var + LN_EPS) * gain + bias


def _silu(x):
    return x * jax.nn.sigmoid(x)


_PACKED = jnp.uint32


def _packed_shape(w):
    return (w.shape[0] // 2, w.shape[1])


def _as_bf16(words):
    return pltpu.bitcast(words, _BF16)


def _prompt_kernel(x_ref, mod_ref, win_ref, wpa_ref, wpb_ref, wo_ref, wsm_ref, bias_ref, vec_ref,
                   y_ref, conv_ref, z_scr, *, alpha, steps_per_seq):
    tm, d = x_ref.shape
    alg, alb, bconv, lng, lnb = (vec_ref[r:r + 1, :] for r in range(5))
    sub = SUB_TILE
    gd = d // GROUPS
    i = pl.program_id(0)

    mod = mod_ref[pl.ds(i // steps_per_seq, 1), :]
    shift, scale, gate = mod[:, 0:d], mod[:, d:2 * d], mod[:, 2 * d:3 * d]
    bias = bias_ref[...]
    carried = jnp.where(i % steps_per_seq == 0, 0.0, z_scr[...])
    wconv = [vec_ref[5 + k:6 + k, :] for k in range(CONV_WIDTH)]
    starts = range(0, tm, sub)
    tails = []

    def modulated(r0):
        return (x_ref[r0:r0 + sub, :] * (1.0 + scale) + shift).astype(_BF16)

    def project(h, ks):
        return [_dot(h, _as_bf16(win_ref[:, k * d:(k + 1) * d])) for k in ks]

    def mixer_a(u, v, z_a):
        vb = _layer_norm(v, vec_ref[0:1, :], vec_ref[1:2, :]).astype(_BF16)
        s_cols = []
        for g in range(GROUPS):
            cols = slice(g * gd, (g + 1) * gd)
            pair_rows = []
            for c in range(0, sub // CHUNK, 2):
                rhs = jnp.concatenate([vb[c * CHUNK:(c + 1) * CHUNK, cols],
                                       vb[(c + 1) * CHUNK:(c + 2) * CHUNK, cols]], axis=1)
                res = _dot(_as_bf16(wsm_ref[g]), rhs)
                pair_rows.append(res[:, :gd] + bias[:, cols])
                pair_rows.append(res[:, gd:] + bias[:, cols])
            s_cols.append(jnp.concatenate(pair_rows, axis=0))
        s = jnp.concatenate(s_cols, axis=1)
        return (u * s * _silu(z_a)).astype(_BF16)

    def mixer_b(b_g, c_g, h_b, z_b):
        z = c_g * h_b
        hist = carried if not tails else tails[-1]
        tails.append(z[sub - 8:sub, :])
        top_rows = lax.broadcasted_iota(jnp.int32, (8, d), 0)

        def delayed(k):
            rolled = pltpu.roll(z, k, axis=0)
            top = jnp.where(top_rows < k, pltpu.roll(hist, k, axis=0), rolled[0:8, :])
            return jnp.concatenate([top, rolled[8:, :]], axis=0)

        conv = (vec_ref[2:3, :] + vec_ref[5:6, :] * delayed(2) + vec_ref[6:7, :] * delayed(1)
                + vec_ref[7:8, :] * z)
        return (b_g * conv * _silu(z_b)).astype(_BF16)

    def merge(out_a, out_b, g_a, g_b):
        return (jax.nn.sigmoid(g_a) * _dot(out_a, _as_bf16(wpa_ref[...]))
                + jax.nn.sigmoid(g_b) * _dot(out_b, _as_bf16(wpb_ref[...]))).astype(_BF16)

    def finish(r0, mg):
        x = x_ref[r0:r0 + sub, :]
        y_ref[r0:r0 + sub, :] = _layer_norm(
            alpha * x + (1.0 + gate) * _dot(mg, _as_bf16(wo_ref[...])),
            vec_ref[3:4, :], vec_ref[4:5, :])

    r_a, r_b = starts
    u, v, z_a, b_g, c_g, h_b, z_b, g_a, g_b = project(modulated(r_a), range(9))
    hb = modulated(r_b)
    u2, v2, z_a2 = project(hb, range(0, 3))
    out_a = mixer_a(u, v, z_a)
    b_g2, c_g2, h_b2, z_b2 = project(hb, range(3, 7))
    out_b = mixer_b(b_g, c_g, h_b, z_b)
    g_a2, g_b2 = project(hb, range(7, 9))
    out_a2 = mixer_a(u2, v2, z_a2)
    mg = merge(out_a, out_b, g_a, g_b)
    out_b2 = mixer_b(b_g2, c_g2, h_b2, z_b2)
    finish(r_a, mg)
    finish(r_b, merge(out_a2, out_b2, g_a2, g_b2))

    z_scr[...] = tails[-1]
    conv_ref[0] = tails[-1][8 - (CONV_WIDTH - 1):8, :]


def _resident(shape):
    nd = len(shape)
    return pl.BlockSpec(shape, lambda i: (0,) * nd, pipeline_mode=pl.Buffered(1))


def _prompt_call(x, mod, win, wpa, wpb, wo, wsm, bias, vecs, alpha):
    nb, length, d = x.shape
    tm = ROW_TILE
    assert length % tm == 0 and tm == 2 * SUB_TILE and SUB_TILE % (2 * CHUNK) == 0
    assert d % GROUPS == 0 and d // GROUPS == CHUNK
    steps_per_seq = length // tm
    y, conv = pl.pallas_call(
        functools.partial(_prompt_kernel, alpha=alpha, steps_per_seq=steps_per_seq),
        grid=(nb * steps_per_seq,),
        in_specs=[
            pl.BlockSpec((tm, d), lambda i: (i, 0)),
            _resident(mod.shape),
            _resident(win.shape), _resident(wpa.shape), _resident(wpb.shape), _resident(wo.shape),
            _resident(wsm.shape), _resident(bias.shape), _resident(vecs.shape),
        ],
        out_specs=[
            pl.BlockSpec((tm, d), lambda i: (i, 0)),
            pl.BlockSpec((1, CONV_WIDTH - 1, d), lambda i: (i // steps_per_seq, 0, 0)),
        ],
        out_shape=[
            jax.ShapeDtypeStruct((nb * length, d), _F32),
            jax.ShapeDtypeStruct((nb, CONV_WIDTH - 1, d), _F32),
        ],
        scratch_shapes=[pltpu.VMEM((8, d), _F32)],
        compiler_params=pltpu.CompilerParams(
            dimension_semantics=("arbitrary",),
            vmem_limit_bytes=VMEM_LIMIT_BYTES),
        name="prompt_layer",
    )(x.reshape(nb * length, d), mod, win, wpa, wpb, wo, wsm, bias, vecs)
    return y.reshape(nb, length, d), conv


_PROJ_ORDER = (1, 0, 2, 4, 5, 3, 6, 7, 8)


def _sample_kernel(bs_ref, ws_ref, cs_ref, cp_ref, bc_ref,
                   alg_ref, alb_ref, wconv_ref, bconv_ref, lng_ref, lnb_ref,
                   x_hbm, st_hbm, ws_hbm, wc_hbm, win_hbm, wpa_hbm, wpb_hbm, wo_hbm,
                   y_hbm, conv_hbm, v_hbm, win_out, wpa_out, wpb_out, wo_out, modp_ref,
                   wsm_ref, bias_ref, vec_ref,
                   win_ref, wpa_ref, wpb_ref, wo_ref, stage, xbuf, stbuf, ybuf, cvbuf, vbuf, ws_smem,
                   sem, exp_sem, in_sem, res_sem, ws_sem, *, alpha):
    steps, n, d = xbuf.shape
    gd = d // GROUPS
    hist_rows = CONV_WIDTH - 1

    ws_copy = pltpu.make_async_copy(ws_hbm.at[:, 0:ws_smem.shape[1], :], ws_smem, ws_sem.at[0])
    ws_copy.start()

    in_copies = ([pltpu.make_async_copy(x_hbm.at[:, t, :], xbuf.at[t], in_sem.at[t])
                  for t in range(steps)]
                 + [pltpu.make_async_copy(st_hbm.at[:, k, :], stbuf.at[k], in_sem.at[steps + k])
                    for k in range(hist_rows)])
    for cp in in_copies:
        cp.start()

    blocks = ([(wc_hbm, None, None, j) for j in range(wc_hbm.shape[1] // d)]
              + [(win_hbm, win_ref, win_out, k) for k in _PROJ_ORDER]
              + [(wpa_hbm, wpa_ref, wpa_out, 0), (wpb_hbm, wpb_ref, wpb_out, 0),
                 (wo_hbm, wo_ref, wo_out, 0)])
    exports = []
    position = [0]

    def stage_copy(i):
        src, _, _, k = blocks[i]
        slot = i % STAGE_SLOTS
        return pltpu.make_async_copy(src.at[:, k * d:(k + 1) * d], stage.at[slot], sem.at[slot])

    def next_weight():
        i = position[0]
        position[0] += 1
        stage_copy(i).wait()
        if i + STAGE_SLOTS - 1 < len(blocks):
            stage_copy(i + STAGE_SLOTS - 1).start()
        w = stage[i % STAGE_SLOTS].astype(_BF16)
        _, keep, out, k = blocks[i]
        if keep is not None:
            cols = slice(k * d, (k + 1) * d)
            keep[:, cols] = pltpu.bitcast(w, _PACKED)
            cp = pltpu.make_async_copy(keep.at[:, cols], out.at[:, cols], exp_sem.at[len(exports)])
            cp.start()
            exports.append(cp)
        return w

    for i in range(STAGE_SLOTS - 1):
        stage_copy(i).start()

    c_all = jnp.concatenate([cs_ref[...], cp_ref[...]], axis=0).astype(_BF16)
    bc = bc_ref[...]
    mods = []
    for j in range(3):
        m = _dot(c_all, next_weight()) + bc[:, j * d:(j + 1) * d]
        modp_ref[:, j * d:(j + 1) * d] = m[n:, :]
        mods.append(m[0:n, :])
    shift, scale, gate = mods

    row = lax.broadcasted_iota(jnp.int32, (CHUNK, CHUNK), 0)
    col = lax.broadcasted_iota(jnp.int32, (CHUNK, CHUNK), 1)
    for g in range(GROUPS):
        wsm_ref[g] = pltpu.bitcast(jnp.where(col <= row, ws_ref[g], 0.0).astype(_BF16), _PACKED)

    for t in range(CHUNK):
        for g in range(GROUPS):
            bias_ref[t:t + 1, g * gd:(g + 1) * gd] = jnp.full((1, gd), bs_ref[g, t], _F32)
    for r, ref in enumerate((alg_ref, alb_ref, bconv_ref, lng_ref, lnb_ref)):
        vec_ref[r:r + 1, :] = ref[...]
    for k in range(CONV_WIDTH):
        vec_ref[5 + k:6 + k, :] = wconv_ref[:, k * d:(k + 1) * d]

    for cp in in_copies:
        cp.wait()
    ws_copy.wait()
    h = jnp.concatenate([(xbuf[t] * (1.0 + scale) + shift).astype(_BF16) for t in range(steps)],
                        axis=0)

    def rows(a, t):
        return a[t * n:(t + 1) * n, :]

    def proj():
        return _dot(h, next_weight())

    results = []

    def send(buf, t, dst):
        cp = pltpu.make_async_copy(buf.at[t], dst.at[:, t, :], res_sem.at[len(results)])
        cp.start()
        results.append(cp)

    vn = _layer_norm(proj(), alg_ref[...], alb_ref[...])
    s_rows = []
    for t in range(steps):
        vbuf[t] = rows(vn, t)
        send(vbuf, t, v_hbm)
        s_cols = []
        for g in range(GROUPS):
            cols = slice(g * gd, (g + 1) * gd)
            acc = jnp.full((n, gd), bs_ref[g, t], _F32)
            for jj in range(t + 1):
                acc = acc + ws_smem[g, t, jj] * rows(vn, jj)[:, cols]
            s_cols.append(acc)
        s_rows.append(jnp.concatenate(s_cols, axis=1))
    s = jnp.concatenate(s_rows, axis=0)
    u = proj()
    out_a = (u * s * _silu(proj())).astype(_BF16)

    z = proj()
    z = z * proj()
    hist = [stbuf[k] for k in range(hist_rows)] + [rows(z, t) for t in range(steps)]
    for k in range(hist_rows):
        cvbuf[k] = hist[steps + k]
        send(cvbuf, k, conv_hbm)
    wconv = [wconv_ref[:, k * d:(k + 1) * d] for k in range(CONV_WIDTH)]
    conv = jnp.concatenate(
        [bconv_ref[...] + sum(wconv[k] * hist[t + k] for k in range(CONV_WIDTH))
         for t in range(steps)], axis=0)
    b_g = proj()
    out_b = (b_g * conv * _silu(proj())).astype(_BF16)

    sg_a = jax.nn.sigmoid(proj())
    sg_b = jax.nn.sigmoid(proj())
    pa = _dot(out_a, next_weight())
    pb = _dot(out_b, next_weight())
    merged = (sg_a * pa + sg_b * pb).astype(_BF16)
    y = _dot(merged, next_weight())
    for t in range(steps):
        ybuf[t] = _layer_norm(alpha * xbuf[t] + (1.0 + gate) * rows(y, t), lng_ref[...], lnb_ref[...])
        send(ybuf, t, y_hbm)

    for cp in exports + results:
        cp.wait()


def _sample_call(x, state, c_s, c_p, w_c, b_c, win, wpa, wpb, wo, w_s, b_s,
                 alg, alb, wconv, bconv, lng, lnb, alpha):
    n, steps, d = x.shape
    nb = c_p.shape[0]
    hist_rows = CONV_WIDTH - 1
    assert hist_rows <= steps <= CHUNK and w_c.shape == (d, 3 * d)
    assert w_s.shape == (GROUPS, CHUNK, CHUNK) and d == GROUPS * CHUNK
    vmem = pl.BlockSpec(memory_space=pltpu.VMEM)
    smem = pl.BlockSpec(memory_space=pltpu.SMEM)
    hbm = pl.BlockSpec(memory_space=pl.ANY)
    weights = (win, wpa, wpb, wo)
    n_exports = sum(w.shape[1] // d for w in weights)
    n_results = 2 * steps + hist_rows
    ws_rows = -(-steps // 8) * 8
    y, conv, v, *weights_bf16, mod_p, wsm, bias, vecs = pl.pallas_call(
        functools.partial(_sample_kernel, alpha=alpha),
        in_specs=[smem] + [vmem] * 10 + [hbm] * 8,
        out_specs=[hbm] * (3 + len(weights)) + [vmem] * 4,
        out_shape=[
            jax.ShapeDtypeStruct((n, steps, d), _F32),
            jax.ShapeDtypeStruct((n, hist_rows, d), _F32),
            jax.ShapeDtypeStruct((n, steps, d), _F32),
        ] + [jax.ShapeDtypeStruct(_packed_shape(w), _PACKED) for w in weights] + [
            jax.ShapeDtypeStruct((nb, 3 * d), _F32),
            jax.ShapeDtypeStruct((GROUPS, CHUNK // 2, CHUNK), _PACKED),
            jax.ShapeDtypeStruct((CHUNK, d), _F32),
            jax.ShapeDtypeStruct((5 + CONV_WIDTH, d), _F32)],
        scratch_shapes=[pltpu.VMEM(_packed_shape(w), _PACKED) for w in weights] + [
            pltpu.VMEM((STAGE_SLOTS, d, d), _F32),
            pltpu.VMEM((steps, n, d), _F32), pltpu.VMEM((hist_rows, n, d), _F32),
            pltpu.VMEM((steps, n, d), _F32), pltpu.VMEM((hist_rows, n, d), _F32),
            pltpu.VMEM((steps, n, d), _F32),
            pltpu.SMEM((w_s.shape[0], ws_rows, w_s.shape[2]), _F32),
            pltpu.SemaphoreType.DMA((STAGE_SLOTS,)), pltpu.SemaphoreType.DMA((n_exports,)),
            pltpu.SemaphoreType.DMA((steps + hist_rows,)), pltpu.SemaphoreType.DMA((n_results,)),
            pltpu.SemaphoreType.DMA((1,))],
        compiler_params=pltpu.CompilerParams(vmem_limit_bytes=VMEM_LIMIT_BYTES),
        name="sample_layer",
    )(b_s, w_s, c_s, c_p, b_c.reshape(1, 3 * d), alg, alb, wconv, bconv, lng, lnb,
      x, state, w_s, w_c, win, wpa, wpb, wo)
    return y, conv, v, weights_bf16, mod_p, wsm, bias, vecs


def kernel(x_prompt, x_sample, state_conv, c_prompt, c_sample, w_c, b_c, w_in, a_ln_g, a_ln_b,
           w_s, b_s, w_conv, b_conv, w_pa, w_pb, w_o, ln_g, ln_b):
    depth = w_in.shape[0]
    d = x_prompt.shape[-1]
    alpha = (2.0 * depth) ** 0.25

    xp, xs = x_prompt, x_sample
    conv_p_rows, conv_s_rows, v_rows = [], [], []
    for l in range(depth):
        row = lambda a: a.reshape(1, d)
        small = (row(a_ln_g[l]), row(a_ln_b[l]), w_conv[l].reshape(1, CONV_WIDTH * d),
                 row(b_conv[l]), row(ln_g[l]), row(ln_b[l]))
        xs, conv_s, v_s, weights_bf16, mod_p, wsm, bias, vecs = _sample_call(
            xs, state_conv[l], c_sample, c_prompt, w_c[l], b_c[l],
            w_in[l], w_pa[l], w_pb[l], w_o[l], w_s[l], b_s[l], *small, alpha)
        xp, conv_p = _prompt_call(xp, mod_p, *weights_bf16, wsm, bias, vecs, alpha)
        conv_p_rows.append(conv_p)
        conv_s_rows.append(conv_s)
        v_rows.append(v_s)
    stack = (lambda rows: rows[0][None]) if depth == 1 else jnp.stack
    return (xp, xs, stack(conv_p_rows), stack(conv_s_rows), stack(v_rows))
```
